```python
import math
import jax, jax.numpy as jnp
from jax import lax
import numpy as np

D_MODEL = 1024
BATCH = 2
SEQ = 8192
DEPTH = 1

HG_HEADS = 4
HG_DIM = 128
HG_WIDTH = HG_HEADS * HG_DIM
CHUNK = 64
ATT_Q_HEADS = 8
ATT_KV_HEADS = 2
ATT_HEAD_DIM = 64
ATT_GROUP = ATT_Q_HEADS // ATT_KV_HEADS
ATT_WIDTH = ATT_Q_HEADS * ATT_HEAD_DIM
KV_WIDTH = ATT_KV_HEADS * ATT_HEAD_DIM
WINDOW = 128
BLOCK = 128
NUM_BUCKETS = 32
MAX_DISTANCE = 128
D_FF = 2816
N_MOD = 9
EPS = 1e-6
MIX_WIDTH = HG_WIDTH + ATT_WIDTH
IN_SPLITS = (HG_WIDTH, HG_WIDTH, HG_WIDTH, HG_WIDTH, HG_WIDTH, ATT_WIDTH, KV_WIDTH, KV_WIDTH)
D_IN = sum(IN_SPLITS)

kernel_name = "hybrid_hgrn2_swa_macaron_encoder"


def rmsnorm(x, g):
    x32 = x.astype(jnp.float32)
    y = x32 * lax.rsqrt(jnp.mean(x32 * x32, axis=-1, keepdims=True) + EPS)
    return (y * g.astype(jnp.float32)).astype(x.dtype)


def modulate(h, shift, scale):
    return h * (1.0 + scale[:, None, :]) + shift[:, None, :]


def swiglu(h, w_in, w_out):
    gate, up = jnp.split(h @ w_in, 2, axis=-1)
    return (jax.nn.silu(gate) * up) @ w_out


def split_columns(z):
    outs, off = [], 0
    for n in IN_SPLITS:
        outs.append(z[..., off:off + n])
        off += n
    return outs


def t5_bucket(rel):
    nb = NUM_BUCKETS // 2
    max_exact = nb // 2
    ret = (rel > 0).astype(jnp.int32) * nb
    n = jnp.abs(rel)
    large = max_exact + (jnp.log(jnp.maximum(n, 1).astype(jnp.float32) / max_exact)
                         / math.log(MAX_DISTANCE / max_exact) * (nb - max_exact)).astype(jnp.int32)
    large = jnp.minimum(large, nb - 1)
    return ret + jnp.where(n < max_exact, n, large)


def hgrn2_direction(q, k, v, g):
    B, H, S, DK = q.shape
    DV = v.shape[-1]
    n = S // CHUNK

    def to_chunks(t):
        return jnp.moveaxis(t.reshape(B, H, n, CHUNK, t.shape[-1]), 2, 0)

    causal = jnp.tril(jnp.ones((CHUNK, CHUNK), dtype=bool))[:, :, None]

    def step(state, inp):
        q_c, k_c, v_c, g_c = inp
        G = jnp.cumsum(g_c, axis=2)
        o_inter = jnp.einsum('bhtd,bhde->bhte', q_c * jnp.exp(G), state)
        diff = G[:, :, :, None, :] - G[:, :, None, :, :]
        decay = jnp.where(causal, jnp.exp(jnp.where(causal, diff, 0.0)), 0.0)
        A = jnp.einsum('bhtd,bhsd,bhtsd->bhts', q_c, k_c, decay)
        o_intra = jnp.einsum('bhts,bhse->bhte', A, v_c)
        G_last = G[:, :, -1:, :]
        k_dec = k_c * jnp.exp(G_last - G)
        new_state = jnp.exp(G_last[:, :, 0, :])[..., None] * state + jnp.einsum('bhsd,bhse->bhde', k_dec, v_c)
        return new_state, o_inter + o_intra

    s0 = jnp.zeros((B, H, DK, DV), jnp.float32)
    _, o = lax.scan(step, s0, (to_chunks(q), to_chunks(k), to_chunks(v), to_chunks(g)))
    return jnp.moveaxis(o, 0, 2).reshape(B, H, S, DV)


def hgrn2_group(q_raw, ff_raw, fb_raw, i_raw, g_raw, lb, norm_g):
    B, S, _ = q_raw.shape

    def heads(t):
        return t.astype(jnp.float32).reshape(B, S, HG_HEADS, HG_DIM).transpose(0, 2, 1, 3)

    q = heads(jax.nn.silu(q_raw))
    v = heads(i_raw)
    lb = lb.astype(jnp.float32)
    outs = []
    for d, f_raw in enumerate((ff_raw, fb_raw)):
        fr = heads(f_raw)
        lb_d = lb[d].reshape(1, HG_HEADS, 1, HG_DIM)
        log_f = jnp.logaddexp(jnp.log(lb_d), jnp.log1p(-lb_d) + jax.nn.log_sigmoid(fr))
        k = (1.0 - lb_d) * jax.nn.sigmoid(-fr)
        if d == 0:
            outs.append(hgrn2_direction(q, k, v, log_f))
        else:
            flip = lambda t: jnp.flip(t, axis=2)
            outs.append(flip(hgrn2_direction(flip(q), flip(k), flip(v), flip(log_f))))
    o = (outs[0] + outs[1]).transpose(0, 2, 1, 3)
    o = o * lax.rsqrt(jnp.mean(o * o, axis=-1, keepdims=True) + EPS)
    o = o.reshape(B, S, HG_WIDTH) * norm_g.astype(jnp.float32)
    return (o * jax.nn.silu(g_raw.astype(jnp.float32))).astype(q_raw.dtype)


def windowed_gqa(q_raw, k_raw, v_raw, q_g, k_g, sink, rel_bias):
    B, S, _ = q_raw.shape
    nb = S // BLOCK
    q = rmsnorm(q_raw.reshape(B, S, ATT_Q_HEADS, ATT_HEAD_DIM), q_g)
    k = rmsnorm(k_raw.reshape(B, S, ATT_KV_HEADS, ATT_HEAD_DIM), k_g)
    v = v_raw.reshape(B, S, ATT_KV_HEADS, ATT_HEAD_DIM)
    qb = q.reshape(B, nb, BLOCK, ATT_KV_HEADS, ATT_GROUP, ATT_HEAD_DIM)

    def band(t):
        tp = jnp.pad(t, ((0, 0), (BLOCK, BLOCK), (0, 0), (0, 0))).reshape(B, nb + 2, BLOCK, ATT_KV_HEADS, ATT_HEAD_DIM)
        return jnp.concatenate([tp[:, :-2], tp[:, 1:-1], tp[:, 2:]], axis=2)

    kw, vw = band(k), band(v)
    logits = jnp.einsum('bnqhgd,bnkhd->bnhgqk', qb, kw).astype(jnp.float32) / math.sqrt(ATT_HEAD_DIM)
    rel = (jnp.arange(3 * BLOCK)[None, :] - BLOCK) - jnp.arange(BLOCK)[:, None]
    bias = rel_bias.astype(jnp.float32)[t5_bucket(rel)]
    bias = bias.reshape(BLOCK, 3 * BLOCK, ATT_KV_HEADS, ATT_GROUP).transpose(2, 3, 0, 1)
    key_pos = jnp.arange(nb)[:, None] * BLOCK + jnp.arange(3 * BLOCK)[None, :] - BLOCK
    valid = (jnp.abs(rel) <= WINDOW)[None] & ((key_pos >= 0) & (key_pos < S))[:, None, :]
    logits = jnp.where(valid[None, :, None, None], logits + bias, -jnp.inf)
    sink_b = sink.astype(jnp.float32).reshape(ATT_KV_HEADS, ATT_GROUP, 1, 1)
    m = jnp.maximum(jnp.max(logits, axis=-1, keepdims=True), sink_b)
    p = jnp.exp(logits - m)
    p = p / (jnp.sum(p, axis=-1, keepdims=True) + jnp.exp(sink_b - m))
    o = jnp.einsum('bnhgqk,bnkhd->bnqhgd', p.astype(vw.dtype), vw)
    return o.reshape(B, S, ATT_WIDTH)


def setup_inputs(seed: int = 0) -> dict:
    key = jax.random.key(seed)
    ks = jax.random.split(key, 16)
    L, D, F = DEPTH, D_MODEL, D_FF
    nrm = jax.random.normal
    return {
        "x": nrm(ks[0], (BATCH, SEQ, D), jnp.float32),
        "c": nrm(ks[1], (BATCH, D), jnp.float32),
        "w_ada": nrm(ks[2], (L, D, N_MOD * D), jnp.float32) * 0.02,
        "b_ada": nrm(ks[3], (L, N_MOD * D), jnp.float32) * 0.02,
        "norm_g": 1.0 + 0.02 * nrm(ks[4], (L, 3, D), jnp.float32),
        "w_ffn1_in": nrm(ks[5], (L, D, 2 * F), jnp.float32) * D ** -0.5,
        "w_ffn1_out": nrm(ks[6], (L, F, D), jnp.float32) * F ** -0.5,
        "w_ffn2_in": nrm(ks[7], (L, D, 2 * F), jnp.float32) * D ** -0.5,
        "w_ffn2_out": nrm(ks[8], (L, F, D), jnp.float32) * F ** -0.5,
        "w_mix_in": nrm(ks[9], (L, D, D_IN), jnp.float32) * D ** -0.5,
        "w_mix_out": nrm(ks[10], (L, MIX_WIDTH, D), jnp.float32) * MIX_WIDTH ** -0.5,
        "hgrn_lb": nrm(ks[11], (2, L + 1, HG_WIDTH), jnp.float32),
        "hgrn_norm_g": 1.0 + 0.02 * nrm(ks[12], (L, HG_WIDTH), jnp.float32),
        "qk_norm_g": 1.0 + 0.02 * nrm(ks[13], (L, 2, ATT_HEAD_DIM), jnp.float32),
        "attn_sink": 0.5 * nrm(ks[14], (L, ATT_Q_HEADS), jnp.float32),
        "rel_bias": 0.5 * nrm(ks[15], (NUM_BUCKETS, ATT_Q_HEADS), jnp.float32),
    }


def reference(x, c, w_ada, b_ada, norm_g, w_ffn1_in, w_ffn1_out, w_ffn2_in, w_ffn2_out,
              w_mix_in, w_mix_out, hgrn_lb, hgrn_norm_g, qk_norm_g, attn_sink, rel_bias):
    lb_all = jnp.cumsum(jax.nn.softmax(hgrn_lb.astype(jnp.float32), axis=1), axis=1)
    c_act = jax.nn.silu(c)
    for l in range(DEPTH):
        mods = jnp.split(c_act @ w_ada[l] + b_ada[l], N_MOD, axis=-1)
        sh1, sc1, g1, sh2, sc2, g2, sh3, sc3, g3 = mods
        h = modulate(rmsnorm(x, norm_g[l, 0]), sh1, sc1)
        x = x + 0.5 * g1[:, None, :] * swiglu(h, w_ffn1_in[l], w_ffn1_out[l])
        h = modulate(rmsnorm(x, norm_g[l, 1]), sh2, sc2)
        hq, hf_f, hf_b, hi, hg, aq, ak, av = split_columns(h @ w_mix_in[l])
        o_hgrn = hgrn2_group(hq, hf_f, hf_b, hi, hg, lb_all[:, l], hgrn_norm_g[l])
        o_attn = windowed_gqa(aq, ak, av, qk_norm_g[l, 0], qk_norm_g[l, 1], attn_sink[l], rel_bias)
        mixed = jnp.concatenate([o_hgrn, o_attn.astype(o_hgrn.dtype)], axis=-1) @ w_mix_out[l]
        x = x + g2[:, None, :] * mixed
        h = modulate(rmsnorm(x, norm_g[l, 2]), sh3, sc3)
        x = x + 0.5 * g3[:, None, :] * swiglu(h, w_ffn2_in[l], w_ffn2_out[l])
    return x
```

```python
import functools
import math

import numpy as np
import jax
import jax.numpy as jnp
from jax import lax
from jax.experimental import pallas as pl
from jax.experimental.pallas import tpu as pltpu

F32 = jnp.float32
BF16 = jnp.bfloat16

D_MODEL = 1024
HG_HEADS = 4
HG_DIM = 128
HG_WIDTH = HG_HEADS * HG_DIM
ATT_Q_HEADS = 8
ATT_KV_HEADS = 2
ATT_HEAD_DIM = 64
ATT_GROUP = ATT_Q_HEADS // ATT_KV_HEADS
ATT_WIDTH = ATT_Q_HEADS * ATT_HEAD_DIM
KV_WIDTH = ATT_KV_HEADS * ATT_HEAD_DIM
WINDOW = 128
BLOCK = 128
NUM_BUCKETS = 32
MAX_DISTANCE = 128
D_FF = 2816
N_MOD = 9
EPS = 1e-6
D_IN = 5 * HG_WIDTH + ATT_WIDTH + 2 * KV_WIDTH

LANES = 128
SUBLANES = 8
VMEM_LIMIT = 56 * 1024 * 1024

FFN_TM = 256
HG_CHUNK = 128
NEG_INF = float("-inf")


def _cparams(sem):
    return pltpu.CompilerParams(dimension_semantics=sem, vmem_limit_bytes=VMEM_LIMIT)


def _sigmoid(x):
    return 1.0 / (1.0 + jnp.exp(-x))


def _ada_kernel(c_ref, w_ref, b_ref, o_ref):
    c = c_ref[...]
    ca = c * _sigmoid(c)
    o_ref[...] = jnp.dot(ca, w_ref[...], precision=lax.Precision.HIGHEST,
                         preferred_element_type=F32) + b_ref[...]


def _ada(c_pad, w, b):
    rows, d = c_pad.shape
    n = w.shape[1]
    tn = 1024
    return pl.pallas_call(
        _ada_kernel,
        grid=(n // tn,),
        in_specs=[pl.BlockSpec((rows, d), lambda j: (0, 0)),
                  pl.BlockSpec((d, tn), lambda j: (0, j)),
                  pl.BlockSpec((1, tn), lambda j: (0, j))],
        out_specs=pl.BlockSpec((rows, tn), lambda j: (0, j)),
        out_shape=jax.ShapeDtypeStruct((rows, n), F32),
        compiler_params=_cparams(("arbitrary",)),
        name="ada",
    )(c_pad, w, b)


def _t5_bucket_np(rel):
    nb = NUM_BUCKETS // 2
    max_exact = nb // 2
    ret = (rel > 0).astype(np.int32) * nb
    n = np.abs(rel)
    ratio = np.maximum(n, 1).astype(np.float32) / np.float32(max_exact)
    large = max_exact + (np.log(ratio) / np.float32(math.log(MAX_DISTANCE / max_exact))
                         * np.float32(nb - max_exact)).astype(np.int32)
    large = np.minimum(large, nb - 1)
    return ret + np.where(n < max_exact, n, large)


def _bias_kernel(oh_ref, msk_ref, rb_ref, o_ref):
    b = jnp.dot(oh_ref[...], rb_ref[...], precision=lax.Precision.HIGHEST,
                preferred_element_type=F32)
    o_ref[...] = b + msk_ref[...]


def _bias_tile(rel_bias):
    rel = (np.arange(3 * BLOCK)[None, :] - BLOCK) - np.arange(BLOCK)[:, None]
    bucket = _t5_bucket_np(rel).reshape(-1)
    onehot = (bucket[:, None] == np.arange(NUM_BUCKETS)[None, :]).astype(np.float32)
    mask = np.where(np.abs(rel) <= WINDOW, 0.0, NEG_INF).astype(np.float32).reshape(-1, 1)
    nrow = BLOCK * 3 * BLOCK
    tr = 4096
    out = pl.pallas_call(
        _bias_kernel,
        grid=(nrow // tr,),
        in_specs=[pl.BlockSpec((tr, NUM_BUCKETS), lambda i: (i, 0)),
                  pl.BlockSpec((tr, 1), lambda i: (i, 0)),
                  pl.BlockSpec((NUM_BUCKETS, ATT_Q_HEADS), lambda i: (0, 0))],
        out_specs=pl.BlockSpec((tr, ATT_Q_HEADS), lambda i: (i, 0)),
        out_shape=jax.ShapeDtypeStruct((nrow, ATT_Q_HEADS), F32),
        compiler_params=_cparams(("arbitrary",)),
        name="bias_tile",
    )(jnp.asarray(onehot), jnp.asarray(mask), rel_bias.astype(F32))
    out = out.reshape(BLOCK, 3 * BLOCK, ATT_Q_HEADS).transpose(2, 0, 1)
    return out.reshape(ATT_Q_HEADS // 2, 2 * BLOCK, 3 * BLOCK)


def _norm_mod(x, ng, sh, sc):
    ms = jnp.mean(x * x, axis=-1, keepdims=True)
    y = x * lax.rsqrt(ms + EPS) * ng
    return y * (1.0 + sc) + sh


def _ffn_kernel(x_ref, sh_ref, sc_ref, gt_ref, ng_ref, win_ref, wout_ref, o_ref):
    x = x_ref[0]
    h = _norm_mod(x, ng_ref[...], sh_ref[0], sc_ref[0]).astype(BF16)
    gu = jnp.dot(h, win_ref[...], preferred_element_type=F32)
    g = gu[:, :D_FF]
    u = gu[:, D_FF:]
    act = (g * _sigmoid(g) * u).astype(BF16)
    y = jnp.dot(act, wout_ref[...], preferred_element_type=F32)
    o_ref[0] = x + 0.5 * gt_ref[0] * y


def _ffn(x, sh, sc, gt, ng, w_in, w_out):
    b, s, d = x.shape
    tm = FFN_TM
    vec = pl.BlockSpec((1, 1, d), lambda i, j: (i, 0, 0))
    return pl.pallas_call(
        _ffn_kernel,
        grid=(b, s // tm),
        in_specs=[pl.BlockSpec((1, tm, d), lambda i, j: (i, j, 0)),
                  vec, vec, vec,
                  pl.BlockSpec((1, d), lambda i, j: (0, 0)),
                  pl.BlockSpec(w_in.shape, lambda i, j: (0, 0)),
                  pl.BlockSpec(w_out.shape, lambda i, j: (0, 0))],
        out_specs=pl.BlockSpec((1, tm, d), lambda i, j: (i, j, 0)),
        out_shape=jax.ShapeDtypeStruct(x.shape, F32),
        compiler_params=_cparams(("arbitrary", "arbitrary")),
        name="ffn",
    )(x, sh, sc, gt, ng, w_in, w_out)


def _mixin_kernel(x_ref, sh_ref, sc_ref, ng_ref, w_ref, o_ref):
    h = _norm_mod(x_ref[0], ng_ref[...], sh_ref[0], sc_ref[0]).astype(BF16)
    o_ref[0] = jnp.dot(h, w_ref[...], preferred_element_type=F32)


def _mixin(x, sh, sc, ng, w):
    b, s, d = x.shape
    n = w.shape[1]
    tm = FFN_TM
    vec = pl.BlockSpec((1, 1, d), lambda i, j: (i, 0, 0))
    return pl.pallas_call(
        _mixin_kernel,
        grid=(b, s // tm),
        in_specs=[pl.BlockSpec((1, tm, d), lambda i, j: (i, j, 0)),
                  vec, vec,
                  pl.BlockSpec((1, d), lambda i, j: (0, 0)),
                  pl.BlockSpec(w.shape, lambda i, j: (0, 0))],
        out_specs=pl.BlockSpec((1, tm, n), lambda i, j: (i, j, 0)),
        out_shape=jax.ShapeDtypeStruct((b, s, n), F32),
        compiler_params=_cparams(("arbitrary", "arbitrary")),
        name="mix_in",
    )(x, sh, sc, ng, w)


def _hgrn_levels(c):
    return [1 << i for i in range(int(math.log2(c)))]


def _hgrn_consts(c):
    t = np.arange(c)[:, None]
    s = np.arange(c)[None, :]
    x = t ^ s
    lev = np.where(x > 0, np.floor(np.log2(np.maximum(x, 1))), -1).astype(np.int32)
    lev_f = np.where(t > s, lev, np.where(t == s, -1, -2)).astype(np.int32)
    lev_b = lev_f.T.copy()
    tri_f = (s <= t).astype(np.float32)
    tri_b = (s >= t).astype(np.float32)
    return (jnp.asarray(np.stack([lev_f, lev_b])),
            jnp.asarray(np.stack([tri_f, tri_b]), dtype=BF16))


def _ref_rows(g_cum, w, rev, c):
    idx = w if rev else w - 1
    if 2 * w >= SUBLANES:
        g3 = g_cum.reshape(c // (2 * w), 2 * w, LANES)
        m = jnp.broadcast_to(g3[:, idx:idx + 1, :], g3.shape)
        return m.reshape(c, LANES)
    g3 = g_cum.reshape(c // SUBLANES, SUBLANES, LANES)
    sub = lax.broadcasted_iota(jnp.int32, g3.shape, 1)
    m = None
    for p in range(SUBLANES // (2 * w)):
        r = jnp.broadcast_to(g3[:, p * 2 * w + idx:p * 2 * w + idx + 1, :], g3.shape)
        m = r if m is None else jnp.where(sub >= p * 2 * w, r, m)
    return m.reshape(c, LANES)


def _split3(x):
    hi = x.astype(BF16)
    r1 = x - hi.astype(F32)
    mid = r1.astype(BF16)
    lo = (r1 - mid.astype(F32)).astype(BF16)
    return hi, mid, lo


def _hgrn_chunk(q_raw, fr, v, lb, tri, lev, st_ref, rev, c):
    e = jnp.exp(-jnp.abs(fr))
    r = 1.0 / (1.0 + e)
    pos = fr >= 0
    sig = jnp.where(pos, r, e * r)
    sig_n = jnp.where(pos, e * r, r)
    f = lb + (1.0 - lb) * sig
    g = jnp.log(f)
    k = (1.0 - lb) * sig_n
    q = q_raw * _sigmoid(q_raw)

    hi, mid, lo = _split3(g)
    g_cum = (jnp.dot(tri, hi, preferred_element_type=F32)
             + jnp.dot(tri, mid, preferred_element_type=F32)
             + jnp.dot(tri, lo, preferred_element_type=F32))
    g_last = g_cum[0:1, :] if rev else g_cum[c - 1:c, :]

    vb = v.astype(BF16)
    st = st_ref[...]
    nt = (((1,), (1,)), ((), ()))
    tn = (((0,), (0,)), ((), ()))
    qi = (q * jnp.exp(g_cum)).astype(BF16)
    o = lax.dot_general(qi, st.astype(BF16), nt, preferred_element_type=F32)

    kd = (k * jnp.exp(g_last - g_cum)).astype(BF16)
    u_t = lax.dot_general(vb, kd, tn, preferred_element_type=F32)
    st_ref[...] = jnp.exp(g_last) * st + u_t

    a = jnp.where(lev == -1,
                  lax.dot_general(q.astype(BF16), k.astype(BF16), nt, preferred_element_type=F32),
                  0.0)
    for li, w in enumerate(_hgrn_levels(c)):
        x = jnp.exp(-jnp.abs(g_cum - _ref_rows(g_cum, w, rev, c)))
        p = lax.dot_general((q * x).astype(BF16), (k * x).astype(BF16), nt,
                            preferred_element_type=F32)
        a = jnp.where(lev == li, p, a)
    return o + jnp.dot(a.astype(BF16), vb, preferred_element_type=F32)


def _hgrn_kernel(qf_ref, qb_ref, ff_ref, fb_ref, vf_ref, vb_ref, lb_ref, lev_ref, tri_ref,
                 of_ref, ob_ref, stf_ref, stb_ref, *, c, layer):
    @pl.when(pl.program_id(2) == 0)
    def _():
        stf_ref[...] = jnp.zeros_like(stf_ref)
        stb_ref[...] = jnp.zeros_like(stb_ref)

    raw = lb_ref[...]
    mx = jnp.max(raw, axis=1, keepdims=True)
    ex = jnp.exp(raw - mx)
    lb = jnp.sum(ex[:, :layer + 1, :], axis=1) / jnp.sum(ex, axis=1)

    of_ref[0] = _hgrn_chunk(qf_ref[0], ff_ref[0], vf_ref[0], lb[0:1], tri_ref[0], lev_ref[0],
                            stf_ref, False, c)
    ob_ref[0] = _hgrn_chunk(qb_ref[0], fb_ref[0], vb_ref[0], lb[1:2], tri_ref[1], lev_ref[1],
                            stb_ref, True, c)


def _hgrn(z, hgrn_lb, layer):
    b, s, _ = z.shape
    c = HG_CHUNK
    nc = s // c
    lev, tri = _hgrn_consts(c)
    nl = hgrn_lb.shape[1]

    def col(section, rev):
        if rev:
            return pl.BlockSpec((1, c, LANES), lambda i, h, j: (i, nc - 1 - j, section * HG_HEADS + h))
        return pl.BlockSpec((1, c, LANES), lambda i, h, j: (i, j, section * HG_HEADS + h))

    out_f = pl.BlockSpec((1, c, LANES), lambda i, h, j: (i, j, h))
    out_b = pl.BlockSpec((1, c, LANES), lambda i, h, j: (i, nc - 1 - j, h))
    const3 = lambda shape: pl.BlockSpec(shape, lambda i, h, j: (0, 0, 0))
    return pl.pallas_call(
        functools.partial(_hgrn_kernel, c=c, layer=layer),
        grid=(b, HG_HEADS, nc),
        in_specs=[col(0, False), col(0, True), col(1, False), col(2, True),
                  col(3, False), col(3, True),
                  pl.BlockSpec((2, nl, LANES), lambda i, h, j: (0, 0, h)),
                  const3((2, c, c)), const3((2, c, c))],
        out_specs=[out_f, out_b],
        out_shape=[jax.ShapeDtypeStruct((b, s, HG_WIDTH), F32)] * 2,
        scratch_shapes=[pltpu.VMEM((HG_DIM, HG_DIM), F32)] * 2,
        compiler_params=_cparams(("arbitrary", "arbitrary", "arbitrary")),
        name="hgrn",
    )(z, z, z, z, z, z, hgrn_lb, lev, tri)


def _half_norm(x, gain, lo):
    x2 = x * x
    s_lo = jnp.sum(jnp.where(lo, x2, 0.0), axis=-1, keepdims=True)
    s_hi = jnp.sum(jnp.where(lo, 0.0, x2), axis=-1, keepdims=True)
    ms = jnp.where(lo, s_lo, s_hi) * (1.0 / ATT_HEAD_DIM)
    return x * lax.rsqrt(ms + EPS) * gain


def _attn_kernel(sink_ref, q_ref, kp_ref, ko_ref, kn_ref, vp_ref, vo_ref, vn_ref,
                 qg_ref, kg_ref, bias_ref, o_ref, *, nb):
    n = pl.program_id(1)
    half = LANES // 2
    kb = jnp.concatenate([kp_ref[0], ko_ref[0], kn_ref[0]], axis=0)
    vb = jnp.concatenate([vp_ref[0], vo_ref[0], vn_ref[0]], axis=0)
    lo_k = lax.broadcasted_iota(jnp.int32, kb.shape, 1) < half
    lo_q = lax.broadcasted_iota(jnp.int32, (BLOCK, LANES), 1) < half

    kn = _half_norm(kb, kg_ref[...], lo_k)
    kr = pltpu.roll(kn, half, axis=1)
    k2 = (jnp.where(lo_k, kn, kr).astype(BF16), jnp.where(lo_k, kr, kn).astype(BF16))
    v0lo = jnp.where(lo_k, vb, 0.0)
    v1hi = jnp.where(lo_k, 0.0, vb)
    v_lo = (v0lo.astype(BF16), pltpu.roll(v1hi, half, axis=1).astype(BF16))
    v_hi = (pltpu.roll(v0lo, half, axis=1).astype(BF16), v1hi.astype(BF16))

    kpos = n * BLOCK - BLOCK + lax.broadcasted_iota(jnp.int32, (1, 3 * BLOCK), 1)
    kvalid = (kpos >= 0) & (kpos < nb * BLOCK)
    row_lo = lax.broadcasted_iota(jnp.int32, (2 * BLOCK, 1), 0) < BLOCK
    nt = (((1,), (1,)), ((), ()))
    scale = 1.0 / math.sqrt(ATT_HEAD_DIM)

    for m in range(ATT_Q_HEADS // 2):
        j = (2 * m) // ATT_GROUP
        qp = _half_norm(q_ref[0, :, m * LANES:(m + 1) * LANES], qg_ref[...], lo_q) * scale
        q2 = jnp.concatenate([jnp.where(lo_q, qp, 0.0), jnp.where(lo_q, 0.0, qp)],
                             axis=0).astype(BF16)
        logits = lax.dot_general(q2, k2[j], nt, preferred_element_type=F32)
        logits = jnp.where(kvalid, logits + bias_ref[m], NEG_INF)
        sink = jnp.where(row_lo, sink_ref[2 * m], sink_ref[2 * m + 1])
        mx = jnp.maximum(jnp.max(logits, axis=-1, keepdims=True), sink)
        p = jnp.exp(logits - mx)
        den = jnp.sum(p, axis=-1, keepdims=True) + jnp.exp(sink - mx)
        p = (p / den).astype(BF16)
        o = (jnp.dot(p[:BLOCK], v_lo[j], preferred_element_type=F32)
             + jnp.dot(p[BLOCK:], v_hi[j], preferred_element_type=F32))
        o_ref[0, :, m * LANES:(m + 1) * LANES] = o


def _attn(z, q_gain, k_gain, sink, bias):
    b, s, _ = z.shape
    nb = s // BLOCK
    q_col = (5 * HG_WIDTH) // ATT_WIDTH
    k_col = (5 * HG_WIDTH + ATT_WIDTH) // KV_WIDTH
    v_col = k_col + 1

    def kv(col, shift):
        return pl.BlockSpec(
            (1, BLOCK, KV_WIDTH),
            lambda i, n: (i, jnp.clip(n + shift, 0, nb - 1), col))

    return pl.pallas_call(
        functools.partial(_attn_kernel, nb=nb),
        grid=(b, nb),
        in_specs=[pl.BlockSpec(memory_space=pltpu.SMEM),
                  pl.BlockSpec((1, BLOCK, ATT_WIDTH), lambda i, n: (i, n, q_col)),
                  kv(k_col, -1), kv(k_col, 0), kv(k_col, 1),
                  kv(v_col, -1), kv(v_col, 0), kv(v_col, 1),
                  pl.BlockSpec((1, LANES), lambda i, n: (0, 0)),
                  pl.BlockSpec((1, LANES), lambda i, n: (0, 0)),
                  pl.BlockSpec(bias.shape, lambda i, n: (0, 0, 0))],
        out_specs=pl.BlockSpec((1, BLOCK, ATT_WIDTH), lambda i, n: (i, n, 0)),
        out_shape=jax.ShapeDtypeStruct((b, s, ATT_WIDTH), F32),
        compiler_params=_cparams(("arbitrary", "arbitrary")),
        name="attn",
    )(sink, z, z, z, z, z, z, z, q_gain, k_gain, bias)


def _mixout_kernel(x_ref, of_ref, ob_ref, hg_ref, at_ref, gt_ref, ng_ref, w_ref, o_ref):
    o = of_ref[0] + ob_ref[0]
    parts = []
    for h in range(HG_HEADS):
        oh = o[:, h * HG_DIM:(h + 1) * HG_DIM]
        ms = jnp.mean(oh * oh, axis=-1, keepdims=True)
        parts.append(oh * lax.rsqrt(ms + EPS))
    o = jnp.concatenate(parts, axis=-1) * ng_ref[...]
    hg = hg_ref[0]
    o = o * (hg * _sigmoid(hg))
    mixed = (jnp.dot(o.astype(BF16), w_ref[:HG_WIDTH, :], preferred_element_type=F32)
             + jnp.dot(at_ref[0].astype(BF16), w_ref[HG_WIDTH:, :], preferred_element_type=F32))
    o_ref[0] = x_ref[0] + gt_ref[0] * mixed


def _mixout(x, o_f, o_b, z, attn, gt, ng, w):
    b, s, d = x.shape
    tm = FFN_TM
    half = pl.BlockSpec((1, tm, HG_WIDTH), lambda i, j: (i, j, 0))
    return pl.pallas_call(
        _mixout_kernel,
        grid=(b, s // tm),
        in_specs=[pl.BlockSpec((1, tm, d), lambda i, j: (i, j, 0)),
                  half, half,
                  pl.BlockSpec((1, tm, HG_WIDTH), lambda i, j: (i, j, 4)),
                  half,
                  pl.BlockSpec((1, 1, d), lambda i, j: (i, 0, 0)),
                  pl.BlockSpec((1, HG_WIDTH), lambda i, j: (0, 0)),
                  pl.BlockSpec(w.shape, lambda i, j: (0, 0))],
        out_specs=pl.BlockSpec((1, tm, d), lambda i, j: (i, j, 0)),
        out_shape=jax.ShapeDtypeStruct(x.shape, F32),
        compiler_params=_cparams(("arbitrary", "arbitrary")),
        name="mix_out",
    )(x, o_f, o_b, z, attn, gt, ng, w)


def kernel(x, c, w_ada, b_ada, norm_g, w_ffn1_in, w_ffn1_out, w_ffn2_in, w_ffn2_out,
           w_mix_in, w_mix_out, hgrn_lb, hgrn_norm_g, qk_norm_g, attn_sink, rel_bias):
    b, s, d = x.shape
    depth = w_ada.shape[0]
    bias = _bias_tile(rel_bias)
    c_pad = jnp.pad(c.astype(F32), ((0, SUBLANES - b), (0, 0)))
    for l in range(depth):
        mods = _ada(c_pad, w_ada[l], b_ada[l][None, :])[:b]
        sh1, sc1, g1, sh2, sc2, g2, sh3, sc3, g3 = [
            mods[:, i * d:(i + 1) * d][:, None, :] for i in range(N_MOD)]
        x = _ffn(x, sh1, sc1, g1, norm_g[l, 0][None, :],
                 w_ffn1_in[l].astype(BF16), w_ffn1_out[l].astype(BF16))
        z = _mixin(x, sh2, sc2, norm_g[l, 1][None, :], w_mix_in[l].astype(BF16))
        o_f, o_b = _hgrn(z, hgrn_lb, l)
        qg = jnp.tile(qk_norm_g[l, 0], 2)[None, :]
        kg = jnp.tile(qk_norm_g[l, 1], 2)[None, :]
        att = _attn(z, qg, kg, attn_sink[l], bias)
        x = _mixout(x, o_f, o_b, z, att, g2, hgrn_norm_g[l][None, :], w_mix_out[l].astype(BF16))
        x = _ffn(x, sh3, sc3, g3, norm_g[l, 2][None, :],
                 w_ffn2_in[l].astype(BF16), w_ffn2_out[l].astype(BF16))
    return x
```

```python
import functools
import math

import numpy as np
import jax
import jax.numpy as jnp
from jax import lax
from jax.experimental import pallas as pl
from jax.experimental.pallas import tpu as pltpu

F32 = jnp.float32
BF16 = jnp.bfloat16

D_MODEL = 1024
HG_HEADS = 4
HG_DIM = 128
HG_WIDTH = HG_HEADS * HG_DIM
ATT_Q_HEADS = 8
ATT_KV_HEADS = 2
ATT_HEAD_DIM = 64
ATT_GROUP = ATT_Q_HEADS // ATT_KV_HEADS
ATT_WIDTH = ATT_Q_HEADS * ATT_HEAD_DIM
KV_WIDTH = ATT_KV_HEADS * ATT_HEAD_DIM
WINDOW = 128
BLOCK = 128
NUM_BUCKETS = 32
MAX_DISTANCE = 128
D_FF = 2816
N_MOD = 9
EPS = 1e-6

LANES = 128
SUBLANES = 8
VMEM_LIMIT = 56 * 1024 * 1024

FFN_TM = 256
HG_CHUNK = 128
NEG_INF = float("-inf")
LOG2E = 1.0 / math.log(2.0)

ZB_Q, ZB_I, ZB_G, ZB_AQ = 0, HG_WIDTH, 2 * HG_WIDTH, 3 * HG_WIDTH
ZB_AK = ZB_AQ + ATT_WIDTH
ZB_AV = ZB_AK + KV_WIDTH
ZB_WIDTH = ZB_AV + KV_WIDTH
ZF_WIDTH = 2 * HG_WIDTH


def _cparams(sem):
    return pltpu.CompilerParams(dimension_semantics=sem, vmem_limit_bytes=VMEM_LIMIT)


def _sigmoid(x):
    return 1.0 / (1.0 + jnp.exp(-x))


def _ada_kernel(c_ref, w_ref, b_ref, o_ref):
    c = c_ref[...]
    ca = c * _sigmoid(c)
    o_ref[...] = jnp.dot(ca, w_ref[...], precision=lax.Precision.HIGHEST,
                         preferred_element_type=F32) + b_ref[...]


def _ada(c_pad, w, b):
    rows, d = c_pad.shape
    n = w.shape[1]
    tn = 1024
    return pl.pallas_call(
        _ada_kernel,
        grid=(n // tn,),
        in_specs=[pl.BlockSpec((rows, d), lambda j: (0, 0)),
                  pl.BlockSpec((d, tn), lambda j: (0, j)),
                  pl.BlockSpec((1, tn), lambda j: (0, j))],
        out_specs=pl.BlockSpec((rows, tn), lambda j: (0, j)),
        out_shape=jax.ShapeDtypeStruct((rows, n), F32),
        compiler_params=_cparams(("arbitrary",)),
        name="ada",
    )(c_pad, w, b)


def _t5_bucket_np(rel):
    nb = NUM_BUCKETS // 2
    max_exact = nb // 2
    ret = (rel > 0).astype(np.int32) * nb
    n = np.abs(rel)
    ratio = np.maximum(n, 1).astype(np.float32) / np.float32(max_exact)
    large = max_exact + (np.log(ratio) / np.float32(math.log(MAX_DISTANCE / max_exact))
                         * np.float32(nb - max_exact)).astype(np.int32)
    large = np.minimum(large, nb - 1)
    return ret + np.where(n < max_exact, n, large)


def _bias_kernel(rbt_ref, oh_ref, msk_ref, o_ref):
    b = jnp.dot(rbt_ref[...], oh_ref[...], precision=lax.Precision.HIGHEST,
                preferred_element_type=F32)
    o_ref[...] = b + msk_ref[...]


def _bias_tile(rel_bias):
    rel = (np.arange(3 * BLOCK)[None, :] - BLOCK) - np.arange(BLOCK)[:, None]
    bucket = _t5_bucket_np(rel).reshape(-1)
    onehot_t = (np.arange(NUM_BUCKETS)[:, None] == bucket[None, :]).astype(np.float32)
    mask = np.where(np.abs(rel) <= WINDOW, 0.0, NEG_INF).astype(np.float32).reshape(1, -1)
    ncol = BLOCK * 3 * BLOCK
    tc = ncol // 4
    out = pl.pallas_call(
        _bias_kernel,
        grid=(ncol // tc,),
        in_specs=[pl.BlockSpec((ATT_Q_HEADS, NUM_BUCKETS), lambda i: (0, 0)),
                  pl.BlockSpec((NUM_BUCKETS, tc), lambda i: (0, i)),
                  pl.BlockSpec((1, tc), lambda i: (0, i))],
        out_specs=pl.BlockSpec((ATT_Q_HEADS, tc), lambda i: (0, i)),
        out_shape=jax.ShapeDtypeStruct((ATT_Q_HEADS, ncol), F32),
        compiler_params=_cparams(("arbitrary",)),
        name="bias_tile",
    )(rel_bias.astype(F32).T, jnp.asarray(onehot_t), jnp.asarray(mask))
    return out.reshape(ATT_Q_HEADS // 2, 2 * BLOCK, 3 * BLOCK)


def _norm_mod(x, ng, sh, sc):
    ms = jnp.mean(x * x, axis=-1, keepdims=True)
    y = x * lax.rsqrt(ms + EPS) * ng
    return y * (1.0 + sc) + sh


def _ffn_kernel(x_ref, sh_ref, sc_ref, gt_ref, ng_ref, win_ref, wout_ref, o_ref):
    x = x_ref[0]
    h = _norm_mod(x, ng_ref[...], sh_ref[0], sc_ref[0]).astype(BF16)
    gu = jnp.dot(h, win_ref[...], preferred_element_type=F32)
    g = gu[:, :D_FF]
    u = gu[:, D_FF:]
    act = (g * _sigmoid(g) * u).astype(BF16)
    y = jnp.dot(act, wout_ref[...], preferred_element_type=F32)
    o_ref[0] = x + 0.5 * gt_ref[0] * y


def _ffn(x, sh, sc, gt, ng, w_in, w_out):
    b, s, d = x.shape
    tm = FFN_TM
    vec = pl.BlockSpec((1, 1, d), lambda i, j: (i, 0, 0))
    return pl.pallas_call(
        _ffn_kernel,
        grid=(b, s // tm),
        in_specs=[pl.BlockSpec((1, tm, d), lambda i, j: (i, j, 0)),
                  vec, vec, vec,
                  pl.BlockSpec((1, d), lambda i, j: (0, 0)),
                  pl.BlockSpec(w_in.shape, lambda i, j: (0, 0)),
                  pl.BlockSpec(w_out.shape, lambda i, j: (0, 0))],
        out_specs=pl.BlockSpec((1, tm, d), lambda i, j: (i, j, 0)),
        out_shape=jax.ShapeDtypeStruct(x.shape, F32),
        compiler_params=_cparams(("arbitrary", "arbitrary")),
        name="ffn",
    )(x, sh, sc, gt, ng, w_in, w_out)


def _mixin_kernel(x_ref, sh_ref, sc_ref, ng_ref, w_ref, zb_ref, zf_ref):
    h = _norm_mod(x_ref[0], ng_ref[...], sh_ref[0], sc_ref[0]).astype(BF16)
    z = jnp.dot(h, w_ref[...], preferred_element_type=F32)
    q = z[:, ZB_Q:ZB_I]
    zb_ref[0, :, ZB_Q:ZB_I] = (q * _sigmoid(q)).astype(BF16)
    zb_ref[0, :, ZB_I:ZB_G] = z[:, ZB_I:ZB_G].astype(BF16)
    g = z[:, ZB_G:ZB_AQ]
    zb_ref[0, :, ZB_G:ZB_AQ] = (g * _sigmoid(g)).astype(BF16)
    zb_ref[0, :, ZB_AQ:ZB_WIDTH] = z[:, ZB_AQ:ZB_WIDTH].astype(BF16)
    zf_ref[0] = z[:, ZB_WIDTH:]


def _mixin(x, sh, sc, ng, w):
    b, s, d = x.shape
    tm = FFN_TM
    vec = pl.BlockSpec((1, 1, d), lambda i, j: (i, 0, 0))
    return pl.pallas_call(
        _mixin_kernel,
        grid=(b, s // tm),
        in_specs=[pl.BlockSpec((1, tm, d), lambda i, j: (i, j, 0)),
                  vec, vec,
                  pl.BlockSpec((1, d), lambda i, j: (0, 0)),
                  pl.BlockSpec(w.shape, lambda i, j: (0, 0))],
        out_specs=[pl.BlockSpec((1, tm, ZB_WIDTH), lambda i, j: (i, j, 0)),
                   pl.BlockSpec((1, tm, ZF_WIDTH), lambda i, j: (i, j, 0))],
        out_shape=[jax.ShapeDtypeStruct((b, s, ZB_WIDTH), BF16),
                   jax.ShapeDtypeStruct((b, s, ZF_WIDTH), F32)],
        compiler_params=_cparams(("arbitrary", "arbitrary")),
        name="mix_in",
    )(x, sh, sc, ng, w)


SMALL_W = (1, 2, 4)


def _hgrn_consts(c):
    t = np.arange(c)[:, None]
    s = np.arange(c)[None, :]
    x = t ^ s
    lev = np.where(x > 0, np.floor(np.log2(np.maximum(x, 1))), -1).astype(np.int32)
    lev_f = np.where(t > s, lev, np.where(t == s, -1, -2)).astype(np.int32)
    lev_b = lev_f.T.copy()
    tri_f = (s <= t).astype(np.float32)
    tri_b = (s >= t).astype(np.float32)
    return (jnp.asarray(np.stack([lev_f, lev_b])),
            jnp.asarray(np.stack([tri_f, tri_b]), dtype=BF16))


def _ref_rows(g_cum, w, rev, c):
    idx = w if rev else w - 1
    if 2 * w >= SUBLANES:
        g3 = g_cum.reshape(c // (2 * w), 2 * w, LANES)
        m = jnp.broadcast_to(g3[:, idx:idx + 1, :], g3.shape)
        return m.reshape(c, LANES)
    g3 = g_cum.reshape(c // SUBLANES, SUBLANES, LANES)
    sub = lax.broadcasted_iota(jnp.int32, g3.shape, 1)
    m = None
    for p in range(SUBLANES // (2 * w)):
        r = jnp.broadcast_to(g3[:, p * 2 * w + idx:p * 2 * w + idx + 1, :], g3.shape)
        m = r if m is None else jnp.where(sub >= p * 2 * w, r, m)
    return m.reshape(c, LANES)


def _split3(x):
    hi = x.astype(BF16)
    r1 = x - hi.astype(F32)
    mid = r1.astype(BF16)
    lo = (r1 - mid.astype(F32)).astype(BF16)
    return hi, mid, lo


def _level_scale(g2, w, rev, c):
    return jnp.exp2(-jnp.abs(g2 - _ref_rows(g2, w, rev, c)))


def _hgrn_chunk(q, fr, v, lb, tri, lev, st_ref, a_ref, rev, c):
    nt = (((1,), (1,)), ((), ()))
    tn = (((0,), (0,)), ((), ()))
    e = jnp.exp(-jnp.abs(fr))
    r = 1.0 / (1.0 + e)
    pos = fr >= 0
    sig = jnp.where(pos, r, e * r)
    sig_n = jnp.where(pos, e * r, r)
    f = lb + (1.0 - lb) * sig
    g2 = jnp.log(f) * LOG2E
    k = (1.0 - lb) * sig_n
    qf = q.astype(F32)

    hi, mid, lo = _split3(g2)
    g_cum = (jnp.dot(tri, hi, preferred_element_type=F32)
             + jnp.dot(tri, mid, preferred_element_type=F32)
             + jnp.dot(tri, lo, preferred_element_type=F32))
    g_last = g_cum[0:1, :] if rev else g_cum[c - 1:c, :]

    st = st_ref[...]
    qi = (qf * jnp.exp2(g_cum)).astype(BF16)
    o = lax.dot_general(qi, st.astype(BF16), nt, preferred_element_type=F32)
    kd = (k * jnp.exp2(g_last - g_cum)).astype(BF16)
    u_t = lax.dot_general(v, kd, tn, preferred_element_type=F32)
    st_ref[...] = jnp.exp2(g_last) * st + u_t

    w = c // 2
    while w >= SUBLANES:
        x = _level_scale(g_cum, w, rev, c)
        q_rows, k_rows = [], []
        for b in range(c // (2 * w)):
            lo_half = slice(b * 2 * w, b * 2 * w + w)
            hi_half = slice(b * 2 * w + w, (b + 1) * 2 * w)
            qs, ks = (lo_half, hi_half) if rev else (hi_half, lo_half)
            q_rows.append(qf[qs] * x[qs])
            k_rows.extend([k[lo_half], k[hi_half] * x[hi_half]] if rev
                          else [k[lo_half] * x[lo_half], k[hi_half]])
        p = lax.dot_general(jnp.concatenate(q_rows, axis=0).astype(BF16),
                            jnp.concatenate(k_rows, axis=0).astype(BF16), nt,
                            preferred_element_type=F32)
        for b in range(c // (2 * w)):
            lo_half = slice(b * 2 * w, b * 2 * w + w)
            hi_half = slice(b * 2 * w + w, (b + 1) * 2 * w)
            qs, ks = (lo_half, hi_half) if rev else (hi_half, lo_half)
            a_ref[qs, ks] = p[b * w:(b + 1) * w, ks]
        w //= 2

    a = jnp.where(lev == -1,
                  lax.dot_general(q, k.astype(BF16), nt, preferred_element_type=F32), 0.0)
    for li, w in enumerate(SMALL_W):
        x = _level_scale(g_cum, w, rev, c)
        p = lax.dot_general((qf * x).astype(BF16), (k * x).astype(BF16), nt,
                            preferred_element_type=F32)
        a = jnp.where(lev == li, p, a)
    for b in range(c // SUBLANES):
        blk = slice(b * SUBLANES, (b + 1) * SUBLANES)
        a_ref[blk, blk] = a[blk, blk]

    return o + jnp.dot(a_ref[...].astype(BF16), v, preferred_element_type=F32)


def _hgrn_kernel(q_ref, i_ref, sg_ref, ff_ref, fb_ref, lb_ref, ng_ref, lev_ref, tri_ref,
                 o_ref, acc_ref, stf_ref, stb_ref, af_ref, ab_ref, *, c, nc, layer):
    stf_ref[...] = jnp.zeros_like(stf_ref)
    stb_ref[...] = jnp.zeros_like(stb_ref)
    af_ref[...] = jnp.zeros_like(af_ref)
    ab_ref[...] = jnp.zeros_like(ab_ref)

    raw = lb_ref[...]
    mx = jnp.max(raw, axis=1, keepdims=True)
    ex = jnp.exp(raw - mx)
    lb = jnp.sum(ex[:, :layer + 1, :], axis=1) / jnp.sum(ex, axis=1)
    lb_f, lb_b = lb[0:1], lb[1:2]
    ng = ng_ref[...]

    def chunk_f(j):
        rows = pl.ds(pl.multiple_of(j * c, c), c)
        return rows, _hgrn_chunk(q_ref[0, rows, :], ff_ref[0, rows, :], i_ref[0, rows, :], lb_f,
                                 tri_ref[0], lev_ref[0], stf_ref, af_ref, False, c)

    def chunk_b(j):
        rows = pl.ds(pl.multiple_of(j * c, c), c)
        return rows, _hgrn_chunk(q_ref[0, rows, :], fb_ref[0, rows, :], i_ref[0, rows, :], lb_b,
                                 tri_ref[1], lev_ref[1], stb_ref, ab_ref, True, c)

    def first_half(it, carry):
        rows, o = chunk_f(it)
        acc_ref[rows, :] = o
        rows, o = chunk_b(nc - 1 - it)
        acc_ref[rows, :] = o
        return carry

    def finish(rows, o):
        o = o + acc_ref[rows, :]
        ms = jnp.mean(o * o, axis=-1, keepdims=True)
        o = o * lax.rsqrt(ms + EPS) * ng
        o_ref[0, rows, :] = (o * sg_ref[0, rows, :].astype(F32)).astype(o_ref.dtype)

    def second_half(it, carry):
        finish(*chunk_f(it))
        finish(*chunk_b(nc - 1 - it))
        return carry

    lax.fori_loop(0, nc // 2, first_half, 0)
    lax.fori_loop(nc // 2, nc, second_half, 0)


def _hgrn(zb, zf, hgrn_lb, norm_g, layer):
    b, s, _ = zb.shape
    c = HG_CHUNK
    nc = s // c
    assert nc % 2 == 0
    lev, tri = _hgrn_consts(c)
    nl = hgrn_lb.shape[1]

    def col(first_col):
        base = first_col // LANES
        return pl.BlockSpec((1, s, LANES), lambda i, h: (i, 0, base + h))

    const3 = lambda shape: pl.BlockSpec(shape, lambda i, h: (0, 0, 0))
    return pl.pallas_call(
        functools.partial(_hgrn_kernel, c=c, nc=nc, layer=layer),
        grid=(b, HG_HEADS),
        in_specs=[col(ZB_Q), col(ZB_I), col(ZB_G), col(0), col(HG_WIDTH),
                  pl.BlockSpec((2, nl, LANES), lambda i, h: (0, 0, h)),
                  pl.BlockSpec((1, LANES), lambda i, h: (0, h)),
                  const3((2, c, c)), const3((2, c, c))],
        out_specs=pl.BlockSpec((1, s, LANES), lambda i, h: (i, 0, h)),
        out_shape=jax.ShapeDtypeStruct((b, s, HG_WIDTH), BF16),
        scratch_shapes=[pltpu.VMEM((s, HG_DIM), F32),
                        pltpu.VMEM((HG_DIM, HG_DIM), F32), pltpu.VMEM((HG_DIM, HG_DIM), F32),
                        pltpu.VMEM((c, c), F32), pltpu.VMEM((c, c), F32)],
        compiler_params=_cparams(("arbitrary", "arbitrary")),
        name="hgrn",
    )(zb, zb, zb, zf, zf, hgrn_lb, norm_g, lev, tri)


def _half_norm(x, gain, lo):
    x2 = x * x
    s_lo = jnp.sum(jnp.where(lo, x2, 0.0), axis=-1, keepdims=True)
    s_hi = jnp.sum(jnp.where(lo, 0.0, x2), axis=-1, keepdims=True)
    ms = jnp.where(lo, s_lo, s_hi) * (1.0 / ATT_HEAD_DIM)
    return x * lax.rsqrt(ms + EPS) * gain


def _attn_kernel(sink_ref, q_ref, kp_ref, ko_ref, kn_ref, vp_ref, vo_ref, vn_ref,
                 qg_ref, kg_ref, bias_ref, o_ref, *, nb):
    n = pl.program_id(1)
    half = LANES // 2
    kb = jnp.concatenate([kp_ref[0], ko_ref[0], kn_ref[0]], axis=0).astype(F32)
    vb = jnp.concatenate([vp_ref[0], vo_ref[0], vn_ref[0]], axis=0).astype(F32)
    lo_k = lax.broadcasted_iota(jnp.int32, kb.shape, 1) < half
    lo_q = lax.broadcasted_iota(jnp.int32, (BLOCK, LANES), 1) < half

    kn = _half_norm(kb, kg_ref[...], lo_k)
    kr = pltpu.roll(kn, half, axis=1)
    k2 = (jnp.where(lo_k, kn, kr).astype(BF16), jnp.where(lo_k, kr, kn).astype(BF16))
    v0lo = jnp.where(lo_k, vb, 0.0)
    v1hi = jnp.where(lo_k, 0.0, vb)
    v_lo = (v0lo.astype(BF16), pltpu.roll(v1hi, half, axis=1).astype(BF16))
    v_hi = (pltpu.roll(v0lo, half, axis=1).astype(BF16), v1hi.astype(BF16))

    kpos = n * BLOCK - BLOCK + lax.broadcasted_iota(jnp.int32, (1, 3 * BLOCK), 1)
    kvalid = (kpos >= 0) & (kpos < nb * BLOCK)
    row_lo = lax.broadcasted_iota(jnp.int32, (2 * BLOCK, 1), 0) < BLOCK
    nt = (((1,), (1,)), ((), ()))
    scale = 1.0 / math.sqrt(ATT_HEAD_DIM)

    for m in range(ATT_Q_HEADS // 2):
        j = (2 * m) // ATT_GROUP
        qp = _half_norm(q_ref[0, :, m * LANES:(m + 1) * LANES].astype(F32), qg_ref[...], lo_q) * scale
        q2 = jnp.concatenate([jnp.where(lo_q, qp, 0.0), jnp.where(lo_q, 0.0, qp)],
                             axis=0).astype(BF16)
        logits = lax.dot_general(q2, k2[j], nt, preferred_element_type=F32)
        logits = jnp.where(kvalid, logits + bias_ref[m], NEG_INF)
        sink = jnp.where(row_lo, sink_ref[2 * m], sink_ref[2 * m + 1])
        mx = jnp.maximum(jnp.max(logits, axis=-1, keepdims=True), sink)
        p = jnp.exp(logits - mx)
        den = jnp.sum(p, axis=-1, keepdims=True) + jnp.exp(sink - mx)
        p = (p / den).astype(BF16)
        o = (jnp.dot(p[:BLOCK], v_lo[j], preferred_element_type=F32)
             + jnp.dot(p[BLOCK:], v_hi[j], preferred_element_type=F32))
        o_ref[0, :, m * LANES:(m + 1) * LANES] = o.astype(o_ref.dtype)


def _attn(zb, q_gain, k_gain, sink, bias):
    b, s, _ = zb.shape
    nb = s // BLOCK
    q_col = ZB_AQ // ATT_WIDTH
    k_col = ZB_AK // KV_WIDTH
    v_col = ZB_AV // KV_WIDTH

    def kv(col, shift):
        return pl.BlockSpec(
            (1, BLOCK, KV_WIDTH),
            lambda i, n: (i, jnp.clip(n + shift, 0, nb - 1), col))

    return pl.pallas_call(
        functools.partial(_attn_kernel, nb=nb),
        grid=(b, nb),
        in_specs=[pl.BlockSpec(memory_space=pltpu.SMEM),
                  pl.BlockSpec((1, BLOCK, ATT_WIDTH), lambda i, n: (i, n, q_col)),
                  kv(k_col, -1), kv(k_col, 0), kv(k_col, 1),
                  kv(v_col, -1), kv(v_col, 0), kv(v_col, 1),
                  pl.BlockSpec((1, LANES), lambda i, n: (0, 0)),
                  pl.BlockSpec((1, LANES), lambda i, n: (0, 0)),
                  pl.BlockSpec(bias.shape, lambda i, n: (0, 0, 0))],
        out_specs=pl.BlockSpec((1, BLOCK, ATT_WIDTH), lambda i, n: (i, n, 0)),
        out_shape=jax.ShapeDtypeStruct((b, s, ATT_WIDTH), BF16),
        compiler_params=_cparams(("arbitrary", "arbitrary")),
        name="attn",
    )(sink, zb, zb, zb, zb, zb, zb, zb, q_gain, k_gain, bias)


def _mixout_kernel(x_ref, hg_ref, at_ref, gt_ref, w_ref, o_ref):
    mixed = (jnp.dot(hg_ref[0], w_ref[:HG_WIDTH, :], preferred_element_type=F32)
             + jnp.dot(at_ref[0], w_ref[HG_WIDTH:, :], preferred_element_type=F32))
    o_ref[0] = x_ref[0] + gt_ref[0] * mixed


def _mixout(x, o_hg, attn, gt, w):
    b, s, d = x.shape
    tm = FFN_TM
    half = pl.BlockSpec((1, tm, HG_WIDTH), lambda i, j: (i, j, 0))
    return pl.pallas_call(
        _mixout_kernel,
        grid=(b, s // tm),
        in_specs=[pl.BlockSpec((1, tm, d), lambda i, j: (i, j, 0)),
                  half, half,
                  pl.BlockSpec((1, 1, d), lambda i, j: (i, 0, 0)),
                  pl.BlockSpec(w.shape, lambda i, j: (0, 0))],
        out_specs=pl.BlockSpec((1, tm, d), lambda i, j: (i, j, 0)),
        out_shape=jax.ShapeDtypeStruct(x.shape, F32),
        compiler_params=_cparams(("arbitrary", "arbitrary")),
        name="mix_out",
    )(x, o_hg, attn, gt, w)


def _mixin_weight(w):
    hw = HG_WIDTH
    cols = [w[:, 0:hw], w[:, 3 * hw:4 * hw], w[:, 4 * hw:5 * hw], w[:, 5 * hw:],
            w[:, hw:3 * hw]]
    return jnp.concatenate(cols, axis=1).astype(BF16)


def kernel(x, c, w_ada, b_ada, norm_g, w_ffn1_in, w_ffn1_out, w_ffn2_in, w_ffn2_out,
           w_mix_in, w_mix_out, hgrn_lb, hgrn_norm_g, qk_norm_g, attn_sink, rel_bias):
    b, s, d = x.shape
    depth = w_ada.shape[0]
    bias = _bias_tile(rel_bias)
    c_pad = jnp.pad(c.astype(F32), ((0, SUBLANES - b), (0, 0)))
    for l in range(depth):
        mods = _ada(c_pad, w_ada[l], b_ada[l][None, :])[:b]
        sh1, sc1, g1, sh2, sc2, g2, sh3, sc3, g3 = [
            mods[:, i * d:(i + 1) * d][:, None, :] for i in range(N_MOD)]
        x = _ffn(x, sh1, sc1, g1, norm_g[l, 0][None, :],
                 w_ffn1_in[l].astype(BF16), w_ffn1_out[l].astype(BF16))
        zb, zf = _mixin(x, sh2, sc2, norm_g[l, 1][None, :], _mixin_weight(w_mix_in[l]))
        o_hg = _hgrn(zb, zf, hgrn_lb, hgrn_norm_g[l][None, :], l)
        qg = jnp.tile(qk_norm_g[l, 0], 2)[None, :]
        kg = jnp.tile(qk_norm_g[l, 1], 2)[None, :]
        att = _attn(zb, qg, kg, attn_sink[l], bias)
        x = _mixout(x, o_hg, att, g2, w_mix_out[l].astype(BF16))
        x = _ffn(x, sh3, sc3, g3, norm_g[l, 2][None, :],
                 w_ffn2_in[l].astype(BF16), w_ffn2_out[l].astype(BF16))
    return x
```

```python
import functools
import math

import numpy as np
import jax
import jax.numpy as jnp
from jax import lax
from jax.experimental import pallas as pl
from jax.experimental.pallas import tpu as pltpu

F32 = jnp.float32
BF16 = jnp.bfloat16

D_MODEL = 1024
HG_HEADS = 4
HG_DIM = 128
HG_WIDTH = HG_HEADS * HG_DIM
ATT_Q_HEADS = 8
ATT_KV_HEADS = 2
ATT_HEAD_DIM = 64
ATT_GROUP = ATT_Q_HEADS // ATT_KV_HEADS
ATT_WIDTH = ATT_Q_HEADS * ATT_HEAD_DIM
KV_WIDTH = ATT_KV_HEADS * ATT_HEAD_DIM
WINDOW = 128
BLOCK = 128
NUM_BUCKETS = 32
MAX_DISTANCE = 128
D_FF = 2816
N_MOD = 9
EPS = 1e-6

LANES = 128
SUBLANES = 8
PACKED_ROWS = 16
VMEM_LIMIT = 56 * 1024 * 1024

FFN_TM = 256
HG_CHUNK = 128
NEG_INF = float("-inf")
LOG2E = 1.0 / math.log(2.0)

Z_Q, Z_I, Z_G, Z_AQ = 0, HG_WIDTH, 2 * HG_WIDTH, 3 * HG_WIDTH
Z_AK = Z_AQ + ATT_WIDTH
Z_AV = Z_AK + KV_WIDTH
Z_FF = Z_AV + KV_WIDTH
Z_FB = Z_FF + HG_WIDTH
ZB_KF = Z_FF
ZB_KB = ZB_KF + HG_WIDTH
ZB_GHF = ZB_KB + HG_WIDTH
ZB_GMF = ZB_GHF + HG_WIDTH
ZB_GHB = ZB_GMF + HG_WIDTH
ZB_GMB = ZB_GHB + HG_WIDTH
ZB_WIDTH = ZB_GMB + HG_WIDTH


def _cparams(sem):
    return pltpu.CompilerParams(dimension_semantics=sem, vmem_limit_bytes=VMEM_LIMIT)


def _sigmoid(x):
    return 1.0 / (1.0 + jnp.exp(-x))


def _ada_kernel(c_ref, w_ref, b_ref, o_ref):
    c = c_ref[...]
    ca = c * _sigmoid(c)
    o_ref[...] = jnp.dot(ca, w_ref[...], precision=lax.Precision.HIGHEST,
                         preferred_element_type=F32) + b_ref[...]


def _ada(c_pad, w, b):
    rows, d = c_pad.shape
    n = w.shape[1]
    tn = 1024
    return pl.pallas_call(
        _ada_kernel,
        grid=(n // tn,),
        in_specs=[pl.BlockSpec((rows, d), lambda j: (0, 0)),
                  pl.BlockSpec((d, tn), lambda j: (0, j)),
                  pl.BlockSpec((1, tn), lambda j: (0, j))],
        out_specs=pl.BlockSpec((rows, tn), lambda j: (0, j)),
        out_shape=jax.ShapeDtypeStruct((rows, n), F32),
        compiler_params=_cparams(("arbitrary",)),
        name="ada",
    )(c_pad, w, b)


def _t5_bucket_np(rel):
    nb = NUM_BUCKETS // 2
    max_exact = nb // 2
    ret = (rel > 0).astype(np.int32) * nb
    n = np.abs(rel)
    ratio = np.maximum(n, 1).astype(np.float32) / np.float32(max_exact)
    large = max_exact + (np.log(ratio) / np.float32(math.log(MAX_DISTANCE / max_exact))
                         * np.float32(nb - max_exact)).astype(np.int32)
    large = np.minimum(large, nb - 1)
    return ret + np.where(n < max_exact, n, large)


def _bias_kernel(rbt_ref, oh_ref, msk_ref, o_ref):
    b = jnp.dot(rbt_ref[...], oh_ref[...], precision=lax.Precision.HIGHEST,
                preferred_element_type=F32)
    o_ref[0] = b * LOG2E + msk_ref[0]


def _bias_tile(rel_bias):
    kcol = np.arange(3 * BLOCK)[None, :]
    rel = (kcol - BLOCK) - np.arange(BLOCK)[:, None]
    bucket = _t5_bucket_np(rel).reshape(-1)
    onehot_t = (np.arange(NUM_BUCKETS)[:, None] == bucket[None, :]).astype(np.float32)
    window = np.abs(rel) <= WINDOW
    valid = np.stack([window & (kcol >= BLOCK), window, window & (kcol < 2 * BLOCK)])
    mask = np.where(valid, 0.0, NEG_INF).astype(np.float32).reshape(3, 1, -1)
    ncol = BLOCK * 3 * BLOCK
    tc = ncol // 4
    out = pl.pallas_call(
        _bias_kernel,
        grid=(3, ncol // tc),
        in_specs=[pl.BlockSpec((ATT_Q_HEADS, NUM_BUCKETS), lambda e, i: (0, 0)),
                  pl.BlockSpec((NUM_BUCKETS, tc), lambda e, i: (0, i)),
                  pl.BlockSpec((1, 1, tc), lambda e, i: (e, 0, i))],
        out_specs=pl.BlockSpec((1, ATT_Q_HEADS, tc), lambda e, i: (e, 0, i)),
        out_shape=jax.ShapeDtypeStruct((3, ATT_Q_HEADS, ncol), F32),
        compiler_params=_cparams(("arbitrary", "arbitrary")),
        name="bias_tile",
    )(rel_bias.astype(F32).T, jnp.asarray(onehot_t), jnp.asarray(mask))
    return out.reshape(3, ATT_Q_HEADS // 2, 2 * BLOCK, 3 * BLOCK)


def _norm_mod(x, ng, sh, sc):
    ms = jnp.mean(x * x, axis=-1, keepdims=True)
    y = x * lax.rsqrt(ms + EPS) * ng
    return y * (1.0 + sc) + sh


def _ffn_kernel(x_ref, sh_ref, sc_ref, gt_ref, ng_ref, win_ref, wout_ref, o_ref):
    x = x_ref[0]
    h = _norm_mod(x, ng_ref[...], sh_ref[0], sc_ref[0]).astype(BF16)
    gu = jnp.dot(h, win_ref[...], preferred_element_type=F32)
    g = gu[:, :D_FF]
    u = gu[:, D_FF:]
    act = (g * _sigmoid(g) * u).astype(BF16)
    y = jnp.dot(act, wout_ref[...], preferred_element_type=F32)
    o_ref[0] = x + 0.5 * gt_ref[0] * y


def _ffn(x, sh, sc, gt, ng, w_in, w_out):
    b, s, d = x.shape
    tm = FFN_TM
    vec = pl.BlockSpec((1, 1, d), lambda i, j: (i, 0, 0))
    return pl.pallas_call(
        _ffn_kernel,
        grid=(b, s // tm),
        in_specs=[pl.BlockSpec((1, tm, d), lambda i, j: (i, j, 0)),
                  vec, vec, vec,
                  pl.BlockSpec((1, d), lambda i, j: (0, 0)),
                  pl.BlockSpec(w_in.shape, lambda i, j: (0, 0)),
                  pl.BlockSpec(w_out.shape, lambda i, j: (0, 0))],
        out_specs=pl.BlockSpec((1, tm, d), lambda i, j: (i, j, 0)),
        out_shape=jax.ShapeDtypeStruct(x.shape, F32),
        compiler_params=_cparams(("arbitrary", "arbitrary")),
        name="ffn",
    )(x, sh, sc, gt, ng, w_in, w_out)


def _half_norm(x, gain, lo):
    x2 = x * x
    s_lo = jnp.sum(jnp.where(lo, x2, 0.0), axis=-1, keepdims=True)
    s_hi = jnp.sum(jnp.where(lo, 0.0, x2), axis=-1, keepdims=True)
    ms = jnp.where(lo, s_lo, s_hi) * (1.0 / ATT_HEAD_DIM)
    return x * lax.rsqrt(ms + EPS) * gain


def _mixin_kernel(x_ref, sh_ref, sc_ref, ng_ref, w_ref, lb_ref, qg_ref, kg_ref, zb_ref, *, layer):
    h = _norm_mod(x_ref[0], ng_ref[...], sh_ref[0], sc_ref[0]).astype(BF16)
    z = jnp.dot(h, w_ref[...], preferred_element_type=F32)

    q = z[:, Z_Q:Z_I]
    zb_ref[0, :, Z_Q:Z_I] = (q * _sigmoid(q)).astype(BF16)
    zb_ref[0, :, Z_I:Z_G] = z[:, Z_I:Z_G].astype(BF16)
    g = z[:, Z_G:Z_AQ]
    zb_ref[0, :, Z_G:Z_AQ] = (g * _sigmoid(g)).astype(BF16)

    lo = lax.broadcasted_iota(jnp.int32, (z.shape[0], LANES), 1) < LANES // 2
    for m in range(ATT_WIDTH // LANES):
        cols = slice(Z_AQ + m * LANES, Z_AQ + (m + 1) * LANES)
        zb_ref[0, :, cols] = _half_norm(z[:, cols], qg_ref[...], lo).astype(BF16)
    zb_ref[0, :, Z_AK:Z_AV] = _half_norm(z[:, Z_AK:Z_AV], kg_ref[...], lo).astype(BF16)
    zb_ref[0, :, Z_AV:Z_FF] = z[:, Z_AV:Z_FF].astype(BF16)

    raw = lb_ref[...]
    mx = jnp.max(raw, axis=1, keepdims=True)
    ex = jnp.exp(raw - mx)
    lb = jnp.sum(ex[:, :layer + 1, :], axis=1) / jnp.sum(ex, axis=1)

    for d, (z0, k0, h0, m0) in enumerate(((Z_FF, ZB_KF, ZB_GHF, ZB_GMF),
                                          (Z_FB, ZB_KB, ZB_GHB, ZB_GMB))):
        fr = z[:, z0:z0 + HG_WIDTH]
        lbd = lb[d:d + 1]
        e = jnp.exp(-jnp.abs(fr))
        r = 1.0 / (1.0 + e)
        pos = fr >= 0
        sig = jnp.where(pos, r, e * r)
        sig_n = jnp.where(pos, e * r, r)
        g2 = jnp.log(lbd + (1.0 - lbd) * sig) * LOG2E
        hi = g2.astype(BF16)
        zb_ref[0, :, k0:k0 + HG_WIDTH] = ((1.0 - lbd) * sig_n).astype(BF16)
        zb_ref[0, :, h0:h0 + HG_WIDTH] = hi
        zb_ref[0, :, m0:m0 + HG_WIDTH] = (g2 - hi.astype(F32)).astype(BF16)


def _mixin(x, sh, sc, ng, w, hgrn_lb, qg, kg, layer):
    b, s, d = x.shape
    tm = FFN_TM
    vec = pl.BlockSpec((1, 1, d), lambda i, j: (i, 0, 0))
    row = lambda n: pl.BlockSpec((1, n), lambda i, j: (0, 0))
    return pl.pallas_call(
        functools.partial(_mixin_kernel, layer=layer),
        grid=(b, s // tm),
        in_specs=[pl.BlockSpec((1, tm, d), lambda i, j: (i, j, 0)),
                  vec, vec, row(d),
                  pl.BlockSpec(w.shape, lambda i, j: (0, 0)),
                  pl.BlockSpec(hgrn_lb.shape, lambda i, j: (0, 0, 0)),
                  row(LANES), row(LANES)],
        out_specs=pl.BlockSpec((1, tm, ZB_WIDTH), lambda i, j: (i, j, 0)),
        out_shape=jax.ShapeDtypeStruct((b, s, ZB_WIDTH), BF16),
        compiler_params=_cparams(("arbitrary", "arbitrary")),
        name="mix_in",
    )(x, sh, sc, ng, w, hgrn_lb, qg, kg)


SMALL_W = (1, 2, 4, 8)
GROUP = PACKED_ROWS


def _hgrn_consts(c):
    t = np.arange(c)[:, None]
    s = np.arange(c)[None, :]
    x = t ^ s
    lev = np.where(x > 0, np.floor(np.log2(np.maximum(x, 1))), -1).astype(np.int32)
    lev_f = np.where(t > s, lev, np.where(t == s, -1, -2)).astype(np.int32)
    lev_b = lev_f.T.copy()

    def exponent_rows(w, rev):
        m = np.zeros((GROUP, GROUP), np.float32)
        for r in range(GROUP):
            b0 = (r // (2 * w)) * 2 * w
            if not rev:
                ref = b0 + w - 1
                lo_u, hi_u = (ref + 1, r) if r > ref else (r + 1, ref)
            else:
                ref = b0 + w
                lo_u, hi_u = (r, ref - 1) if r < ref else (ref, r - 1)
            m[r, lo_u:hi_u + 1] = 1.0
        return m

    cums, smalls = [], []
    for rev in (False, True):
        cum = ((s >= t) if rev else (s <= t)).astype(np.float32)
        cums.append(np.concatenate([cum, cum], axis=1))
        rows = np.concatenate([exponent_rows(w, rev) for w in SMALL_W])
        smalls.append(np.concatenate([rows, rows], axis=1))
    return (jnp.asarray(np.stack([lev_f, lev_b])),
            jnp.asarray(np.stack(cums), dtype=BF16),
            jnp.asarray(np.stack(smalls), dtype=BF16))


def _neg_abs(x):
    bits = lax.bitcast_convert_type(x, jnp.uint32) | jnp.uint32(0x80000000)
    return lax.bitcast_convert_type(bits, F32)


def _ref_rows(g_cum, w, rev, c):
    idx = w if rev else w - 1
    g3 = g_cum.reshape(c // (2 * w), 2 * w, LANES)
    return jnp.broadcast_to(g3[:, idx:idx + 1, :], g3.shape).reshape(c, LANES)


def _hgrn_stage_a(tcum, tsmall, gh, gm, gc_ref, ge_ref, c):
    gc_ref[...] = jnp.dot(tcum, jnp.concatenate([gh, gm], axis=0), preferred_element_type=F32)
    groups = [slice(g * GROUP, (g + 1) * GROUP) for g in range(c // GROUP)]
    wide = jnp.concatenate([jnp.concatenate([gh[g] for g in groups], axis=1),
                            jnp.concatenate([gm[g] for g in groups], axis=1)], axis=0)
    ge_ref[...] = jnp.dot(tsmall, wide, preferred_element_type=F32)


def _hgrn_stage_b(q, k, v, gc_ref, ge_ref, lev, st_ref, a_ref, ab_ref, oi_ref, rev, c):
    nt = (((1,), (1,)), ((), ()))
    tn = (((0,), (0,)), ((), ()))
    g_cum = gc_ref[...]
    g_last = g_cum[0:1, :] if rev else g_cum[c - 1:c, :]

    st = st_ref[...]
    qi = q * jnp.exp2(g_cum).astype(BF16)
    oi_ref[...] = lax.dot_general(qi, st.astype(BF16), nt, preferred_element_type=F32)
    kd = k * jnp.exp2(g_last - g_cum).astype(BF16)
    u_t = lax.dot_general(v, kd, tn, preferred_element_type=F32)
    st_ref[...] = jnp.exp2(g_last) * st + u_t

    w = c // 2
    while w >= GROUP:
        x = jnp.exp2(_neg_abs(g_cum - _ref_rows(g_cum, w, rev, c))).astype(BF16)
        q_rows, k_rows = [], []
        for b in range(c // (2 * w)):
            lo_half = slice(b * 2 * w, b * 2 * w + w)
            hi_half = slice(b * 2 * w + w, (b + 1) * 2 * w)
            qs = lo_half if rev else hi_half
            q_rows.append(q[qs] * x[qs])
            k_rows.extend([k[lo_half], k[hi_half] * x[hi_half]] if rev
                          else [k[lo_half] * x[lo_half], k[hi_half]])
        p = lax.dot_general(jnp.concatenate(q_rows, axis=0), jnp.concatenate(k_rows, axis=0), nt,
                            preferred_element_type=F32)
        for b in range(c // (2 * w)):
            lo_half = slice(b * 2 * w, b * 2 * w + w)
            hi_half = slice(b * 2 * w + w, (b + 1) * 2 * w)
            qs, ks = (lo_half, hi_half) if rev else (hi_half, lo_half)
            a_ref[qs, ks] = p[b * w:(b + 1) * w, ks]
        w //= 2

    a = jnp.where(lev == -1, lax.dot_general(q, k, nt, preferred_element_type=F32), 0.0)
    for li in range(len(SMALL_W)):
        e = jnp.concatenate([ge_ref[li * GROUP:(li + 1) * GROUP, g * LANES:(g + 1) * LANES]
                             for g in range(c // GROUP)], axis=0)
        x = jnp.exp2(e).astype(BF16)
        p = lax.dot_general(q * x, k * x, nt, preferred_element_type=F32)
        a = jnp.where(lev == li, p, a)
    for b in range(c // GROUP):
        blk = slice(b * GROUP, (b + 1) * GROUP)
        a_ref[blk, blk] = a[blk, blk]
    ab_ref[...] = a_ref[...].astype(BF16)


def _hgrn_kernel(q_ref, i_ref, sg_ref, kf_ref, kb_ref, ghf_ref, gmf_ref, ghb_ref, gmb_ref,
                 ng_ref, lev_ref, tc_ref, ts_ref, o_ref,
                 acc_ref, st_ref, a_ref, ab_ref, oi_ref, gc_ref, ge_ref, *, c, nc):
    for ref in (acc_ref, st_ref, a_ref, ab_ref, oi_ref):
        ref[...] = jnp.zeros_like(ref)
    k_refs = (kf_ref, kb_ref)
    g_refs = ((ghf_ref, gmf_ref), (ghb_ref, gmb_ref))

    def rows_of(p, d):
        j = p if d == 0 else nc - 1 - p
        return pl.ds(pl.multiple_of(j * c, c), c)

    def stage_a(p, slot):
        for d in range(2):
            rows = rows_of(p, d)
            _hgrn_stage_a(tc_ref[d], ts_ref[d], g_refs[d][0][0, rows, :], g_refs[d][1][0, rows, :],
                          gc_ref.at[slot, d], ge_ref.at[slot, d], c)

    def stage_t(p):
        for d in range(2):
            rows = rows_of(p, d)
            acc_ref[rows, :] += oi_ref[d] + jnp.dot(ab_ref[d], i_ref[0, rows, :],
                                                    preferred_element_type=F32)

    def stage_b(p, slot):
        for d in range(2):
            rows = rows_of(p, d)
            _hgrn_stage_b(q_ref[0, rows, :], k_refs[d][0, rows, :], i_ref[0, rows, :],
                          gc_ref.at[slot, d], ge_ref.at[slot, d], lev_ref[d], st_ref.at[d],
                          a_ref.at[d], ab_ref.at[d], oi_ref.at[d], d == 1, c)

    def pair(p, slot):
        stage_a(jnp.minimum(p + 1, nc - 1), 1 - slot)
        stage_t(jnp.maximum(p - 1, 0))
        stage_b(p, slot)

    def two_pairs(i2, carry):
        pair(2 * i2, 0)
        pair(2 * i2 + 1, 1)
        return carry

    stage_a(0, 0)
    lax.fori_loop(0, nc // 2, two_pairs, 0)
    stage_t(nc - 1)

    ng = ng_ref[...]

    def finish(j, carry):
        rows = pl.ds(pl.multiple_of(j * c, c), c)
        o = acc_ref[rows, :]
        ms = jnp.mean(o * o, axis=-1, keepdims=True)
        o = o * lax.rsqrt(ms + EPS) * ng
        o_ref[0, rows, :] = (o * sg_ref[0, rows, :].astype(F32)).astype(o_ref.dtype)
        return carry

    lax.fori_loop(0, nc, finish, 0)


def _hgrn(zb, norm_g):
    b, s, _ = zb.shape
    c = HG_CHUNK
    nc = s // c
    assert nc % 2 == 0 and c % (2 * GROUP) == 0
    lev, tcum, tsmall = _hgrn_consts(c)
    n_small = len(SMALL_W) * GROUP

    def col(first_col):
        base = first_col // LANES
        return pl.BlockSpec((1, s, LANES), lambda i, h: (i, 0, base + h))

    const3 = lambda a: pl.BlockSpec(a.shape, lambda i, h: (0, 0, 0))
    return pl.pallas_call(
        functools.partial(_hgrn_kernel, c=c, nc=nc),
        grid=(b, HG_HEADS),
        in_specs=[col(Z_Q), col(Z_I), col(Z_G), col(ZB_KF), col(ZB_KB),
                  col(ZB_GHF), col(ZB_GMF), col(ZB_GHB), col(ZB_GMB),
                  pl.BlockSpec((1, LANES), lambda i, h: (0, h)),
                  const3(lev), const3(tcum), const3(tsmall)],
        out_specs=pl.BlockSpec((1, s, LANES), lambda i, h: (i, 0, h)),
        out_shape=jax.ShapeDtypeStruct((b, s, HG_WIDTH), BF16),
        scratch_shapes=[pltpu.VMEM((s, HG_DIM), F32),
                        pltpu.VMEM((2, HG_DIM, HG_DIM), F32),
                        pltpu.VMEM((2, c, c), F32),
                        pltpu.VMEM((2, c, c), BF16),
                        pltpu.VMEM((2, c, HG_DIM), F32),
                        pltpu.VMEM((2, 2, c, LANES), F32),
                        pltpu.VMEM((2, 2, n_small, (c // GROUP) * LANES), F32)],
        compiler_params=_cparams(("arbitrary", "arbitrary")),
        name="hgrn",
    )(zb, zb, zb, zb, zb, zb, zb, zb, zb, norm_g, lev, tcum, tsmall)


def _attn_kernel(sink_ref, q_ref, kp_ref, ko_ref, kn_ref, vp_ref, vo_ref, vn_ref,
                 bias_ref, o_ref, *, nb):
    n = pl.program_id(1)
    half = LANES // 2
    kb = jnp.concatenate([kp_ref[0], ko_ref[0], kn_ref[0]], axis=0)
    vb = jnp.concatenate([vp_ref[0], vo_ref[0], vn_ref[0]], axis=0)
    lo_k = lax.broadcasted_iota(jnp.int32, kb.shape, 1) < half
    lo_q = lax.broadcasted_iota(jnp.int32, (BLOCK, LANES), 1) < half
    zero = jnp.zeros_like(kb)

    kr = pltpu.roll(kb, half, axis=1)
    k2 = (jnp.where(lo_k, kb, kr), jnp.where(lo_k, kr, kb))
    v0lo = jnp.where(lo_k, vb, zero)
    v1hi = jnp.where(lo_k, zero, vb)
    v_lo = (v0lo, pltpu.roll(v1hi, half, axis=1))
    v_hi = (pltpu.roll(v0lo, half, axis=1), v1hi)

    edge = jnp.where(n == 0, 0, jnp.where(n == nb - 1, 2, 1))
    row_lo = lax.broadcasted_iota(jnp.int32, (2 * BLOCK, 1), 0) < BLOCK
    nt = (((1,), (1,)), ((), ()))

    for m in range(ATT_Q_HEADS // 2):
        j = (2 * m) // ATT_GROUP
        qp = q_ref[0, :, m * LANES:(m + 1) * LANES]
        zq = jnp.zeros_like(qp)
        q2 = jnp.concatenate([jnp.where(lo_q, qp, zq), jnp.where(lo_q, zq, qp)], axis=0)
        logits = lax.dot_general(q2, k2[j], nt, preferred_element_type=F32) + bias_ref[edge, m]
        sink = jnp.where(row_lo, sink_ref[2 * m], sink_ref[2 * m + 1])
        mx = jnp.maximum(jnp.max(logits, axis=-1, keepdims=True), sink)
        p = jnp.exp2(logits - mx)
        den = jnp.sum(p, axis=-1, keepdims=True) + jnp.exp2(sink - mx)
        pb = p.astype(BF16)
        rden = 1.0 / den
        o = (jnp.dot(pb[:BLOCK], v_lo[j], preferred_element_type=F32)
             + jnp.dot(pb[BLOCK:], v_hi[j], preferred_element_type=F32))
        o = o * jnp.where(lo_q, rden[:BLOCK], rden[BLOCK:])
        o_ref[0, :, m * LANES:(m + 1) * LANES] = o.astype(o_ref.dtype)


def _attn(zb, sink2, bias):
    b, s, _ = zb.shape
    nb = s // BLOCK
    assert nb >= 2
    q_col = Z_AQ // ATT_WIDTH
    k_col = Z_AK // KV_WIDTH
    v_col = Z_AV // KV_WIDTH

    def kv(col, shift):
        return pl.BlockSpec(
            (1, BLOCK, KV_WIDTH),
            lambda i, n: (i, jnp.clip(n + shift, 0, nb - 1), col))

    return pl.pallas_call(
        functools.partial(_attn_kernel, nb=nb),
        grid=(b, nb),
        in_specs=[pl.BlockSpec(memory_space=pltpu.SMEM),
                  pl.BlockSpec((1, BLOCK, ATT_WIDTH), lambda i, n: (i, n, q_col)),
                  kv(k_col, -1), kv(k_col, 0), kv(k_col, 1),
                  kv(v_col, -1), kv(v_col, 0), kv(v_col, 1),
                  pl.BlockSpec(bias.shape, lambda i, n: (0, 0, 0, 0))],
        out_specs=pl.BlockSpec((1, BLOCK, ATT_WIDTH), lambda i, n: (i, n, 0)),
        out_shape=jax.ShapeDtypeStruct((b, s, ATT_WIDTH), BF16),
        compiler_params=_cparams(("arbitrary", "arbitrary")),
        name="attn",
    )(sink2, zb, zb, zb, zb, zb, zb, zb, bias)


def _mixout_kernel(x_ref, hg_ref, at_ref, gt_ref, w_ref, o_ref):
    mixed = (jnp.dot(hg_ref[0], w_ref[:HG_WIDTH, :], preferred_element_type=F32)
             + jnp.dot(at_ref[0], w_ref[HG_WIDTH:, :], preferred_element_type=F32))
    o_ref[0] = x_ref[0] + gt_ref[0] * mixed


def _mixout(x, o_hg, attn, gt, w):
    b, s, d = x.shape
    tm = FFN_TM
    half = pl.BlockSpec((1, tm, HG_WIDTH), lambda i, j: (i, j, 0))
    return pl.pallas_call(
        _mixout_kernel,
        grid=(b, s // tm),
        in_specs=[pl.BlockSpec((1, tm, d), lambda i, j: (i, j, 0)),
                  half, half,
                  pl.BlockSpec((1, 1, d), lambda i, j: (i, 0, 0)),
                  pl.BlockSpec(w.shape, lambda i, j: (0, 0))],
        out_specs=pl.BlockSpec((1, tm, d), lambda i, j: (i, j, 0)),
        out_shape=jax.ShapeDtypeStruct(x.shape, F32),
        compiler_params=_cparams(("arbitrary", "arbitrary")),
        name="mix_out",
    )(x, o_hg, attn, gt, w)


def _mixin_weight(w):
    hw = HG_WIDTH
    cols = [w[:, 0:hw], w[:, 3 * hw:4 * hw], w[:, 4 * hw:5 * hw], w[:, 5 * hw:],
            w[:, hw:3 * hw]]
    return jnp.concatenate(cols, axis=1).astype(BF16)


def kernel(x, c, w_ada, b_ada, norm_g, w_ffn1_in, w_ffn1_out, w_ffn2_in, w_ffn2_out,
           w_mix_in, w_mix_out, hgrn_lb, hgrn_norm_g, qk_norm_g, attn_sink, rel_bias):
    b, s, d = x.shape
    depth = w_ada.shape[0]
    bias = _bias_tile(rel_bias)
    c_pad = jnp.pad(c.astype(F32), ((0, SUBLANES - b), (0, 0)))
    for l in range(depth):
        mods = _ada(c_pad, w_ada[l], b_ada[l][None, :])[:b]
        sh1, sc1, g1, sh2, sc2, g2, sh3, sc3, g3 = [
            mods[:, i * d:(i + 1) * d][:, None, :] for i in range(N_MOD)]
        x = _ffn(x, sh1, sc1, g1, norm_g[l, 0][None, :],
                 w_ffn1_in[l].astype(BF16), w_ffn1_out[l].astype(BF16))
        qg = (jnp.tile(qk_norm_g[l, 0], 2) * (LOG2E / math.sqrt(ATT_HEAD_DIM)))[None, :]
        kg = jnp.tile(qk_norm_g[l, 1], 2)[None, :]
        zb = _mixin(x, sh2, sc2, norm_g[l, 1][None, :], _mixin_weight(w_mix_in[l]),
                    hgrn_lb, qg, kg, l)
        o_hg = _hgrn(zb, hgrn_norm_g[l][None, :])
        att = _attn(zb, attn_sink[l] * LOG2E, bias)
        x = _mixout(x, o_hg, att, g2, w_mix_out[l].astype(BF16))
        x = _ffn(x, sh3, sc3, g3, norm_g[l, 2][None, :],
                 w_ffn2_in[l].astype(BF16), w_ffn2_out[l].astype(BF16))
    return x
```

```python
import functools
import math

import numpy as np
import jax
import jax.numpy as jnp
from jax import lax
from jax.experimental import pallas as pl
from jax.experimental.pallas import tpu as pltpu

F32 = jnp.float32
BF16 = jnp.bfloat16

D_MODEL = 1024
HG_HEADS = 4
HG_DIM = 128
HG_WIDTH = HG_HEADS * HG_DIM
ATT_Q_HEADS = 8
ATT_KV_HEADS = 2
ATT_HEAD_DIM = 64
ATT_GROUP = ATT_Q_HEADS // ATT_KV_HEADS
ATT_WIDTH = ATT_Q_HEADS * ATT_HEAD_DIM
KV_WIDTH = ATT_KV_HEADS * ATT_HEAD_DIM
WINDOW = 128
BLOCK = 128
NUM_BUCKETS = 32
MAX_DISTANCE = 128
D_FF = 2816
N_MOD = 9
EPS = 1e-6

LANES = 128
SUBLANES = 8
PACKED_ROWS = 16
VMEM_LIMIT = 56 * 1024 * 1024

FFN_TM = 256
HG_CHUNK = 128
NEG_INF = float("-inf")
LOG2E = 1.0 / math.log(2.0)

Z_Q, Z_I, Z_G, Z_AQ = 0, HG_WIDTH, 2 * HG_WIDTH, 3 * HG_WIDTH
Z_AK = Z_AQ + ATT_WIDTH
Z_AV = Z_AK + KV_WIDTH
Z_FF = Z_AV + KV_WIDTH
Z_FB = Z_FF + HG_WIDTH
ZB_KF = Z_FF
ZB_KB = ZB_KF + HG_WIDTH
ZB_GHF = ZB_KB + HG_WIDTH
ZB_GMF = ZB_GHF + HG_WIDTH
ZB_GHB = ZB_GMF + HG_WIDTH
ZB_GMB = ZB_GHB + HG_WIDTH
ZB_WIDTH = ZB_GMB + HG_WIDTH


def _cparams(sem, flags=None):
    return pltpu.CompilerParams(dimension_semantics=sem, vmem_limit_bytes=VMEM_LIMIT, flags=flags)


def _sigmoid(x):
    return 1.0 / (1.0 + jnp.exp(-x))


def _ada_kernel(c_ref, w_ref, b_ref, o_ref):
    c = c_ref[...]
    ca = c * _sigmoid(c)
    o_ref[...] = jnp.dot(ca, w_ref[...], precision=lax.Precision.HIGHEST,
                         preferred_element_type=F32) + b_ref[...]


def _ada(c_pad, w, b):
    rows, d = c_pad.shape
    n = w.shape[1]
    tn = 1024
    return pl.pallas_call(
        _ada_kernel,
        grid=(n // tn,),
        in_specs=[pl.BlockSpec((rows, d), lambda j: (0, 0)),
                  pl.BlockSpec((d, tn), lambda j: (0, j)),
                  pl.BlockSpec((1, tn), lambda j: (0, j))],
        out_specs=pl.BlockSpec((rows, tn), lambda j: (0, j)),
        out_shape=jax.ShapeDtypeStruct((rows, n), F32),
        compiler_params=_cparams(("arbitrary",)),
        name="ada",
    )(c_pad, w, b)


def _t5_bucket_np(rel):
    nb = NUM_BUCKETS // 2
    max_exact = nb // 2
    ret = (rel > 0).astype(np.int32) * nb
    n = np.abs(rel)
    ratio = np.maximum(n, 1).astype(np.float32) / np.float32(max_exact)
    large = max_exact + (np.log(ratio) / np.float32(math.log(MAX_DISTANCE / max_exact))
                         * np.float32(nb - max_exact)).astype(np.int32)
    large = np.minimum(large, nb - 1)
    return ret + np.where(n < max_exact, n, large)


def _bias_kernel(rbt_ref, oh_ref, msk_ref, o_ref):
    b = jnp.dot(rbt_ref[...], oh_ref[...], precision=lax.Precision.HIGHEST,
                preferred_element_type=F32)
    o_ref[0] = b * LOG2E + msk_ref[0]


def _bias_tile(rel_bias):
    kcol = np.arange(3 * BLOCK)[None, :]
    rel = (kcol - BLOCK) - np.arange(BLOCK)[:, None]
    bucket = _t5_bucket_np(rel).reshape(-1)
    onehot_t = (np.arange(NUM_BUCKETS)[:, None] == bucket[None, :]).astype(np.float32)
    window = np.abs(rel) <= WINDOW
    valid = np.stack([window & (kcol >= BLOCK), window, window & (kcol < 2 * BLOCK)])
    mask = np.where(valid, 0.0, NEG_INF).astype(np.float32).reshape(3, 1, -1)
    ncol = BLOCK * 3 * BLOCK
    tc = ncol // 4
    out = pl.pallas_call(
        _bias_kernel,
        grid=(3, ncol // tc),
        in_specs=[pl.BlockSpec((ATT_Q_HEADS, NUM_BUCKETS), lambda e, i: (0, 0)),
                  pl.BlockSpec((NUM_BUCKETS, tc), lambda e, i: (0, i)),
                  pl.BlockSpec((1, 1, tc), lambda e, i: (e, 0, i))],
        out_specs=pl.BlockSpec((1, ATT_Q_HEADS, tc), lambda e, i: (e, 0, i)),
        out_shape=jax.ShapeDtypeStruct((3, ATT_Q_HEADS, ncol), F32),
        compiler_params=_cparams(("arbitrary", "arbitrary")),
        name="bias_tile",
    )(rel_bias.astype(F32).T, jnp.asarray(onehot_t), jnp.asarray(mask))
    return out.reshape(3, ATT_Q_HEADS // 2, 2 * BLOCK, 3 * BLOCK)


def _norm_mod(x, ng, sh, sc):
    ms = jnp.mean(x * x, axis=-1, keepdims=True)
    y = x * lax.rsqrt(ms + EPS) * ng
    return y * (1.0 + sc) + sh


def _ffn_kernel(x_ref, sh_ref, sc_ref, gt_ref, ng_ref, win_ref, wout_ref, o_ref):
    x = x_ref[0]
    h = _norm_mod(x, ng_ref[...], sh_ref[0], sc_ref[0]).astype(BF16)
    gu = jnp.dot(h, win_ref[...], preferred_element_type=F32)
    g = gu[:, :D_FF]
    u = gu[:, D_FF:]
    act = (g * _sigmoid(g) * u).astype(BF16)
    y = jnp.dot(act, wout_ref[...], preferred_element_type=F32)
    o_ref[0] = x + 0.5 * gt_ref[0] * y


def _ffn(x, sh, sc, gt, ng, w_in, w_out):
    b, s, d = x.shape
    tm = FFN_TM
    vec = pl.BlockSpec((1, 1, d), lambda i, j: (i, 0, 0))
    return pl.pallas_call(
        _ffn_kernel,
        grid=(b, s // tm),
        in_specs=[pl.BlockSpec((1, tm, d), lambda i, j: (i, j, 0)),
                  vec, vec, vec,
                  pl.BlockSpec((1, d), lambda i, j: (0, 0)),
                  pl.BlockSpec(w_in.shape, lambda i, j: (0, 0)),
                  pl.BlockSpec(w_out.shape, lambda i, j: (0, 0))],
        out_specs=pl.BlockSpec((1, tm, d), lambda i, j: (i, j, 0)),
        out_shape=jax.ShapeDtypeStruct(x.shape, F32),
        compiler_params=_cparams(("arbitrary", "arbitrary")),
        name="ffn",
    )(x, sh, sc, gt, ng, w_in, w_out)


def _half_norm(x, gain, lo):
    x2 = x * x
    s_lo = jnp.sum(jnp.where(lo, x2, 0.0), axis=-1, keepdims=True)
    s_hi = jnp.sum(jnp.where(lo, 0.0, x2), axis=-1, keepdims=True)
    ms = jnp.where(lo, s_lo, s_hi) * (1.0 / ATT_HEAD_DIM)
    return x * lax.rsqrt(ms + EPS) * gain


def _mixin_kernel(x_ref, sh_ref, sc_ref, ng_ref, w_ref, lb_ref, qg_ref, kg_ref, zb_ref, *, layer):
    h = _norm_mod(x_ref[0], ng_ref[...], sh_ref[0], sc_ref[0]).astype(BF16)
    z = jnp.dot(h, w_ref[...], preferred_element_type=F32)

    q = z[:, Z_Q:Z_I]
    zb_ref[0, :, Z_Q:Z_I] = (q * _sigmoid(q)).astype(BF16)
    zb_ref[0, :, Z_I:Z_G] = z[:, Z_I:Z_G].astype(BF16)
    g = z[:, Z_G:Z_AQ]
    zb_ref[0, :, Z_G:Z_AQ] = (g * _sigmoid(g)).astype(BF16)

    lo = lax.broadcasted_iota(jnp.int32, (z.shape[0], LANES), 1) < LANES // 2
    for m in range(ATT_WIDTH // LANES):
        cols = slice(Z_AQ + m * LANES, Z_AQ + (m + 1) * LANES)
        zb_ref[0, :, cols] = _half_norm(z[:, cols], qg_ref[...], lo).astype(BF16)
    zb_ref[0, :, Z_AK:Z_AV] = _half_norm(z[:, Z_AK:Z_AV], kg_ref[...], lo).astype(BF16)
    zb_ref[0, :, Z_AV:Z_FF] = z[:, Z_AV:Z_FF].astype(BF16)

    raw = lb_ref[...]
    mx = jnp.max(raw, axis=1, keepdims=True)
    ex = jnp.exp(raw - mx)
    lb = jnp.sum(ex[:, :layer + 1, :], axis=1) / jnp.sum(ex, axis=1)

    for d, (z0, k0, h0, m0) in enumerate(((Z_FF, ZB_KF, ZB_GHF, ZB_GMF),
                                          (Z_FB, ZB_KB, ZB_GHB, ZB_GMB))):
        fr = z[:, z0:z0 + HG_WIDTH]
        lbd = lb[d:d + 1]
        e = jnp.exp(-jnp.abs(fr))
        r = 1.0 / (1.0 + e)
        pos = fr >= 0
        sig = jnp.where(pos, r, e * r)
        sig_n = jnp.where(pos, e * r, r)
        g2 = jnp.log(lbd + (1.0 - lbd) * sig) * LOG2E
        hi = g2.astype(BF16)
        zb_ref[0, :, k0:k0 + HG_WIDTH] = ((1.0 - lbd) * sig_n).astype(BF16)
        zb_ref[0, :, h0:h0 + HG_WIDTH] = hi
        zb_ref[0, :, m0:m0 + HG_WIDTH] = (g2 - hi.astype(F32)).astype(BF16)


def _mixin(x, sh, sc, ng, w, hgrn_lb, qg, kg, layer):
    b, s, d = x.shape
    tm = FFN_TM
    vec = pl.BlockSpec((1, 1, d), lambda i, j: (i, 0, 0))
    row = lambda n: pl.BlockSpec((1, n), lambda i, j: (0, 0))
    return pl.pallas_call(
        functools.partial(_mixin_kernel, layer=layer),
        grid=(b, s // tm),
        in_specs=[pl.BlockSpec((1, tm, d), lambda i, j: (i, j, 0)),
                  vec, vec, row(d),
                  pl.BlockSpec(w.shape, lambda i, j: (0, 0)),
                  pl.BlockSpec(hgrn_lb.shape, lambda i, j: (0, 0, 0)),
                  row(LANES), row(LANES)],
        out_specs=pl.BlockSpec((1, tm, ZB_WIDTH), lambda i, j: (i, j, 0)),
        out_shape=jax.ShapeDtypeStruct((b, s, ZB_WIDTH), BF16),
        compiler_params=_cparams(("arbitrary", "arbitrary")),
        name="mix_in",
    )(x, sh, sc, ng, w, hgrn_lb, qg, kg)


SMALL_W = (1, 2, 4, 8)
GROUP = PACKED_ROWS
FINISH_CHUNKS = 8
PAIR_ORDER = (("a", 0), ("t", 0), ("b", 0), ("a", 1), ("t", 1), ("b", 1))


def _hgrn_consts(c):
    t = np.arange(c)[:, None]
    s = np.arange(c)[None, :]
    x = t ^ s
    lev = np.where(x > 0, np.floor(np.log2(np.maximum(x, 1))), -1).astype(np.int32)
    lev_f = np.where(t > s, lev, np.where(t == s, -1, -2)).astype(np.int32)
    lev_b = lev_f.T.copy()

    def exponent_rows(w, rev):
        m = np.zeros((GROUP, GROUP), np.float32)
        for r in range(GROUP):
            b0 = (r // (2 * w)) * 2 * w
            if not rev:
                ref = b0 + w - 1
                lo_u, hi_u = (ref + 1, r) if r > ref else (r + 1, ref)
            else:
                ref = b0 + w
                lo_u, hi_u = (r, ref - 1) if r < ref else (ref, r - 1)
            m[r, lo_u:hi_u + 1] = 1.0
        return m

    cums, smalls = [], []
    for rev in (False, True):
        cum = ((s >= t) if rev else (s <= t)).astype(np.float32)
        cums.append(np.concatenate([cum, cum], axis=1))
        rows = np.concatenate([exponent_rows(w, rev) for w in SMALL_W])
        smalls.append(np.concatenate([rows, rows], axis=1))
    return (jnp.asarray(np.stack([lev_f, lev_b])),
            jnp.asarray(np.stack(cums), dtype=BF16),
            jnp.asarray(np.stack(smalls), dtype=BF16))


def _neg_abs(x):
    bits = lax.bitcast_convert_type(x, jnp.uint32) | jnp.uint32(0x80000000)
    return lax.bitcast_convert_type(bits, F32)


def _ref_rows(g_cum, w, rev, c):
    idx = w if rev else w - 1
    g3 = g_cum.reshape(c // (2 * w), 2 * w, LANES)
    return jnp.broadcast_to(g3[:, idx:idx + 1, :], g3.shape).reshape(c, LANES)


def _hgrn_stage_a(tcum, tsmall, gh, gm, gc_ref, ge_ref, c):
    gc_ref[...] = jnp.dot(tcum, jnp.concatenate([gh, gm], axis=0), preferred_element_type=F32)
    groups = [slice(g * GROUP, (g + 1) * GROUP) for g in range(c // GROUP)]
    wide = jnp.concatenate([jnp.concatenate([gh[g] for g in groups], axis=1),
                            jnp.concatenate([gm[g] for g in groups], axis=1)], axis=0)
    ge_ref[...] = jnp.dot(tsmall, wide, preferred_element_type=F32)


def _hgrn_stage_b(q, k, v, gc_ref, ge_ref, lev, st_ref, a_ref, ab_ref, oi_ref, rev, c):
    nt = (((1,), (1,)), ((), ()))
    tn = (((0,), (0,)), ((), ()))
    g_cum = gc_ref[...]
    g_last = g_cum[0:1, :] if rev else g_cum[c - 1:c, :]
    qf, kf = q.astype(F32), k.astype(F32)

    st = st_ref[...]
    qi = (qf * jnp.exp2(g_cum)).astype(BF16)
    oi_ref[...] = lax.dot_general(qi, st.astype(BF16), nt, preferred_element_type=F32)
    kd = (kf * jnp.exp2(g_last - g_cum)).astype(BF16)
    u_t = lax.dot_general(v, kd, tn, preferred_element_type=F32)
    st_ref[...] = jnp.exp2(g_last) * st + u_t

    w = c // 2
    while w >= GROUP:
        x = jnp.exp2(_neg_abs(g_cum - _ref_rows(g_cum, w, rev, c)))
        q_rows, k_rows = [], []
        for b in range(c // (2 * w)):
            lo_half = slice(b * 2 * w, b * 2 * w + w)
            hi_half = slice(b * 2 * w + w, (b + 1) * 2 * w)
            qs = lo_half if rev else hi_half
            q_rows.append(qf[qs] * x[qs])
            k_rows.extend([kf[lo_half], kf[hi_half] * x[hi_half]] if rev
                          else [kf[lo_half] * x[lo_half], kf[hi_half]])
        p = lax.dot_general(jnp.concatenate(q_rows, axis=0).astype(BF16),
                            jnp.concatenate(k_rows, axis=0).astype(BF16), nt,
                            preferred_element_type=F32)
        for b in range(c // (2 * w)):
            lo_half = slice(b * 2 * w, b * 2 * w + w)
            hi_half = slice(b * 2 * w + w, (b + 1) * 2 * w)
            qs, ks = (lo_half, hi_half) if rev else (hi_half, lo_half)
            a_ref[qs, ks] = p[b * w:(b + 1) * w, ks]
        w //= 2

    a = jnp.where(lev == -1, lax.dot_general(q, k, nt, preferred_element_type=F32), 0.0)
    for li in range(len(SMALL_W)):
        e = jnp.concatenate([ge_ref[li * GROUP:(li + 1) * GROUP, g * LANES:(g + 1) * LANES]
                             for g in range(c // GROUP)], axis=0)
        x = jnp.exp2(e)
        p = lax.dot_general((qf * x).astype(BF16), (kf * x).astype(BF16), nt,
                            preferred_element_type=F32)
        a = jnp.where(lev == li, p, a)
    for b in range(c // GROUP):
        blk = slice(b * GROUP, (b + 1) * GROUP)
        a_ref[blk, blk] = a[blk, blk]
    ab_ref[...] = a_ref[...].astype(BF16)


def _hgrn_kernel(q_ref, i_ref, sg_ref, kf_ref, kb_ref, ghf_ref, gmf_ref, ghb_ref, gmb_ref,
                 ng_ref, lev_ref, tc_ref, ts_ref, o_ref,
                 acc_ref, st_ref, a_ref, ab_ref, oi_ref, gc_ref, ge_ref, *, c, nc):
    for ref in (acc_ref, st_ref, a_ref, ab_ref, oi_ref):
        ref[...] = jnp.zeros_like(ref)
    k_refs = (kf_ref, kb_ref)
    g_refs = ((ghf_ref, gmf_ref), (ghb_ref, gmb_ref))

    def rows_of(p, d):
        j = p if d == 0 else nc - 1 - p
        return pl.ds(pl.multiple_of(j * c, c), c)

    def stage_a(p, slot, d):
        rows = rows_of(p, d)
        _hgrn_stage_a(tc_ref[d], ts_ref[d], g_refs[d][0][0, rows, :], g_refs[d][1][0, rows, :],
                      gc_ref.at[slot, d], ge_ref.at[slot, d], c)

    def stage_t(p, d):
        rows = rows_of(p, d)
        acc_ref[rows, :] += oi_ref[d] + jnp.dot(ab_ref[d], i_ref[0, rows, :],
                                                preferred_element_type=F32)

    def stage_b(p, slot, d):
        rows = rows_of(p, d)
        _hgrn_stage_b(q_ref[0, rows, :], k_refs[d][0, rows, :], i_ref[0, rows, :],
                      gc_ref.at[slot, d], ge_ref.at[slot, d], lev_ref[d], st_ref.at[d],
                      a_ref.at[d], ab_ref.at[d], oi_ref.at[d], d == 1, c)

    def pair(p, slot):
        p_next, p_prev = jnp.minimum(p + 1, nc - 1), jnp.maximum(p - 1, 0)
        for stage, d in PAIR_ORDER:
            if stage == "a":
                stage_a(p_next, 1 - slot, d)
            elif stage == "t":
                stage_t(p_prev, d)
            else:
                stage_b(p, slot, d)

    def two_pairs(i2, carry):
        pair(2 * i2, 0)
        pair(2 * i2 + 1, 1)
        return carry

    for d in range(2):
        stage_a(0, 0, d)
    lax.fori_loop(0, nc // 2, two_pairs, 0)
    for d in range(2):
        stage_t(nc - 1, d)

    ng = ng_ref[...]
    rows_fin = FINISH_CHUNKS * c

    def finish(j, carry):
        rows = pl.ds(pl.multiple_of(j * rows_fin, rows_fin), rows_fin)
        o = acc_ref[rows, :]
        ms = jnp.mean(o * o, axis=-1, keepdims=True)
        o = o * lax.rsqrt(ms + EPS) * ng
        o_ref[0, rows, :] = (o * sg_ref[0, rows, :].astype(F32)).astype(o_ref.dtype)
        return carry

    lax.fori_loop(0, nc // FINISH_CHUNKS, finish, 0)


def _hgrn(zb, norm_g):
    b, s, _ = zb.shape
    c = HG_CHUNK
    nc = s // c
    assert nc % 2 == 0 and c % (2 * GROUP) == 0
    lev, tcum, tsmall = _hgrn_consts(c)
    n_small = len(SMALL_W) * GROUP

    def col(first_col):
        base = first_col // LANES
        return pl.BlockSpec((1, s, LANES), lambda i, h: (i, 0, base + h))

    const3 = lambda a: pl.BlockSpec(a.shape, lambda i, h: (0, 0, 0))
    return pl.pallas_call(
        functools.partial(_hgrn_kernel, c=c, nc=nc),
        grid=(b, HG_HEADS),
        in_specs=[col(Z_Q), col(Z_I), col(Z_G), col(ZB_KF), col(ZB_KB),
                  col(ZB_GHF), col(ZB_GMF), col(ZB_GHB), col(ZB_GMB),
                  pl.BlockSpec((1, LANES), lambda i, h: (0, h)),
                  const3(lev), const3(tcum), const3(tsmall)],
        out_specs=pl.BlockSpec((1, s, LANES), lambda i, h: (i, 0, h)),
        out_shape=jax.ShapeDtypeStruct((b, s, HG_WIDTH), BF16),
        scratch_shapes=[pltpu.VMEM((s, HG_DIM), F32),
                        pltpu.VMEM((2, HG_DIM, HG_DIM), F32),
                        pltpu.VMEM((2, c, c), F32),
                        pltpu.VMEM((2, c, c), BF16),
                        pltpu.VMEM((2, c, HG_DIM), F32),
                        pltpu.VMEM((2, 2, c, LANES), F32),
                        pltpu.VMEM((2, 2, n_small, (c // GROUP) * LANES), F32)],
        compiler_params=_cparams(("arbitrary", "arbitrary")),
        name="hgrn",
    )(zb, zb, zb, zb, zb, zb, zb, zb, zb, norm_g, lev, tcum, tsmall)


def _attn_kernel(sink_ref, q_ref, kp_ref, ko_ref, kn_ref, vp_ref, vo_ref, vn_ref,
                 bias_ref, o_ref, lg_ref, *, nb):
    n = pl.program_id(1)
    half = LANES // 2
    kb = jnp.concatenate([kp_ref[0], ko_ref[0], kn_ref[0]], axis=0)
    vb = jnp.concatenate([vp_ref[0], vo_ref[0], vn_ref[0]], axis=0)
    lo_k = lax.broadcasted_iota(jnp.int32, kb.shape, 1) < half
    lo_q = lax.broadcasted_iota(jnp.int32, (BLOCK, LANES), 1) < half
    zero = jnp.zeros_like(kb)

    k1 = (jnp.where(lo_k, kb, zero), jnp.where(lo_k, zero, kb))
    v0lo, v1hi = jnp.where(lo_k, vb, zero), jnp.where(lo_k, zero, vb)
    v_lo = (v0lo, pltpu.roll(v1hi, half, axis=1))
    v_hi = (pltpu.roll(v0lo, half, axis=1), v1hi)

    edge = jnp.where(n == 0, 0, jnp.where(n == nb - 1, 2, 1))
    row_lo = lax.broadcasted_iota(jnp.int32, (2 * BLOCK, 1), 0) < BLOCK
    nt = (((1,), (1,)), ((), ()))

    for m in range(ATT_Q_HEADS // 2):
        j = (2 * m) // ATT_GROUP
        qp = q_ref[0, :, m * LANES:(m + 1) * LANES]
        qr = pltpu.roll(qp, half, axis=1)
        q2 = jnp.concatenate([qp, qr] if j == 0 else [qr, qp], axis=0)
        lg_ref[m] = lax.dot_general(q2, k1[j], nt, preferred_element_type=F32) + bias_ref[edge, m]

    for m in range(ATT_Q_HEADS // 2):
        j = (2 * m) // ATT_GROUP
        logits = lg_ref[m]
        sink = jnp.where(row_lo, sink_ref[2 * m], sink_ref[2 * m + 1])
        mx = jnp.maximum(jnp.max(logits, axis=-1, keepdims=True), sink)
        p = jnp.exp2(logits - mx)
        den = jnp.sum(p, axis=-1, keepdims=True) + jnp.exp2(sink - mx)
        pb = p.astype(BF16)
        rden = 1.0 / den
        o = (jnp.dot(pb[:BLOCK], v_lo[j], preferred_element_type=F32)
             + jnp.dot(pb[BLOCK:], v_hi[j], preferred_element_type=F32))
        o = o * jnp.where(lo_q, rden[:BLOCK], rden[BLOCK:])
        o_ref[0, :, m * LANES:(m + 1) * LANES] = o.astype(o_ref.dtype)


def _attn(zb, sink2, bias):
    b, s, _ = zb.shape
    nb = s // BLOCK
    assert nb >= 2
    q_col = Z_AQ // ATT_WIDTH
    k_col = Z_AK // KV_WIDTH
    v_col = Z_AV // KV_WIDTH

    def kv(col, shift):
        return pl.BlockSpec(
            (1, BLOCK, KV_WIDTH),
            lambda i, n: (i, jnp.clip(n + shift, 0, nb - 1), col))

    return pl.pallas_call(
        functools.partial(_attn_kernel, nb=nb),
        grid=(b, nb),
        in_specs=[pl.BlockSpec(memory_space=pltpu.SMEM),
                  pl.BlockSpec((1, BLOCK, ATT_WIDTH), lambda i, n: (i, n, q_col)),
                  kv(k_col, -1), kv(k_col, 0), kv(k_col, 1),
                  kv(v_col, -1), kv(v_col, 0), kv(v_col, 1),
                  pl.BlockSpec(bias.shape, lambda i, n: (0, 0, 0, 0))],
        out_specs=pl.BlockSpec((1, BLOCK, ATT_WIDTH), lambda i, n: (i, n, 0)),
        out_shape=jax.ShapeDtypeStruct((b, s, ATT_WIDTH), BF16),
        scratch_shapes=[pltpu.VMEM((ATT_Q_HEADS // 2, 2 * BLOCK, 3 * BLOCK), F32)],
        compiler_params=_cparams(("arbitrary", "arbitrary")),
        name="attn",
    )(sink2, zb, zb, zb, zb, zb, zb, zb, bias)


def _mixout_kernel(x_ref, hg_ref, at_ref, gt_ref, w_ref, o_ref):
    mixed = (jnp.dot(hg_ref[0], w_ref[:HG_WIDTH, :], preferred_element_type=F32)
             + jnp.dot(at_ref[0], w_ref[HG_WIDTH:, :], preferred_element_type=F32))
    o_ref[0] = x_ref[0] + gt_ref[0] * mixed


def _mixout(x, o_hg, attn, gt, w):
    b, s, d = x.shape
    tm = FFN_TM
    half = pl.BlockSpec((1, tm, HG_WIDTH), lambda i, j: (i, j, 0))
    return pl.pallas_call(
        _mixout_kernel,
        grid=(b, s // tm),
        in_specs=[pl.BlockSpec((1, tm, d), lambda i, j: (i, j, 0)),
                  half, half,
                  pl.BlockSpec((1, 1, d), lambda i, j: (i, 0, 0)),
                  pl.BlockSpec(w.shape, lambda i, j: (0, 0))],
        out_specs=pl.BlockSpec((1, tm, d), lambda i, j: (i, j, 0)),
        out_shape=jax.ShapeDtypeStruct(x.shape, F32),
        compiler_params=_cparams(("arbitrary", "arbitrary")),
        name="mix_out",
    )(x, o_hg, attn, gt, w)


def _mixin_weight(w):
    hw = HG_WIDTH
    cols = [w[:, 0:hw], w[:, 3 * hw:4 * hw], w[:, 4 * hw:5 * hw], w[:, 5 * hw:],
            w[:, hw:3 * hw]]
    return jnp.concatenate(cols, axis=1).astype(BF16)


def kernel(x, c, w_ada, b_ada, norm_g, w_ffn1_in, w_ffn1_out, w_ffn2_in, w_ffn2_out,
           w_mix_in, w_mix_out, hgrn_lb, hgrn_norm_g, qk_norm_g, attn_sink, rel_bias):
    b, s, d = x.shape
    depth = w_ada.shape[0]
    bias = _bias_tile(rel_bias)
    c_pad = jnp.pad(c.astype(F32), ((0, SUBLANES - b), (0, 0)))
    for l in range(depth):
        mods = _ada(c_pad, w_ada[l], b_ada[l][None, :])[:b]
        sh1, sc1, g1, sh2, sc2, g2, sh3, sc3, g3 = [
            mods[:, i * d:(i + 1) * d][:, None, :] for i in range(N_MOD)]
        x = _ffn(x, sh1, sc1, g1, norm_g[l, 0][None, :],
                 w_ffn1_in[l].astype(BF16), w_ffn1_out[l].astype(BF16))
        qg = (jnp.tile(qk_norm_g[l, 0], 2) * (LOG2E / math.sqrt(ATT_HEAD_DIM)))[None, :]
        kg = jnp.tile(qk_norm_g[l, 1], 2)[None, :]
        zb = _mixin(x, sh2, sc2, norm_g[l, 1][None, :], _mixin_weight(w_mix_in[l]),
                    hgrn_lb, qg, kg, l)
        o_hg = _hgrn(zb, hgrn_norm_g[l][None, :])
        att = _attn(zb, attn_sink[l] * LOG2E, bias)
        x = _mixout(x, o_hg, att, g2, w_mix_out[l].astype(BF16))
        x = _ffn(x, sh3, sc3, g3, norm_g[l, 2][None, :],
                 w_ffn2_in[l].astype(BF16), w_ffn2_out[l].astype(BF16))
    return x
```

```python
import functools
import math

import numpy as np
import jax
import jax.numpy as jnp
from jax import lax
from jax.experimental import pallas as pl
from jax.experimental.pallas import tpu as pltpu

F32 = jnp.float32
BF16 = jnp.bfloat16

D_MODEL = 1024
HG_HEADS = 4
HG_DIM = 128
HG_WIDTH = HG_HEADS * HG_DIM
ATT_Q_HEADS = 8
ATT_KV_HEADS = 2
ATT_HEAD_DIM = 64
ATT_GROUP = ATT_Q_HEADS // ATT_KV_HEADS
ATT_WIDTH = ATT_Q_HEADS * ATT_HEAD_DIM
KV_WIDTH = ATT_KV_HEADS * ATT_HEAD_DIM
WINDOW = 128
BLOCK = 128
NUM_BUCKETS = 32
MAX_DISTANCE = 128
D_FF = 2816
N_MOD = 9
EPS = 1e-6

LANES = 128
SUBLANES = 8
PACKED_ROWS = 16
VMEM_LIMIT = 56 * 1024 * 1024

FFN_TM = 256
HG_CHUNK = 128
NEG_INF = float("-inf")
LOG2E = 1.0 / math.log(2.0)

Z_Q, Z_I, Z_G, Z_AQ = 0, HG_WIDTH, 2 * HG_WIDTH, 3 * HG_WIDTH
Z_AK = Z_AQ + ATT_WIDTH
Z_AV = Z_AK + KV_WIDTH
Z_FF = Z_AV + KV_WIDTH
Z_FB = Z_FF + HG_WIDTH
(HK_Q, HK_I, HK_G, HK_KF, HK_KB, HK_GHF, HK_GMF, HK_GHB, HK_GMB) = range(9)
N_HK = 9


def _cparams(sem, flags=None):
    return pltpu.CompilerParams(dimension_semantics=sem, vmem_limit_bytes=VMEM_LIMIT, flags=flags)


def _sigmoid(x):
    return 1.0 / (1.0 + jnp.exp(-x))


def _ada_kernel(c_ref, w_ref, b_ref, o_ref):
    c = c_ref[...]
    ca = c * _sigmoid(c)
    o_ref[...] = jnp.dot(ca, w_ref[...], precision=lax.Precision.HIGHEST,
                         preferred_element_type=F32) + b_ref[...]


def _ada(c_pad, w, b):
    rows, d = c_pad.shape
    n = w.shape[1]
    tn = 1024
    return pl.pallas_call(
        _ada_kernel,
        grid=(n // tn,),
        in_specs=[pl.BlockSpec((rows, d), lambda j: (0, 0)),
                  pl.BlockSpec((d, tn), lambda j: (0, j)),
                  pl.BlockSpec((1, tn), lambda j: (0, j))],
        out_specs=pl.BlockSpec((rows, tn), lambda j: (0, j)),
        out_shape=jax.ShapeDtypeStruct((rows, n), F32),
        compiler_params=_cparams(("arbitrary",)),
        name="ada",
    )(c_pad, w, b)


def _t5_bucket_np(rel):
    nb = NUM_BUCKETS // 2
    max_exact = nb // 2
    ret = (rel > 0).astype(np.int32) * nb
    n = np.abs(rel)
    ratio = np.maximum(n, 1).astype(np.float32) / np.float32(max_exact)
    large = max_exact + (np.log(ratio) / np.float32(math.log(MAX_DISTANCE / max_exact))
                         * np.float32(nb - max_exact)).astype(np.int32)
    large = np.minimum(large, nb - 1)
    return ret + np.where(n < max_exact, n, large)


def _bias_kernel(rbt_ref, oh_ref, msk_ref, o_ref):
    b = jnp.dot(rbt_ref[...], oh_ref[...], precision=lax.Precision.HIGHEST,
                preferred_element_type=F32)
    o_ref[...] = (b * LOG2E)[None] + msk_ref[...]


def _bias_tile(rel_bias):
    kcol = np.arange(3 * BLOCK)[None, :]
    rel = (kcol - BLOCK) - np.arange(BLOCK)[:, None]
    bucket = _t5_bucket_np(rel).reshape(-1)
    onehot_t = (np.arange(NUM_BUCKETS)[:, None] == bucket[None, :]).astype(np.float32)
    window = np.abs(rel) <= WINDOW
    valid = np.stack([window & (kcol >= BLOCK), window, window & (kcol < 2 * BLOCK)])
    mask = np.where(valid, 0.0, NEG_INF).astype(np.float32).reshape(3, 1, -1)
    ncol = BLOCK * 3 * BLOCK
    tc = ncol // 4
    out = pl.pallas_call(
        _bias_kernel,
        grid=(ncol // tc,),
        in_specs=[pl.BlockSpec((ATT_Q_HEADS, NUM_BUCKETS), lambda i: (0, 0)),
                  pl.BlockSpec((NUM_BUCKETS, tc), lambda i: (0, i)),
                  pl.BlockSpec((3, 1, tc), lambda i: (0, 0, i))],
        out_specs=pl.BlockSpec((3, ATT_Q_HEADS, tc), lambda i: (0, 0, i)),
        out_shape=jax.ShapeDtypeStruct((3, ATT_Q_HEADS, ncol), F32),
        compiler_params=_cparams(("arbitrary",)),
        name="bias_tile",
    )(rel_bias.astype(F32).T, jnp.asarray(onehot_t), jnp.asarray(mask))
    return out.reshape(3, ATT_Q_HEADS // 2, 2 * BLOCK, 3 * BLOCK)


def _norm_mod(x, ng, sh, sc):
    ms = jnp.mean(x * x, axis=-1, keepdims=True)
    y = x * lax.rsqrt(ms + EPS) * ng
    return y * (1.0 + sc) + sh


def _ffn_kernel(*refs, mix):
    if mix:
        hg_ref, at_ref, gm_ref, wm_ref, *refs = refs
    x_ref, sh_ref, sc_ref, gt_ref, ng_ref, win_ref, wout_ref, o_ref = refs
    x = x_ref[0]
    if mix:
        mixed = (jnp.dot(hg_ref[0], wm_ref[:HG_WIDTH, :], preferred_element_type=F32)
                 + jnp.dot(at_ref[0], wm_ref[HG_WIDTH:, :], preferred_element_type=F32))
        x = x + gm_ref[0] * mixed
    h = _norm_mod(x, ng_ref[...], sh_ref[0], sc_ref[0]).astype(BF16)
    gu = jnp.dot(h, win_ref[...], preferred_element_type=F32)
    g = gu[:, :D_FF]
    u = gu[:, D_FF:]
    act = (g * _sigmoid(g) * u).astype(BF16)
    y = jnp.dot(act, wout_ref[...], preferred_element_type=F32)
    o_ref[0] = x + 0.5 * gt_ref[0] * y


def _ffn(x, sh, sc, gt, ng, w_in, w_out, mix=None):
    b, s, d = x.shape
    tm = FFN_TM
    vec = pl.BlockSpec((1, 1, d), lambda i, j: (i, 0, 0))
    whole = lambda a: pl.BlockSpec(a.shape, lambda i, j: (0, 0))
    args = [x, sh, sc, gt, ng, w_in, w_out]
    specs = [pl.BlockSpec((1, tm, d), lambda i, j: (i, j, 0)), vec, vec, vec,
             whole(ng), whole(w_in), whole(w_out)]
    if mix is not None:
        o_hg, att, gm, wm = mix
        half = pl.BlockSpec((1, tm, HG_WIDTH), lambda i, j: (i, j, 0))
        args = [o_hg, att, gm, wm] + args
        specs = [half, half, vec, whole(wm)] + specs
    return pl.pallas_call(
        functools.partial(_ffn_kernel, mix=mix is not None),
        grid=(b, s // tm),
        in_specs=specs,
        out_specs=pl.BlockSpec((1, tm, d), lambda i, j: (i, j, 0)),
        out_shape=jax.ShapeDtypeStruct(x.shape, F32),
        compiler_params=_cparams(("arbitrary", "arbitrary")),
        name="ffn_mix" if mix is not None else "ffn",
    )(*args)


def _half_norm(x, gain, lo):
    x2 = x * x
    s_lo = jnp.sum(jnp.where(lo, x2, 0.0), axis=-1, keepdims=True)
    s_hi = jnp.sum(jnp.where(lo, 0.0, x2), axis=-1, keepdims=True)
    ms = jnp.where(lo, s_lo, s_hi) * (1.0 / ATT_HEAD_DIM)
    return x * lax.rsqrt(ms + EPS) * gain


def _mixin_kernel(x_ref, sh_ref, sc_ref, ng_ref, w_ref, lb_ref, qg_ref, kg_ref,
                  hg_ref, aq_ref, ak_ref, av_ref, *, layer):
    h = _norm_mod(x_ref[0], ng_ref[...], sh_ref[0], sc_ref[0]).astype(BF16)
    z = jnp.dot(h, w_ref[...], preferred_element_type=F32)

    def put_heads(kind, val):
        for hh in range(HG_HEADS):
            hg_ref[0, kind * HG_HEADS + hh] = val[:, hh * HG_DIM:(hh + 1) * HG_DIM].astype(BF16)

    q = z[:, Z_Q:Z_I]
    put_heads(HK_Q, q * _sigmoid(q))
    put_heads(HK_I, z[:, Z_I:Z_G])
    g = z[:, Z_G:Z_AQ]
    put_heads(HK_G, g * _sigmoid(g))

    lo = lax.broadcasted_iota(jnp.int32, (z.shape[0], LANES), 1) < LANES // 2
    for m in range(ATT_WIDTH // LANES):
        cols = slice(m * LANES, (m + 1) * LANES)
        aq_ref[0, :, cols] = _half_norm(z[:, Z_AQ + m * LANES:Z_AQ + (m + 1) * LANES],
                                        qg_ref[...], lo).astype(BF16)
    ak_ref[0] = _half_norm(z[:, Z_AK:Z_AV], kg_ref[...], lo).astype(BF16)
    av_ref[0] = z[:, Z_AV:Z_FF].astype(BF16)

    raw = lb_ref[...]
    mx = jnp.max(raw, axis=1, keepdims=True)
    ex = jnp.exp(raw - mx)
    lb = jnp.sum(ex[:, :layer + 1, :], axis=1) / jnp.sum(ex, axis=1)

    for d, (z0, kk, kh, km) in enumerate(((Z_FF, HK_KF, HK_GHF, HK_GMF),
                                          (Z_FB, HK_KB, HK_GHB, HK_GMB))):
        fr = z[:, z0:z0 + HG_WIDTH]
        lbd = lb[d:d + 1]
        e = jnp.exp(-jnp.abs(fr))
        r = 1.0 / (1.0 + e)
        pos = fr >= 0
        sig = jnp.where(pos, r, e * r)
        sig_n = jnp.where(pos, e * r, r)
        g2 = jnp.log(lbd + (1.0 - lbd) * sig) * LOG2E
        hi = g2.astype(BF16)
        put_heads(kk, (1.0 - lbd) * sig_n)
        put_heads(kh, hi)
        put_heads(km, g2 - hi.astype(F32))


def _mixin(x, sh, sc, ng, w, hgrn_lb, qg, kg, layer):
    b, s, d = x.shape
    tm = FFN_TM
    vec = pl.BlockSpec((1, 1, d), lambda i, j: (i, 0, 0))
    row = lambda n: pl.BlockSpec((1, n), lambda i, j: (0, 0))
    rows3 = lambda n: pl.BlockSpec((1, tm, n), lambda i, j: (i, j, 0))
    n_hg = N_HK * HG_HEADS
    return pl.pallas_call(
        functools.partial(_mixin_kernel, layer=layer),
        grid=(b, s // tm),
        in_specs=[pl.BlockSpec((1, tm, d), lambda i, j: (i, j, 0)),
                  vec, vec, row(d),
                  pl.BlockSpec(w.shape, lambda i, j: (0, 0)),
                  pl.BlockSpec(hgrn_lb.shape, lambda i, j: (0, 0, 0)),
                  row(LANES), row(LANES)],
        out_specs=[pl.BlockSpec((1, n_hg, tm, HG_DIM), lambda i, j: (i, 0, j, 0)),
                   rows3(ATT_WIDTH), rows3(KV_WIDTH), rows3(KV_WIDTH)],
        out_shape=[jax.ShapeDtypeStruct((b, n_hg, s, HG_DIM), BF16),
                   jax.ShapeDtypeStruct((b, s, ATT_WIDTH), BF16),
                   jax.ShapeDtypeStruct((b, s, KV_WIDTH), BF16),
                   jax.ShapeDtypeStruct((b, s, KV_WIDTH), BF16)],
        compiler_params=_cparams(("arbitrary", "arbitrary")),
        name="mix_in",
    )(x, sh, sc, ng, w, hgrn_lb, qg, kg)


SMALL_W = (1, 2, 4, 8)
GROUP = PACKED_ROWS
FINISH_CHUNKS = 8
PAIR_ORDER = (("a", 0), ("t", 0), ("b", 0), ("a", 1), ("t", 1), ("b", 1))


def _hgrn_consts(c):
    t = np.arange(c)[:, None]
    s = np.arange(c)[None, :]
    x = t ^ s
    lev = np.where(x > 0, np.floor(np.log2(np.maximum(x, 1))), -1).astype(np.int32)
    lev_f = np.where(t > s, lev, np.where(t == s, -1, -2)).astype(np.int32)
    lev_b = lev_f.T.copy()

    def exponent_rows(w, rev):
        m = np.zeros((GROUP, GROUP), np.float32)
        for r in range(GROUP):
            b0 = (r // (2 * w)) * 2 * w
            if not rev:
                ref = b0 + w - 1
                lo_u, hi_u = (ref + 1, r) if r > ref else (r + 1, ref)
            else:
                ref = b0 + w
                lo_u, hi_u = (r, ref - 1) if r < ref else (ref, r - 1)
            m[r, lo_u:hi_u + 1] = 1.0
        return m

    cums, smalls = [], []
    for rev in (False, True):
        cum = ((s >= t) if rev else (s <= t)).astype(np.float32)
        cums.append(np.concatenate([cum, cum], axis=1))
        rows = np.concatenate([exponent_rows(w, rev) for w in SMALL_W])
        smalls.append(np.concatenate([rows, rows], axis=1))
    return (jnp.asarray(np.stack([lev_f, lev_b])),
            jnp.asarray(np.stack(cums), dtype=BF16),
            jnp.asarray(np.stack(smalls), dtype=BF16))


def _neg_abs(x):
    bits = lax.bitcast_convert_type(x, jnp.uint32) | jnp.uint32(0x80000000)
    return lax.bitcast_convert_type(bits, F32)


def _ref_rows(g_cum, w, rev, c):
    idx = w if rev else w - 1
    g3 = g_cum.reshape(c // (2 * w), 2 * w, LANES)
    return jnp.broadcast_to(g3[:, idx:idx + 1, :], g3.shape).reshape(c, LANES)


def _hgrn_stage_a(tcum, tsmall, gh, gm, gc_ref, ge_ref, c):
    gc_ref[...] = jnp.dot(tcum, jnp.concatenate([gh, gm], axis=0), preferred_element_type=F32)
    groups = [slice(g * GROUP, (g + 1) * GROUP) for g in range(c // GROUP)]
    wide = jnp.concatenate([jnp.concatenate([gh[g] for g in groups], axis=1),
                            jnp.concatenate([gm[g] for g in groups], axis=1)], axis=0)
    ge_ref[...] = jnp.dot(tsmall, wide, preferred_element_type=F32)


def _hgrn_stage_b(q, k, v, gc_ref, ge_ref, lev, st_ref, a_ref, ab_ref, oi_ref, rev, c):
    nt = (((1,), (1,)), ((), ()))
    tn = (((0,), (0,)), ((), ()))
    g_cum = gc_ref[...]
    g_last = g_cum[0:1, :] if rev else g_cum[c - 1:c, :]
    qf, kf = q.astype(F32), k.astype(F32)

    st = st_ref[...]
    qi = (qf * jnp.exp2(g_cum)).astype(BF16)
    oi_ref[...] = lax.dot_general(qi, st.astype(BF16), nt, preferred_element_type=F32)
    kd = (kf * jnp.exp2(g_last - g_cum)).astype(BF16)
    u_t = lax.dot_general(v, kd, tn, preferred_element_type=F32)
    st_ref[...] = jnp.exp2(g_last) * st + u_t

    w = c // 2
    while w >= GROUP:
        x = jnp.exp2(_neg_abs(g_cum - _ref_rows(g_cum, w, rev, c)))
        q_rows, k_rows = [], []
        for b in range(c // (2 * w)):
            lo_half = slice(b * 2 * w, b * 2 * w + w)
            hi_half = slice(b * 2 * w + w, (b + 1) * 2 * w)
            qs = lo_half if rev else hi_half
            q_rows.append(qf[qs] * x[qs])
            k_rows.extend([kf[lo_half], kf[hi_half] * x[hi_half]] if rev
                          else [kf[lo_half] * x[lo_half], kf[hi_half]])
        p = lax.dot_general(jnp.concatenate(q_rows, axis=0).astype(BF16),
                            jnp.concatenate(k_rows, axis=0).astype(BF16), nt,
                            preferred_element_type=F32)
        for b in range(c // (2 * w)):
            lo_half = slice(b * 2 * w, b * 2 * w + w)
            hi_half = slice(b * 2 * w + w, (b + 1) * 2 * w)
            qs, ks = (lo_half, hi_half) if rev else (hi_half, lo_half)
            a_ref[qs, ks] = p[b * w:(b + 1) * w, ks]
        w //= 2

    a = jnp.where(lev == -1, lax.dot_general(q, k, nt, preferred_element_type=F32), 0.0)
    for li in range(len(SMALL_W)):
        e = jnp.concatenate([ge_ref[li * GROUP:(li + 1) * GROUP, g * LANES:(g + 1) * LANES]
                             for g in range(c // GROUP)], axis=0)
        x = jnp.exp2(e)
        p = lax.dot_general((qf * x).astype(BF16), (kf * x).astype(BF16), nt,
                            preferred_element_type=F32)
        a = jnp.where(lev == li, p, a)
    for b in range(c // GROUP):
        blk = slice(b * GROUP, (b + 1) * GROUP)
        a_ref[blk, blk] = a[blk, blk]
    ab_ref[...] = a_ref[...].astype(BF16)


def _hgrn_kernel(q_ref, i_ref, sg_ref, kf_ref, kb_ref, ghf_ref, gmf_ref, ghb_ref, gmb_ref,
                 ng_ref, lev_ref, tc_ref, ts_ref, o_ref,
                 acc_ref, st_ref, a_ref, ab_ref, oi_ref, gc_ref, ge_ref, *, c, nc):
    for ref in (acc_ref, st_ref, a_ref, ab_ref, oi_ref):
        ref[...] = jnp.zeros_like(ref)
    k_refs = (kf_ref, kb_ref)
    g_refs = ((ghf_ref, gmf_ref), (ghb_ref, gmb_ref))

    def rows_of(p, d):
        j = p if d == 0 else nc - 1 - p
        return pl.ds(pl.multiple_of(j * c, c), c)

    def stage_a(p, slot, d):
        rows = rows_of(p, d)
        _hgrn_stage_a(tc_ref[d], ts_ref[d], g_refs[d][0][0, 0, rows, :], g_refs[d][1][0, 0, rows, :],
                      gc_ref.at[slot, d], ge_ref.at[slot, d], c)

    def stage_t(p, d):
        rows = rows_of(p, d)
        acc_ref[rows, :] += oi_ref[d] + jnp.dot(ab_ref[d], i_ref[0, 0, rows, :],
                                                preferred_element_type=F32)

    def stage_b(p, slot, d):
        rows = rows_of(p, d)
        _hgrn_stage_b(q_ref[0, 0, rows, :], k_refs[d][0, 0, rows, :], i_ref[0, 0, rows, :],
                      gc_ref.at[slot, d], ge_ref.at[slot, d], lev_ref[d], st_ref.at[d],
                      a_ref.at[d], ab_ref.at[d], oi_ref.at[d], d == 1, c)

    def pair(p, slot):
        p_next, p_prev = jnp.minimum(p + 1, nc - 1), jnp.maximum(p - 1, 0)
        for stage, d in PAIR_ORDER:
            if stage == "a":
                stage_a(p_next, 1 - slot, d)
            elif stage == "t":
                stage_t(p_prev, d)
            else:
                stage_b(p, slot, d)

    def two_pairs(i2, carry):
        pair(2 * i2, 0)
        pair(2 * i2 + 1, 1)
        return carry

    for d in range(2):
        stage_a(0, 0, d)
    lax.fori_loop(0, nc // 2, two_pairs, 0)
    for d in range(2):
        stage_t(nc - 1, d)

    ng = ng_ref[...]
    rows_fin = FINISH_CHUNKS * c

    def finish(j, carry):
        rows = pl.ds(pl.multiple_of(j * rows_fin, rows_fin), rows_fin)
        o = acc_ref[rows, :]
        ms = jnp.mean(o * o, axis=-1, keepdims=True)
        o = o * lax.rsqrt(ms + EPS) * ng
        o_ref[0, rows, :] = (o * sg_ref[0, 0, rows, :].astype(F32)).astype(o_ref.dtype)
        return carry

    lax.fori_loop(0, nc // FINISH_CHUNKS, finish, 0)


def _hgrn(hg, norm_g):
    b, _, s, _ = hg.shape
    c = HG_CHUNK
    nc = s // c
    assert nc % 2 == 0 and c % (2 * GROUP) == 0
    lev, tcum, tsmall = _hgrn_consts(c)
    n_small = len(SMALL_W) * GROUP

    def kind(k):
        return pl.BlockSpec((1, 1, s, HG_DIM), lambda i, h: (i, k * HG_HEADS + h, 0, 0))

    const3 = lambda a: pl.BlockSpec(a.shape, lambda i, h: (0, 0, 0))
    return pl.pallas_call(
        functools.partial(_hgrn_kernel, c=c, nc=nc),
        grid=(b, HG_HEADS),
        in_specs=[kind(HK_Q), kind(HK_I), kind(HK_G), kind(HK_KF), kind(HK_KB),
                  kind(HK_GHF), kind(HK_GMF), kind(HK_GHB), kind(HK_GMB),
                  pl.BlockSpec((1, LANES), lambda i, h: (0, h)),
                  const3(lev), const3(tcum), const3(tsmall)],
        out_specs=pl.BlockSpec((1, s, LANES), lambda i, h: (i, 0, h)),
        out_shape=jax.ShapeDtypeStruct((b, s, HG_WIDTH), BF16),
        scratch_shapes=[pltpu.VMEM((s, HG_DIM), F32),
                        pltpu.VMEM((2, HG_DIM, HG_DIM), F32),
                        pltpu.VMEM((2, c, c), F32),
                        pltpu.VMEM((2, c, c), BF16),
                        pltpu.VMEM((2, c, HG_DIM), F32),
                        pltpu.VMEM((2, 2, c, LANES), F32),
                        pltpu.VMEM((2, 2, n_small, (c // GROUP) * LANES), F32)],
        compiler_params=_cparams(("arbitrary", "arbitrary")),
        name="hgrn",
    )(hg, hg, hg, hg, hg, hg, hg, hg, hg, norm_g, lev, tcum, tsmall)


def _attn_kernel(sink_ref, q_ref, kp_ref, ko_ref, kn_ref, vp_ref, vo_ref, vn_ref,
                 bias_ref, o_ref, lg_ref, *, nb):
    n = pl.program_id(1)
    half = LANES // 2
    kb = jnp.concatenate([kp_ref[0], ko_ref[0], kn_ref[0]], axis=0)
    vb = jnp.concatenate([vp_ref[0], vo_ref[0], vn_ref[0]], axis=0)
    lo_k = lax.broadcasted_iota(jnp.int32, kb.shape, 1) < half
    lo_q = lax.broadcasted_iota(jnp.int32, (BLOCK, LANES), 1) < half
    zero = jnp.zeros_like(kb)

    k1 = (jnp.where(lo_k, kb, zero), jnp.where(lo_k, zero, kb))
    v0lo, v1hi = jnp.where(lo_k, vb, zero), jnp.where(lo_k, zero, vb)
    v_lo = (v0lo, pltpu.roll(v1hi, half, axis=1))
    v_hi = (pltpu.roll(v0lo, half, axis=1), v1hi)

    edge = jnp.where(n == 0, 0, jnp.where(n == nb - 1, 2, 1))
    row_lo = lax.broadcasted_iota(jnp.int32, (2 * BLOCK, 1), 0) < BLOCK
    nt = (((1,), (1,)), ((), ()))

    for m in range(ATT_Q_HEADS // 2):
        j = (2 * m) // ATT_GROUP
        qp = q_ref[0, :, m * LANES:(m + 1) * LANES]
        qr = pltpu.roll(qp, half, axis=1)
        q2 = jnp.concatenate([qp, qr] if j == 0 else [qr, qp], axis=0)
        lg_ref[m] = lax.dot_general(q2, k1[j], nt, preferred_element_type=F32) + bias_ref[edge, m]

    for m in range(ATT_Q_HEADS // 2):
        j = (2 * m) // ATT_GROUP
        logits = lg_ref[m]
        sink = jnp.where(row_lo, sink_ref[2 * m], sink_ref[2 * m + 1])
        mx = jnp.maximum(jnp.max(logits, axis=-1, keepdims=True), sink)
        p = jnp.exp2(logits - mx)
        den = jnp.sum(p, axis=-1, keepdims=True) + jnp.exp2(sink - mx)
        pb = p.astype(BF16)
        rden = 1.0 / den
        o = (jnp.dot(pb[:BLOCK], v_lo[j], preferred_element_type=F32)
             + jnp.dot(pb[BLOCK:], v_hi[j], preferred_element_type=F32))
        o = o * jnp.where(lo_q, rden[:BLOCK], rden[BLOCK:])
        o_ref[0, :, m * LANES:(m + 1) * LANES] = o.astype(o_ref.dtype)


def _attn(aq, ak, av, sink2, bias):
    b, s, _ = aq.shape
    nb = s // BLOCK
    assert nb >= 2

    def kv(shift):
        return pl.BlockSpec(
            (1, BLOCK, KV_WIDTH),
            lambda i, n: (i, jnp.clip(n + shift, 0, nb - 1), 0))

    return pl.pallas_call(
        functools.partial(_attn_kernel, nb=nb),
        grid=(b, nb),
        in_specs=[pl.BlockSpec(memory_space=pltpu.SMEM),
                  pl.BlockSpec((1, BLOCK, ATT_WIDTH), lambda i, n: (i, n, 0)),
                  kv(-1), kv(0), kv(1), kv(-1), kv(0), kv(1),
                  pl.BlockSpec(bias.shape, lambda i, n: (0, 0, 0, 0))],
        out_specs=pl.BlockSpec((1, BLOCK, ATT_WIDTH), lambda i, n: (i, n, 0)),
        out_shape=jax.ShapeDtypeStruct((b, s, ATT_WIDTH), BF16),
        scratch_shapes=[pltpu.VMEM((ATT_Q_HEADS // 2, 2 * BLOCK, 3 * BLOCK), F32)],
        compiler_params=_cparams(("arbitrary", "arbitrary")),
        name="attn",
    )(sink2, aq, ak, ak, ak, av, av, av, bias)


def _mixin_weight(w):
    hw = HG_WIDTH
    cols = [w[:, 0:hw], w[:, 3 * hw:4 * hw], w[:, 4 * hw:5 * hw], w[:, 5 * hw:],
            w[:, hw:3 * hw]]
    return jnp.concatenate(cols, axis=1).astype(BF16)


def kernel(x, c, w_ada, b_ada, norm_g, w_ffn1_in, w_ffn1_out, w_ffn2_in, w_ffn2_out,
           w_mix_in, w_mix_out, hgrn_lb, hgrn_norm_g, qk_norm_g, attn_sink, rel_bias):
    b, s, d = x.shape
    depth = w_ada.shape[0]
    bias = _bias_tile(rel_bias)
    c_pad = jnp.pad(c.astype(F32), ((0, SUBLANES - b), (0, 0)))
    for l in range(depth):
        mods = _ada(c_pad, w_ada[l], b_ada[l][None, :])[:b]
        sh1, sc1, g1, sh2, sc2, g2, sh3, sc3, g3 = [
            mods[:, i * d:(i + 1) * d][:, None, :] for i in range(N_MOD)]
        x = _ffn(x, sh1, sc1, g1, norm_g[l, 0][None, :],
                 w_ffn1_in[l].astype(BF16), w_ffn1_out[l].astype(BF16))
        qg = (jnp.tile(qk_norm_g[l, 0], 2) * (LOG2E / math.sqrt(ATT_HEAD_DIM)))[None, :]
        kg = jnp.tile(qk_norm_g[l, 1], 2)[None, :]
        hg, aq, ak, av = _mixin(x, sh2, sc2, norm_g[l, 1][None, :], _mixin_weight(w_mix_in[l]),
                                hgrn_lb, qg, kg, l)
        o_hg = _hgrn(hg, hgrn_norm_g[l][None, :])
        att = _attn(aq, ak, av, attn_sink[l] * LOG2E, bias)
        x = _ffn(x, sh3, sc3, g3, norm_g[l, 2][None, :],
                 w_ffn2_in[l].astype(BF16), w_ffn2_out[l].astype(BF16),
                 mix=(o_hg, att, g2, w_mix_out[l].astype(BF16)))
    return x
```

```python
import functools
import math

import numpy as np
import jax
import jax.numpy as jnp
from jax import lax
from jax.experimental import pallas as pl
from jax.experimental.pallas import tpu as pltpu

F32 = jnp.float32
BF16 = jnp.bfloat16

D_MODEL = 1024
HG_HEADS = 4
HG_DIM = 128
HG_WIDTH = HG_HEADS * HG_DIM
ATT_Q_HEADS = 8
ATT_KV_HEADS = 2
ATT_HEAD_DIM = 64
ATT_GROUP = ATT_Q_HEADS // ATT_KV_HEADS
ATT_WIDTH = ATT_Q_HEADS * ATT_HEAD_DIM
KV_WIDTH = ATT_KV_HEADS * ATT_HEAD_DIM
WINDOW = 128
BLOCK = 128
NUM_BUCKETS = 32
MAX_DISTANCE = 128
D_FF = 2816
N_MOD = 9
EPS = 1e-6

LANES = 128
SUBLANES = 8
PACKED_ROWS = 16
VMEM_LIMIT = 56 * 1024 * 1024

FFN_TM = 512
MIX_TM = 256
HG_CHUNK = 128
NEG_INF = float("-inf")
LOG2E = 1.0 / math.log(2.0)

Z_FF, Z_FB, Z_Q, Z_G, Z_AQ = (n * HG_WIDTH for n in range(5))
Z_AK = Z_AQ + ATT_WIDTH
Z_AV = Z_AK + KV_WIDTH
Z_I = Z_AV + KV_WIDTH
Z_END = Z_I + HG_WIDTH
(HK_Q, HK_I, HK_G, HK_KF, HK_KB, HK_GHF, HK_GMF, HK_GHB, HK_GMB) = range(9)
N_HK = 9


def _cparams(sem, flags=None):
    return pltpu.CompilerParams(dimension_semantics=sem, vmem_limit_bytes=VMEM_LIMIT, flags=flags)


def _sigmoid(x):
    return 1.0 / (1.0 + jnp.exp(-x))


def _ada_kernel(c_ref, w_ref, b_ref, o_ref):
    c = c_ref[...]
    ca = c * _sigmoid(c)
    o_ref[...] = jnp.dot(ca, w_ref[...], precision=lax.Precision.HIGHEST,
                         preferred_element_type=F32) + b_ref[...]


def _ada(c_pad, w, b):
    rows, d = c_pad.shape
    n = w.shape[1]
    tn = 1024
    return pl.pallas_call(
        _ada_kernel,
        grid=(n // tn,),
        in_specs=[pl.BlockSpec((rows, d), lambda j: (0, 0)),
                  pl.BlockSpec((d, tn), lambda j: (0, j)),
                  pl.BlockSpec((1, tn), lambda j: (0, j))],
        out_specs=pl.BlockSpec((rows, tn), lambda j: (0, j)),
        out_shape=jax.ShapeDtypeStruct((rows, n), F32),
        compiler_params=_cparams(("arbitrary",)),
        name="ada",
    )(c_pad, w, b)


def _t5_bucket_np(rel):
    nb = NUM_BUCKETS // 2
    max_exact = nb // 2
    ret = (rel > 0).astype(np.int32) * nb
    n = np.abs(rel)
    ratio = np.maximum(n, 1).astype(np.float32) / np.float32(max_exact)
    large = max_exact + (np.log(ratio) / np.float32(math.log(MAX_DISTANCE / max_exact))
                         * np.float32(nb - max_exact)).astype(np.int32)
    large = np.minimum(large, nb - 1)
    return ret + np.where(n < max_exact, n, large)


def _bias_kernel(rbt_ref, oh_ref, msk_ref, o_ref):
    b = jnp.dot(rbt_ref[...], oh_ref[...], precision=lax.Precision.HIGHEST,
                preferred_element_type=F32)
    o_ref[...] = (b * LOG2E)[None] + msk_ref[...]


def _bias_tile(rel_bias):
    kcol = np.arange(3 * BLOCK)[None, :]
    rel = (kcol - BLOCK) - np.arange(BLOCK)[:, None]
    bucket = _t5_bucket_np(rel).reshape(-1)
    onehot_t = (np.arange(NUM_BUCKETS)[:, None] == bucket[None, :]).astype(np.float32)
    window = np.abs(rel) <= WINDOW
    valid = np.stack([window & (kcol >= BLOCK), window, window & (kcol < 2 * BLOCK)])
    mask = np.where(valid, 0.0, NEG_INF).astype(np.float32).reshape(3, 1, -1)
    ncol = BLOCK * 3 * BLOCK
    tc = ncol // 4
    out = pl.pallas_call(
        _bias_kernel,
        grid=(ncol // tc,),
        in_specs=[pl.BlockSpec((ATT_Q_HEADS, NUM_BUCKETS), lambda i: (0, 0)),
                  pl.BlockSpec((NUM_BUCKETS, tc), lambda i: (0, i)),
                  pl.BlockSpec((3, 1, tc), lambda i: (0, 0, i))],
        out_specs=pl.BlockSpec((3, ATT_Q_HEADS, tc), lambda i: (0, 0, i)),
        out_shape=jax.ShapeDtypeStruct((3, ATT_Q_HEADS, ncol), F32),
        compiler_params=_cparams(("arbitrary",)),
        name="bias_tile",
    )(rel_bias.astype(F32).T, jnp.asarray(onehot_t), jnp.asarray(mask))
    return out.reshape(3, ATT_Q_HEADS // 2, 2 * BLOCK, 3 * BLOCK)


def _norm_mod(x, ng, sh, sc):
    ms = jnp.mean(x * x, axis=-1, keepdims=True)
    y = x * lax.rsqrt(ms + EPS) * ng
    return y * (1.0 + sc) + sh


def _ffn_kernel(*refs, mix):
    if mix:
        hg_ref, at_ref, gm_ref, wm_ref, *refs = refs
    x_ref, sh_ref, sc_ref, gt_ref, ng_ref, win_ref, wout_ref, o_ref = refs
    x = x_ref[0]
    if mix:
        mixed = (jnp.dot(hg_ref[0], wm_ref[:HG_WIDTH, :], preferred_element_type=F32)
                 + jnp.dot(at_ref[0], wm_ref[HG_WIDTH:, :], preferred_element_type=F32))
        x = x + gm_ref[0] * mixed
    h = _norm_mod(x, ng_ref[...], sh_ref[0], sc_ref[0]).astype(BF16)
    gu = jnp.dot(h, win_ref[...], preferred_element_type=F32)
    g = gu[:, :D_FF]
    u = gu[:, D_FF:]
    act = (g * _sigmoid(g) * u).astype(BF16)
    y = jnp.dot(act, wout_ref[...], preferred_element_type=F32)
    o_ref[0] = x + 0.5 * gt_ref[0] * y


def _ffn(x, sh, sc, gt, ng, w_in, w_out, mix=None):
    b, s, d = x.shape
    tm = FFN_TM
    vec = pl.BlockSpec((1, 1, d), lambda i, j: (i, 0, 0))
    whole = lambda a: pl.BlockSpec(a.shape, lambda i, j: (0, 0), pipeline_mode=pl.Buffered(1))
    args = [x, sh, sc, gt, ng, w_in, w_out]
    specs = [pl.BlockSpec((1, tm, d), lambda i, j: (i, j, 0)), vec, vec, vec,
             whole(ng), whole(w_in), whole(w_out)]
    if mix is not None:
        o_hg, att, gm, wm = mix
        half = pl.BlockSpec((1, tm, HG_WIDTH), lambda i, j: (i, j, 0))
        args = [o_hg, att, gm, wm] + args
        specs = [half, half, vec, whole(wm)] + specs
    return pl.pallas_call(
        functools.partial(_ffn_kernel, mix=mix is not None),
        grid=(b, s // tm),
        in_specs=specs,
        out_specs=pl.BlockSpec((1, tm, d), lambda i, j: (i, j, 0)),
        out_shape=jax.ShapeDtypeStruct(x.shape, F32),
        compiler_params=_cparams(("arbitrary", "arbitrary")),
        name="ffn_mix" if mix is not None else "ffn",
    )(*args)


def _half_norm(x, gain, lo):
    x2 = x * x
    s_lo = jnp.sum(jnp.where(lo, x2, 0.0), axis=-1, keepdims=True)
    s_hi = jnp.sum(jnp.where(lo, 0.0, x2), axis=-1, keepdims=True)
    ms = jnp.where(lo, s_lo, s_hi) * (1.0 / ATT_HEAD_DIM)
    return x * lax.rsqrt(ms + EPS) * gain


def _mixin_kernel(x_ref, sh_ref, sc_ref, ng_ref, w_ref, lb_ref, qg_ref, kg_ref,
                  hg_ref, aq_ref, ak_ref, av_ref, *, layer):
    h = _norm_mod(x_ref[0], ng_ref[...], sh_ref[0], sc_ref[0]).astype(BF16)

    def proj(c0, c1):
        return jnp.dot(h, w_ref[:, c0:c1], preferred_element_type=F32)

    def put_heads(kind, val):
        for hh in range(HG_HEADS):
            hg_ref[0, kind * HG_HEADS + hh] = val[:, hh * HG_DIM:(hh + 1) * HG_DIM].astype(BF16)

    raw = lb_ref[...]
    mx = jnp.max(raw, axis=1, keepdims=True)
    ex = jnp.exp(raw - mx)
    lb = jnp.sum(ex[:, :layer + 1, :], axis=1) / jnp.sum(ex, axis=1)

    for d, (z0, kk, kh, km) in enumerate(((Z_FF, HK_KF, HK_GHF, HK_GMF),
                                          (Z_FB, HK_KB, HK_GHB, HK_GMB))):
        fr = proj(z0, z0 + HG_WIDTH)
        lbd = lb[d:d + 1]
        f = lbd + (1.0 - lbd) * _sigmoid(fr)
        g2 = jnp.log(f) * LOG2E
        hi = g2.astype(BF16)
        put_heads(kk, 1.0 - f)
        put_heads(kh, hi)
        put_heads(km, g2 - hi.astype(F32))

    q = proj(Z_Q, Z_G)
    put_heads(HK_Q, q * _sigmoid(q))
    g = proj(Z_G, Z_AQ)
    put_heads(HK_G, g * _sigmoid(g))

    lo = lax.broadcasted_iota(jnp.int32, (h.shape[0], LANES), 1) < LANES // 2
    aq = proj(Z_AQ, Z_AK)
    for m in range(ATT_WIDTH // LANES):
        cols = slice(m * LANES, (m + 1) * LANES)
        aq_ref[0, :, cols] = _half_norm(aq[:, cols], qg_ref[...], lo).astype(BF16)
    akv = proj(Z_AK, Z_I)
    ak_ref[0] = _half_norm(akv[:, :KV_WIDTH], kg_ref[...], lo).astype(BF16)
    av_ref[0] = akv[:, KV_WIDTH:].astype(BF16)
    put_heads(HK_I, proj(Z_I, Z_END))


def _mixin(x, sh, sc, ng, w, hgrn_lb, qg, kg, layer):
    b, s, d = x.shape
    tm = MIX_TM
    vec = pl.BlockSpec((1, 1, d), lambda i, j: (i, 0, 0))
    row = lambda n: pl.BlockSpec((1, n), lambda i, j: (0, 0))
    rows3 = lambda n: pl.BlockSpec((1, tm, n), lambda i, j: (i, j, 0))
    n_hg = N_HK * HG_HEADS
    return pl.pallas_call(
        functools.partial(_mixin_kernel, layer=layer),
        grid=(b, s // tm),
        in_specs=[pl.BlockSpec((1, tm, d), lambda i, j: (i, j, 0)),
                  vec, vec, row(d),
                  pl.BlockSpec(w.shape, lambda i, j: (0, 0)),
                  pl.BlockSpec(hgrn_lb.shape, lambda i, j: (0, 0, 0)),
                  row(LANES), row(LANES)],
        out_specs=[pl.BlockSpec((1, n_hg, tm, HG_DIM), lambda i, j: (i, 0, j, 0)),
                   rows3(ATT_WIDTH), rows3(KV_WIDTH), rows3(KV_WIDTH)],
        out_shape=[jax.ShapeDtypeStruct((b, n_hg, s, HG_DIM), BF16),
                   jax.ShapeDtypeStruct((b, s, ATT_WIDTH), BF16),
                   jax.ShapeDtypeStruct((b, s, KV_WIDTH), BF16),
                   jax.ShapeDtypeStruct((b, s, KV_WIDTH), BF16)],
        compiler_params=_cparams(("arbitrary", "arbitrary")),
        name="mix_in",
    )(x, sh, sc, ng, w, hgrn_lb, qg, kg)


SMALL_W = (1, 2, 4, 8)
GROUP = PACKED_ROWS
FINISH_CHUNKS = 8
PAIR_ORDER = (("a", 0), ("t", 0), ("b", 0), ("a", 1), ("t", 1), ("b", 1))


def _hgrn_consts(c):
    t = np.arange(c)[:, None]
    s = np.arange(c)[None, :]
    x = t ^ s
    lev = np.where(x > 0, np.floor(np.log2(np.maximum(x, 1))), -1).astype(np.int32)
    lev_f = np.where(t > s, lev, np.where(t == s, -1, -2)).astype(np.int32)
    lev_b = lev_f.T.copy()

    def exponent_rows(w, rev):
        m = np.zeros((GROUP, GROUP), np.float32)
        for r in range(GROUP):
            b0 = (r // (2 * w)) * 2 * w
            if not rev:
                ref = b0 + w - 1
                lo_u, hi_u = (ref + 1, r) if r > ref else (r + 1, ref)
            else:
                ref = b0 + w
                lo_u, hi_u = (r, ref - 1) if r < ref else (ref, r - 1)
            m[r, lo_u:hi_u + 1] = 1.0
        return m

    cums, smalls = [], []
    for rev in (False, True):
        cum = ((s >= t) if rev else (s <= t)).astype(np.float32)
        cums.append(np.concatenate([cum, cum], axis=1))
        rows = np.concatenate([exponent_rows(w, rev) for w in SMALL_W])
        smalls.append(np.concatenate([rows, rows], axis=1))
    return (jnp.asarray(np.stack([lev_f, lev_b])),
            jnp.asarray(np.stack(cums), dtype=BF16),
            jnp.asarray(np.stack(smalls), dtype=BF16))


def _neg_abs(x):
    bits = lax.bitcast_convert_type(x, jnp.uint32) | jnp.uint32(0x80000000)
    return lax.bitcast_convert_type(bits, F32)


def _ref_rows(g_cum, w, rev, c):
    idx = w if rev else w - 1
    g3 = g_cum.reshape(c // (2 * w), 2 * w, LANES)
    return jnp.broadcast_to(g3[:, idx:idx + 1, :], g3.shape).reshape(c, LANES)


def _hgrn_stage_a(tcum, tsmall, gh, gm, gc_ref, ge_ref, c):
    gc_ref[...] = jnp.dot(tcum, jnp.concatenate([gh, gm], axis=0), preferred_element_type=F32)
    groups = [slice(g * GROUP, (g + 1) * GROUP) for g in range(c // GROUP)]
    wide = jnp.concatenate([jnp.concatenate([gh[g] for g in groups], axis=1),
                            jnp.concatenate([gm[g] for g in groups], axis=1)], axis=0)
    ge_ref[...] = jnp.dot(tsmall, wide, preferred_element_type=F32)


def _hgrn_stage_b(q, k, v, gc_ref, ge_ref, lev, st_ref, a_ref, ab_ref, oi_ref, rev, c):
    nt = (((1,), (1,)), ((), ()))
    tn = (((0,), (0,)), ((), ()))
    g_cum = gc_ref[...]
    g_last = g_cum[0:1, :] if rev else g_cum[c - 1:c, :]
    qf, kf = q.astype(F32), k.astype(F32)

    st = st_ref[...]
    qi = (qf * jnp.exp2(g_cum)).astype(BF16)
    oi_ref[...] = lax.dot_general(qi, st.astype(BF16), nt, preferred_element_type=F32)
    kd = (kf * jnp.exp2(g_last - g_cum)).astype(BF16)
    u_t = lax.dot_general(v, kd, tn, preferred_element_type=F32)
    st_ref[...] = jnp.exp2(g_last) * st + u_t

    w = c // 2
    while w >= GROUP:
        x = jnp.exp2(_neg_abs(g_cum - _ref_rows(g_cum, w, rev, c)))
        q_rows, k_rows = [], []
        for b in range(c // (2 * w)):
            lo_half = slice(b * 2 * w, b * 2 * w + w)
            hi_half = slice(b * 2 * w + w, (b + 1) * 2 * w)
            qs = lo_half if rev else hi_half
            q_rows.append(qf[qs] * x[qs])
            k_rows.extend([kf[lo_half], kf[hi_half] * x[hi_half]] if rev
                          else [kf[lo_half] * x[lo_half], kf[hi_half]])
        p = lax.dot_general(jnp.concatenate(q_rows, axis=0).astype(BF16),
                            jnp.concatenate(k_rows, axis=0).astype(BF16), nt,
                            preferred_element_type=F32)
        for b in range(c // (2 * w)):
            lo_half = slice(b * 2 * w, b * 2 * w + w)
            hi_half = slice(b * 2 * w + w, (b + 1) * 2 * w)
            qs, ks = (lo_half, hi_half) if rev else (hi_half, lo_half)
            a_ref[qs, ks] = p[b * w:(b + 1) * w, ks]
        w //= 2

    a = jnp.where(lev == -1, lax.dot_general(q, k, nt, preferred_element_type=F32), 0.0)
    for li in range(len(SMALL_W)):
        e = jnp.concatenate([ge_ref[li * GROUP:(li + 1) * GROUP, g * LANES:(g + 1) * LANES]
                             for g in range(c // GROUP)], axis=0)
        x = jnp.exp2(e)
        p = lax.dot_general((qf * x).astype(BF16), (kf * x).astype(BF16), nt,
                            preferred_element_type=F32)
        a = jnp.where(lev == li, p, a)
    for b in range(c // GROUP):
        blk = slice(b * GROUP, (b + 1) * GROUP)
        a_ref[blk, blk] = a[blk, blk]
    ab_ref[...] = a_ref[...].astype(BF16)


def _hgrn_kernel(q_ref, i_ref, sg_ref, kf_ref, kb_ref, ghf_ref, gmf_ref, ghb_ref, gmb_ref,
                 ng_ref, lev_ref, tc_ref, ts_ref, o_ref,
                 acc_ref, st_ref, a_ref, ab_ref, oi_ref, gc_ref, ge_ref, *, c, nc):
    for ref in (acc_ref, st_ref, a_ref, ab_ref, oi_ref):
        ref[...] = jnp.zeros_like(ref)
    k_refs = (kf_ref, kb_ref)
    g_refs = ((ghf_ref, gmf_ref), (ghb_ref, gmb_ref))

    def rows_of(p, d):
        j = p if d == 0 else nc - 1 - p
        return pl.ds(pl.multiple_of(j * c, c), c)

    def stage_a(p, slot, d):
        rows = rows_of(p, d)
        _hgrn_stage_a(tc_ref[d], ts_ref[d], g_refs[d][0][0, 0, rows, :], g_refs[d][1][0, 0, rows, :],
                      gc_ref.at[slot, d], ge_ref.at[slot, d], c)

    def stage_t(p, d):
        rows = rows_of(p, d)
        acc_ref[rows, :] += oi_ref[d] + jnp.dot(ab_ref[d], i_ref[0, 0, rows, :],
                                                preferred_element_type=F32)

    def stage_b(p, slot, d):
        rows = rows_of(p, d)
        _hgrn_stage_b(q_ref[0, 0, rows, :], k_refs[d][0, 0, rows, :], i_ref[0, 0, rows, :],
                      gc_ref.at[slot, d], ge_ref.at[slot, d], lev_ref[d], st_ref.at[d],
                      a_ref.at[d], ab_ref.at[d], oi_ref.at[d], d == 1, c)

    def pair(p, slot):
        p_next, p_prev = jnp.minimum(p + 1, nc - 1), jnp.maximum(p - 1, 0)
        for stage, d in PAIR_ORDER:
            if stage == "a":
                stage_a(p_next, 1 - slot, d)
            elif stage == "t":
                stage_t(p_prev, d)
            else:
                stage_b(p, slot, d)

    def two_pairs(i2, carry):
        pair(2 * i2, 0)
        pair(2 * i2 + 1, 1)
        return carry

    for d in range(2):
        stage_a(0, 0, d)
    lax.fori_loop(0, nc // 2, two_pairs, 0)
    for d in range(2):
        stage_t(nc - 1, d)

    ng = ng_ref[...]
    rows_fin = FINISH_CHUNKS * c

    def finish(j, carry):
        rows = pl.ds(pl.multiple_of(j * rows_fin, rows_fin), rows_fin)
        o = acc_ref[rows, :]
        ms = jnp.mean(o * o, axis=-1, keepdims=True)
        o = o * lax.rsqrt(ms + EPS) * ng
        o_ref[0, rows, :] = (o * sg_ref[0, 0, rows, :].astype(F32)).astype(o_ref.dtype)
        return carry

    lax.fori_loop(0, nc // FINISH_CHUNKS, finish, 0)


def _hgrn(hg, norm_g):
    b, _, s, _ = hg.shape
    c = HG_CHUNK
    nc = s // c
    assert nc % 2 == 0 and c % (2 * GROUP) == 0
    lev, tcum, tsmall = _hgrn_consts(c)
    n_small = len(SMALL_W) * GROUP

    def kind(k):
        return pl.BlockSpec((1, 1, s, HG_DIM), lambda i, h: (i, k * HG_HEADS + h, 0, 0))

    const3 = lambda a: pl.BlockSpec(a.shape, lambda i, h: (0, 0, 0))
    return pl.pallas_call(
        functools.partial(_hgrn_kernel, c=c, nc=nc),
        grid=(b, HG_HEADS),
        in_specs=[kind(HK_Q), kind(HK_I), kind(HK_G), kind(HK_KF), kind(HK_KB),
                  kind(HK_GHF), kind(HK_GMF), kind(HK_GHB), kind(HK_GMB),
                  pl.BlockSpec((1, LANES), lambda i, h: (0, h)),
                  const3(lev), const3(tcum), const3(tsmall)],
        out_specs=pl.BlockSpec((1, s, LANES), lambda i, h: (i, 0, h)),
        out_shape=jax.ShapeDtypeStruct((b, s, HG_WIDTH), BF16),
        scratch_shapes=[pltpu.VMEM((s, HG_DIM), F32),
                        pltpu.VMEM((2, HG_DIM, HG_DIM), F32),
                        pltpu.VMEM((2, c, c), F32),
                        pltpu.VMEM((2, c, c), BF16),
                        pltpu.VMEM((2, c, HG_DIM), F32),
                        pltpu.VMEM((2, 2, c, LANES), F32),
                        pltpu.VMEM((2, 2, n_small, (c // GROUP) * LANES), F32)],
        compiler_params=_cparams(("arbitrary", "arbitrary")),
        name="hgrn",
    )(hg, hg, hg, hg, hg, hg, hg, hg, hg, norm_g, lev, tcum, tsmall)


def _attn_kernel(sink_ref, q_ref, kp_ref, ko_ref, kn_ref, vp_ref, vo_ref, vn_ref,
                 bias_ref, o_ref, lg_ref, *, nb):
    n = pl.program_id(1)
    half = LANES // 2
    kb = jnp.concatenate([kp_ref[0], ko_ref[0], kn_ref[0]], axis=0)
    vb = jnp.concatenate([vp_ref[0], vo_ref[0], vn_ref[0]], axis=0)
    lo_k = lax.broadcasted_iota(jnp.int32, kb.shape, 1) < half
    lo_q = lax.broadcasted_iota(jnp.int32, (BLOCK, LANES), 1) < half
    zero = jnp.zeros_like(kb)

    k1 = (jnp.where(lo_k, kb, zero), jnp.where(lo_k, zero, kb))
    v0lo, v1hi = jnp.where(lo_k, vb, zero), jnp.where(lo_k, zero, vb)
    v_lo = (v0lo, pltpu.roll(v1hi, half, axis=1))
    v_hi = (pltpu.roll(v0lo, half, axis=1), v1hi)

    edge = jnp.where(n == 0, 0, jnp.where(n == nb - 1, 2, 1))
    row_lo = lax.broadcasted_iota(jnp.int32, (2 * BLOCK, 1), 0) < BLOCK
    nt = (((1,), (1,)), ((), ()))

    for m in range(ATT_Q_HEADS // 2):
        j = (2 * m) // ATT_GROUP
        qp = q_ref[0, :, m * LANES:(m + 1) * LANES]
        qr = pltpu.roll(qp, half, axis=1)
        q2 = jnp.concatenate([qp, qr] if j == 0 else [qr, qp], axis=0)
        lg_ref[m] = lax.dot_general(q2, k1[j], nt, preferred_element_type=F32) + bias_ref[edge, m]

    for m in range(ATT_Q_HEADS // 2):
        j = (2 * m) // ATT_GROUP
        logits = lg_ref[m]
        sink = jnp.where(row_lo, sink_ref[2 * m], sink_ref[2 * m + 1])
        mx = jnp.maximum(jnp.max(logits, axis=-1, keepdims=True), sink)
        p = jnp.exp2(logits - mx)
        den = jnp.sum(p, axis=-1, keepdims=True) + jnp.exp2(sink - mx)
        pb = p.astype(BF16)
        rden = 1.0 / den
        o = (jnp.dot(pb[:BLOCK], v_lo[j], preferred_element_type=F32)
             + jnp.dot(pb[BLOCK:], v_hi[j], preferred_element_type=F32))
        o = o * jnp.where(lo_q, rden[:BLOCK], rden[BLOCK:])
        o_ref[0, :, m * LANES:(m + 1) * LANES] = o.astype(o_ref.dtype)


def _attn(aq, ak, av, sink2, bias):
    b, s, _ = aq.shape
    nb = s // BLOCK
    assert nb >= 2

    def kv(shift):
        return pl.BlockSpec(
            (1, BLOCK, KV_WIDTH),
            lambda i, n: (i, jnp.clip(n + shift, 0, nb - 1), 0))

    return pl.pallas_call(
        functools.partial(_attn_kernel, nb=nb),
        grid=(b, nb),
        in_specs=[pl.BlockSpec(memory_space=pltpu.SMEM),
                  pl.BlockSpec((1, BLOCK, ATT_WIDTH), lambda i, n: (i, n, 0)),
                  kv(-1), kv(0), kv(1), kv(-1), kv(0), kv(1),
                  pl.BlockSpec(bias.shape, lambda i, n: (0, 0, 0, 0))],
        out_specs=pl.BlockSpec((1, BLOCK, ATT_WIDTH), lambda i, n: (i, n, 0)),
        out_shape=jax.ShapeDtypeStruct((b, s, ATT_WIDTH), BF16),
        scratch_shapes=[pltpu.VMEM((ATT_Q_HEADS // 2, 2 * BLOCK, 3 * BLOCK), F32)],
        compiler_params=_cparams(("arbitrary", "arbitrary")),
        name="attn",
    )(sink2, aq, ak, ak, ak, av, av, av, bias)


def _mixin_weight(w):
    hw = HG_WIDTH
    cols = [w[:, hw:3 * hw], w[:, 0:hw], w[:, 4 * hw:5 * hw], w[:, 5 * hw:],
            w[:, 3 * hw:4 * hw]]
    return jnp.concatenate(cols, axis=1).astype(BF16)


def kernel(x, c, w_ada, b_ada, norm_g, w_ffn1_in, w_ffn1_out, w_ffn2_in, w_ffn2_out,
           w_mix_in, w_mix_out, hgrn_lb, hgrn_norm_g, qk_norm_g, attn_sink, rel_bias):
    b, s, d = x.shape
    depth = w_ada.shape[0]
    bias = _bias_tile(rel_bias)
    c_pad = jnp.pad(c.astype(F32), ((0, SUBLANES - b), (0, 0)))
    for l in range(depth):
        mods = _ada(c_pad, w_ada[l], b_ada[l][None, :])[:b]
        sh1, sc1, g1, sh2, sc2, g2, sh3, sc3, g3 = [
            mods[:, i * d:(i + 1) * d][:, None, :] for i in range(N_MOD)]
        x = _ffn(x, sh1, sc1, g1, norm_g[l, 0][None, :],
                 w_ffn1_in[l].astype(BF16), w_ffn1_out[l].astype(BF16))
        qg = (jnp.tile(qk_norm_g[l, 0], 2) * (LOG2E / math.sqrt(ATT_HEAD_DIM)))[None, :]
        kg = jnp.tile(qk_norm_g[l, 1], 2)[None, :]
        hg, aq, ak, av = _mixin(x, sh2, sc2, norm_g[l, 1][None, :], _mixin_weight(w_mix_in[l]),
                                hgrn_lb, qg, kg, l)
        o_hg = _hgrn(hg, hgrn_norm_g[l][None, :])
        att = _attn(aq, ak, av, attn_sink[l] * LOG2E, bias)
        x = _ffn(x, sh3, sc3, g3, norm_g[l, 2][None, :],
                 w_ffn2_in[l].astype(BF16), w_ffn2_out[l].astype(BF16),
                 mix=(o_hg, att, g2, w_mix_out[l].astype(BF16)))
    return x
```

```python
import functools
import math

import numpy as np
import jax
import jax.numpy as jnp
from jax import lax
from jax.experimental import pallas as pl
from jax.experimental.pallas import tpu as pltpu

F32 = jnp.float32
BF16 = jnp.bfloat16

D_MODEL = 1024
HG_HEADS = 4
HG_DIM = 128
HG_WIDTH = HG_HEADS * HG_DIM
ATT_Q_HEADS = 8
ATT_KV_HEADS = 2
ATT_HEAD_DIM = 64
ATT_GROUP = ATT_Q_HEADS // ATT_KV_HEADS
ATT_WIDTH = ATT_Q_HEADS * ATT_HEAD_DIM
KV_WIDTH = ATT_KV_HEADS * ATT_HEAD_DIM
WINDOW = 128
BLOCK = 128
NUM_BUCKETS = 32
MAX_DISTANCE = 128
D_FF = 2816
N_MOD = 9
EPS = 1e-6

LANES = 128
SUBLANES = 8
PACKED_ROWS = 16
MXU_COLS = 256
VMEM_LIMIT = 56 * 1024 * 1024

FFN_TM = 512
MIX_TM = 256
HG_CHUNK = 128
NEG_INF = float("-inf")
LOG2E = 1.0 / math.log(2.0)

Z_FF, Z_FB, Z_Q, Z_G, Z_AQ = (n * HG_WIDTH for n in range(5))
Z_AK = Z_AQ + ATT_WIDTH
Z_AV = Z_AK + KV_WIDTH
Z_I = Z_AV + KV_WIDTH
Z_END = Z_I + HG_WIDTH
(HK_Q, HK_I, HK_G, HK_KF, HK_KB, HK_GHF, HK_GMF, HK_GHB, HK_GMB) = range(9)
N_HK = 9


def _cparams(sem, flags=None):
    return pltpu.CompilerParams(dimension_semantics=sem, vmem_limit_bytes=VMEM_LIMIT, flags=flags)


def _sigmoid(x):
    return 1.0 / (1.0 + jnp.exp(-x))


def _ada_kernel(c_ref, w_ref, b_ref, o_ref):
    c = c_ref[...]
    ca = c * _sigmoid(c)
    o_ref[...] = jnp.dot(ca, w_ref[...], precision=lax.Precision.HIGHEST,
                         preferred_element_type=F32) + b_ref[...]


def _ada(c_pad, w, b):
    rows, d = c_pad.shape
    n = w.shape[1]
    tn = 1024
    return pl.pallas_call(
        _ada_kernel,
        grid=(n // tn,),
        in_specs=[pl.BlockSpec((rows, d), lambda j: (0, 0)),
                  pl.BlockSpec((d, tn), lambda j: (0, j)),
                  pl.BlockSpec((1, tn), lambda j: (0, j))],
        out_specs=pl.BlockSpec((rows, tn), lambda j: (0, j)),
        out_shape=jax.ShapeDtypeStruct((rows, n), F32),
        compiler_params=_cparams(("arbitrary",)),
        name="ada",
    )(c_pad, w, b)


def _t5_bucket_np(rel):
    nb = NUM_BUCKETS // 2
    max_exact = nb // 2
    ret = (rel > 0).astype(np.int32) * nb
    n = np.abs(rel)
    ratio = np.maximum(n, 1).astype(np.float32) / np.float32(max_exact)
    large = max_exact + (np.log(ratio) / np.float32(math.log(MAX_DISTANCE / max_exact))
                         * np.float32(nb - max_exact)).astype(np.int32)
    large = np.minimum(large, nb - 1)
    return ret + np.where(n < max_exact, n, large)


def _bias_kernel(rbt_ref, oh_ref, msk_ref, o_ref):
    b = jnp.dot(rbt_ref[...], oh_ref[...], precision=lax.Precision.HIGHEST,
                preferred_element_type=F32)
    o_ref[...] = (b * LOG2E)[None] + msk_ref[...]


def _bias_tile(rel_bias):
    kcol = np.arange(3 * BLOCK)[None, :]
    rel = (kcol - BLOCK) - np.arange(BLOCK)[:, None]
    bucket = _t5_bucket_np(rel).reshape(-1)
    onehot_t = (np.arange(NUM_BUCKETS)[:, None] == bucket[None, :]).astype(np.float32)
    window = np.abs(rel) <= WINDOW
    valid = np.stack([window & (kcol >= BLOCK), window, window & (kcol < 2 * BLOCK)])
    mask = np.where(valid, 0.0, NEG_INF).astype(np.float32).reshape(3, 1, -1)
    ncol = BLOCK * 3 * BLOCK
    tc = ncol // 4
    out = pl.pallas_call(
        _bias_kernel,
        grid=(ncol // tc,),
        in_specs=[pl.BlockSpec((ATT_Q_HEADS, NUM_BUCKETS), lambda i: (0, 0)),
                  pl.BlockSpec((NUM_BUCKETS, tc), lambda i: (0, i)),
                  pl.BlockSpec((3, 1, tc), lambda i: (0, 0, i))],
        out_specs=pl.BlockSpec((3, ATT_Q_HEADS, tc), lambda i: (0, 0, i)),
        out_shape=jax.ShapeDtypeStruct((3, ATT_Q_HEADS, ncol), F32),
        compiler_params=_cparams(("arbitrary",)),
        name="bias_tile",
    )(rel_bias.astype(F32).T, jnp.asarray(onehot_t), jnp.asarray(mask))
    return out.reshape(3, ATT_Q_HEADS // 2, 2 * BLOCK, 3 * BLOCK)


def _norm_mod(x, ng, sh, sc):
    ms = jnp.mean(x * x, axis=-1, keepdims=True)
    y = x * lax.rsqrt(ms + EPS) * ng
    return y * (1.0 + sc) + sh


def _ffn_kernel(*refs, mix):
    if mix:
        hg_ref, at_ref, gm_ref, wm_ref, *refs = refs
    x_ref, sh_ref, sc_ref, gt_ref, ng_ref, win_ref, wout_ref, o_ref = refs
    x = x_ref[0]
    if mix:
        mixed = (jnp.dot(hg_ref[0], wm_ref[:HG_WIDTH, :], preferred_element_type=F32)
                 + jnp.dot(at_ref[0], wm_ref[HG_WIDTH:, :], preferred_element_type=F32))
        x = x + gm_ref[0] * mixed
    h = _norm_mod(x, ng_ref[...], sh_ref[0], sc_ref[0]).astype(BF16)
    gu = jnp.dot(h, win_ref[...], preferred_element_type=F32)
    g = gu[:, :D_FF]
    u = gu[:, D_FF:]
    act = (g * _sigmoid(g) * u).astype(BF16)
    y = jnp.dot(act, wout_ref[...], preferred_element_type=F32)
    o_ref[0] = x + 0.5 * gt_ref[0] * y


def _ffn(x, sh, sc, gt, ng, w_in, w_out, mix=None):
    b, s, d = x.shape
    tm = FFN_TM
    vec = pl.BlockSpec((1, 1, d), lambda i, j: (i, 0, 0))
    whole = lambda a: pl.BlockSpec(a.shape, lambda i, j: (0, 0), pipeline_mode=pl.Buffered(1))
    args = [x, sh, sc, gt, ng, w_in, w_out]
    specs = [pl.BlockSpec((1, tm, d), lambda i, j: (i, j, 0)), vec, vec, vec,
             whole(ng), whole(w_in), whole(w_out)]
    if mix is not None:
        o_hg, att, gm, wm = mix
        half = pl.BlockSpec((1, tm, HG_WIDTH), lambda i, j: (i, j, 0))
        args = [o_hg, att, gm, wm] + args
        specs = [half, half, vec, whole(wm)] + specs
    return pl.pallas_call(
        functools.partial(_ffn_kernel, mix=mix is not None),
        grid=(b, s // tm),
        in_specs=specs,
        out_specs=pl.BlockSpec((1, tm, d), lambda i, j: (i, j, 0)),
        out_shape=jax.ShapeDtypeStruct(x.shape, F32),
        compiler_params=_cparams(("arbitrary", "arbitrary")),
        name="ffn_mix" if mix is not None else "ffn",
    )(*args)


def _half_norm(x, gain, lo):
    x2 = x * x
    s_lo = jnp.sum(jnp.where(lo, x2, 0.0), axis=-1, keepdims=True)
    s_hi = jnp.sum(jnp.where(lo, 0.0, x2), axis=-1, keepdims=True)
    ms = jnp.where(lo, s_lo, s_hi) * (1.0 / ATT_HEAD_DIM)
    return x * lax.rsqrt(ms + EPS) * gain


def _mixin_kernel(x_ref, sh_ref, sc_ref, ng_ref, w_ref, lb_ref, qg_ref, kg_ref,
                  hg_ref, aq_ref, ak_ref, av_ref, *, layer):
    h = _norm_mod(x_ref[0], ng_ref[...], sh_ref[0], sc_ref[0]).astype(BF16)

    def proj(c0, c1):
        return jnp.dot(h, w_ref[:, c0:c1], preferred_element_type=F32)

    def put_heads(kind, val):
        for hh in range(HG_HEADS):
            hg_ref[0, kind * HG_HEADS + hh] = val[:, hh * HG_DIM:(hh + 1) * HG_DIM].astype(BF16)

    raw = lb_ref[...]
    mx = jnp.max(raw, axis=1, keepdims=True)
    ex = jnp.exp(raw - mx)
    lb = jnp.sum(ex[:, :layer + 1, :], axis=1) / jnp.sum(ex, axis=1)

    for d, (z0, kk, kh, km) in enumerate(((Z_FF, HK_KF, HK_GHF, HK_GMF),
                                          (Z_FB, HK_KB, HK_GHB, HK_GMB))):
        fr = proj(z0, z0 + HG_WIDTH)
        lbd = lb[d:d + 1]
        f = lbd + (1.0 - lbd) * _sigmoid(fr)
        g2 = jnp.log(f) * LOG2E
        hi = g2.astype(BF16)
        put_heads(kk, 1.0 - f)
        put_heads(kh, hi)
        put_heads(km, g2 - hi.astype(F32))

    q = proj(Z_Q, Z_G)
    put_heads(HK_Q, q * _sigmoid(q))
    g = proj(Z_G, Z_AQ)
    put_heads(HK_G, g * _sigmoid(g))

    lo = lax.broadcasted_iota(jnp.int32, (h.shape[0], LANES), 1) < LANES // 2
    aq = proj(Z_AQ, Z_AK)
    for m in range(ATT_WIDTH // LANES):
        cols = slice(m * LANES, (m + 1) * LANES)
        aq_ref[0, :, cols] = _half_norm(aq[:, cols], qg_ref[...], lo).astype(BF16)
    akv = proj(Z_AK, Z_I)
    ak_ref[0] = _half_norm(akv[:, :KV_WIDTH], kg_ref[...], lo).T.astype(BF16)
    av_ref[0] = akv[:, KV_WIDTH:].astype(BF16)
    put_heads(HK_I, proj(Z_I, Z_END))


def _mixin(x, sh, sc, ng, w, hgrn_lb, qg, kg, layer):
    b, s, d = x.shape
    tm = MIX_TM
    vec = pl.BlockSpec((1, 1, d), lambda i, j: (i, 0, 0))
    row = lambda n: pl.BlockSpec((1, n), lambda i, j: (0, 0))
    rows3 = lambda n: pl.BlockSpec((1, tm, n), lambda i, j: (i, j, 0))
    n_hg = N_HK * HG_HEADS
    return pl.pallas_call(
        functools.partial(_mixin_kernel, layer=layer),
        grid=(b, s // tm),
        in_specs=[pl.BlockSpec((1, tm, d), lambda i, j: (i, j, 0)),
                  vec, vec, row(d),
                  pl.BlockSpec(w.shape, lambda i, j: (0, 0)),
                  pl.BlockSpec(hgrn_lb.shape, lambda i, j: (0, 0, 0)),
                  row(LANES), row(LANES)],
        out_specs=[pl.BlockSpec((1, n_hg, tm, HG_DIM), lambda i, j: (i, 0, j, 0)),
                   rows3(ATT_WIDTH),
                   pl.BlockSpec((1, KV_WIDTH, tm), lambda i, j: (i, 0, j)),
                   rows3(KV_WIDTH)],
        out_shape=[jax.ShapeDtypeStruct((b, n_hg, s, HG_DIM), BF16),
                   jax.ShapeDtypeStruct((b, s, ATT_WIDTH), BF16),
                   jax.ShapeDtypeStruct((b, KV_WIDTH, s), BF16),
                   jax.ShapeDtypeStruct((b, s, KV_WIDTH), BF16)],
        compiler_params=_cparams(("arbitrary", "arbitrary")),
        name="mix_in",
    )(x, sh, sc, ng, w, hgrn_lb, qg, kg)


SMALL_W = (1, 2, 4, 8)
GROUP = PACKED_ROWS
FINISH_CHUNKS = 8


def _hgrn_consts(c):
    t = np.arange(c)[:, None]
    s = np.arange(c)[None, :]
    x = t ^ s
    lev = np.where(x > 0, np.floor(np.log2(np.maximum(x, 1))), -1).astype(np.int32)
    lev_f = np.where(t > s, lev, np.where(t == s, -1, -2)).astype(np.int32)
    lev_b = lev_f.T.copy()

    def exponent_rows(w, rev):
        m = np.zeros((GROUP, GROUP), np.float32)
        for r in range(GROUP):
            b0 = (r // (2 * w)) * 2 * w
            if not rev:
                ref = b0 + w - 1
                lo_u, hi_u = (ref + 1, r) if r > ref else (r + 1, ref)
            else:
                ref = b0 + w
                lo_u, hi_u = (r, ref - 1) if r < ref else (ref, r - 1)
            m[r, lo_u:hi_u + 1] = 1.0
        return m

    cums, smalls = [], []
    for rev in (False, True):
        cum = ((s >= t) if rev else (s <= t)).astype(np.float32)
        cums.append(np.concatenate([cum, cum], axis=1))
        rows = np.concatenate([exponent_rows(w, rev) for w in SMALL_W])
        smalls.append(np.concatenate([rows, rows], axis=1))
    return (jnp.asarray(np.stack([lev_f, lev_b])),
            jnp.asarray(np.stack(cums), dtype=BF16),
            jnp.asarray(np.stack(smalls), dtype=BF16))


def _neg_abs(x):
    bits = lax.bitcast_convert_type(x, jnp.uint32) | jnp.uint32(0x80000000)
    return lax.bitcast_convert_type(bits, F32)


def _ref_rows(g_cum, w, rev, c):
    idx = w if rev else w - 1
    g3 = g_cum.reshape(c // (2 * w), 2 * w, LANES)
    return jnp.broadcast_to(g3[:, idx:idx + 1, :], g3.shape).reshape(c, LANES)


def _hgrn_stage_a(tcum, tsmall, gh, gm, gc_ref, ge_ref, c):
    hm = jnp.concatenate([gh, gm], axis=0)
    half = c // 2
    for r in range(2):
        rows = slice(r * half, (r + 1) * half)
        gc_ref[rows, :] = jnp.dot(tcum[rows], hm, preferred_element_type=F32)
        yield
    per_tile = MXU_COLS // LANES
    for t in range(c // GROUP // per_tile):
        groups = [slice(g * GROUP, (g + 1) * GROUP) for g in range(t * per_tile, (t + 1) * per_tile)]
        wide = jnp.concatenate([jnp.concatenate([gh[g] for g in groups], axis=1),
                                jnp.concatenate([gm[g] for g in groups], axis=1)], axis=0)
        ge_ref[:, t * MXU_COLS:(t + 1) * MXU_COLS] = jnp.dot(tsmall, wide,
                                                              preferred_element_type=F32)
        yield


def _hgrn_stage_b(q, k, v, gc_ref, ge_ref, lev, st_ref, a_ref, ab_ref, oi_ref, rev, c):
    tn = (((0,), (0,)), ((), ()))
    g_cum = gc_ref[...]
    g_last = g_cum[0:1, :] if rev else g_cum[c - 1:c, :]
    qf, kf = q.astype(F32), k.astype(F32)

    st = st_ref[...]
    qi = (qf * jnp.exp2(g_cum)).astype(BF16)
    oi_ref[...] = jnp.dot(qi, st.T.astype(BF16), preferred_element_type=F32)
    kd = (kf * jnp.exp2(g_last - g_cum)).astype(BF16)
    u_t = lax.dot_general(v, kd, tn, preferred_element_type=F32)
    st_ref[...] = jnp.exp2(g_last) * st + u_t
    yield

    w = c // 2
    while w >= GROUP:
        x = jnp.exp2(_neg_abs(g_cum - _ref_rows(g_cum, w, rev, c)))
        q_rows, k_rows = [], []
        for b in range(c // (2 * w)):
            lo_half = slice(b * 2 * w, b * 2 * w + w)
            hi_half = slice(b * 2 * w + w, (b + 1) * 2 * w)
            qs = lo_half if rev else hi_half
            q_rows.append(qf[qs] * x[qs])
            k_rows.extend([kf[lo_half], kf[hi_half] * x[hi_half]] if rev
                          else [kf[lo_half] * x[lo_half], kf[hi_half]])
        p = jnp.dot(jnp.concatenate(q_rows, axis=0).astype(BF16),
                    jnp.concatenate(k_rows, axis=0).T.astype(BF16),
                    preferred_element_type=F32)
        for b in range(c // (2 * w)):
            lo_half = slice(b * 2 * w, b * 2 * w + w)
            hi_half = slice(b * 2 * w + w, (b + 1) * 2 * w)
            qs, ks = (lo_half, hi_half) if rev else (hi_half, lo_half)
            a_ref[qs, ks] = p[b * w:(b + 1) * w, ks]
        w //= 2
        yield

    a = jnp.where(lev == -1, jnp.dot(q, kf.T.astype(BF16), preferred_element_type=F32), 0.0)
    for li in range(len(SMALL_W)):
        e = jnp.concatenate([ge_ref[li * GROUP:(li + 1) * GROUP, g * LANES:(g + 1) * LANES]
                             for g in range(c // GROUP)], axis=0)
        x = jnp.exp2(e)
        p = jnp.dot((qf * x).astype(BF16), (kf * x).T.astype(BF16),
                    preferred_element_type=F32)
        a = jnp.where(lev == li, p, a)
        yield
    for b in range(c // GROUP):
        blk = slice(b * GROUP, (b + 1) * GROUP)
        a_ref[blk, blk] = a[blk, blk]
    ab_ref[...] = a_ref[...].astype(BF16)
    yield


def _interleave(main, fill):
    for _ in main:
        next(fill, None)
    for _ in fill:
        pass


def _hgrn_kernel(q_ref, i_ref, sg_ref, kf_ref, kb_ref, ghf_ref, gmf_ref, ghb_ref, gmb_ref,
                 ng_ref, lev_ref, tc_ref, ts_ref, o_ref,
                 acc_ref, st_ref, a_ref, ab_ref, oi_ref, gc_ref, ge_ref, *, c, nc):
    for ref in (acc_ref, st_ref, a_ref, ab_ref, oi_ref):
        ref[...] = jnp.zeros_like(ref)
    k_refs = (kf_ref, kb_ref)
    g_refs = ((ghf_ref, gmf_ref), (ghb_ref, gmb_ref))

    def rows_of(p, d):
        j = p if d == 0 else nc - 1 - p
        return pl.ds(pl.multiple_of(j * c, c), c)

    def stage_a(p, slot, d):
        rows = rows_of(p, d)
        return _hgrn_stage_a(tc_ref[d], ts_ref[d], g_refs[d][0][0, 0, rows, :],
                             g_refs[d][1][0, 0, rows, :], gc_ref.at[slot, d], ge_ref.at[slot, d], c)

    def stage_t(p, d):
        rows = rows_of(p, d)
        acc_ref[rows, :] += oi_ref[d] + jnp.dot(ab_ref[d], i_ref[0, 0, rows, :],
                                                preferred_element_type=F32)

    def stage_b(p, slot, d):
        rows = rows_of(p, d)
        return _hgrn_stage_b(q_ref[0, 0, rows, :], k_refs[d][0, 0, rows, :], i_ref[0, 0, rows, :],
                             gc_ref.at[slot, d], ge_ref.at[slot, d], lev_ref[d], st_ref.at[d],
                             a_ref.at[d], ab_ref.at[d], oi_ref.at[d], d == 1, c)

    def pair(p, slot):
        p_next, p_prev = jnp.minimum(p + 1, nc - 1), jnp.maximum(p - 1, 0)
        for d in range(2):
            stage_t(p_prev, d)
            _interleave(stage_b(p, slot, d), stage_a(p_next, 1 - slot, d))

    def two_pairs(i2, carry):
        pair(2 * i2, 0)
        pair(2 * i2 + 1, 1)
        return carry

    for d in range(2):
        for _ in stage_a(0, 0, d):
            pass
    lax.fori_loop(0, nc // 2, two_pairs, 0)
    for d in range(2):
        stage_t(nc - 1, d)

    ng = ng_ref[...]
    rows_fin = FINISH_CHUNKS * c

    def finish(j, carry):
        rows = pl.ds(pl.multiple_of(j * rows_fin, rows_fin), rows_fin)
        o = acc_ref[rows, :]
        ms = jnp.mean(o * o, axis=-1, keepdims=True)
        o = o * lax.rsqrt(ms + EPS) * ng
        o_ref[0, rows, :] = (o * sg_ref[0, 0, rows, :].astype(F32)).astype(o_ref.dtype)
        return carry

    lax.fori_loop(0, nc // FINISH_CHUNKS, finish, 0)


def _hgrn(hg, norm_g):
    b, _, s, _ = hg.shape
    c = HG_CHUNK
    nc = s // c
    assert nc % 2 == 0 and c % (2 * GROUP) == 0
    lev, tcum, tsmall = _hgrn_consts(c)
    n_small = len(SMALL_W) * GROUP

    def kind(k):
        return pl.BlockSpec((1, 1, s, HG_DIM), lambda i, h: (i, k * HG_HEADS + h, 0, 0))

    const3 = lambda a: pl.BlockSpec(a.shape, lambda i, h: (0, 0, 0))
    return pl.pallas_call(
        functools.partial(_hgrn_kernel, c=c, nc=nc),
        grid=(b, HG_HEADS),
        in_specs=[kind(HK_Q), kind(HK_I), kind(HK_G), kind(HK_KF), kind(HK_KB),
                  kind(HK_GHF), kind(HK_GMF), kind(HK_GHB), kind(HK_GMB),
                  pl.BlockSpec((1, LANES), lambda i, h: (0, h)),
                  const3(lev), const3(tcum), const3(tsmall)],
        out_specs=pl.BlockSpec((1, s, LANES), lambda i, h: (i, 0, h)),
        out_shape=jax.ShapeDtypeStruct((b, s, HG_WIDTH), BF16),
        scratch_shapes=[pltpu.VMEM((s, HG_DIM), F32),
                        pltpu.VMEM((2, HG_DIM, HG_DIM), F32),
                        pltpu.VMEM((2, c, c), F32),
                        pltpu.VMEM((2, c, c), BF16),
                        pltpu.VMEM((2, c, HG_DIM), F32),
                        pltpu.VMEM((2, 2, c, LANES), F32),
                        pltpu.VMEM((2, 2, n_small, (c // GROUP) * LANES), F32)],
        compiler_params=_cparams(("arbitrary", "arbitrary")),
        name="hgrn",
    )(hg, hg, hg, hg, hg, hg, hg, hg, hg, norm_g, lev, tcum, tsmall)


def _attn_kernel(sink_ref, q_ref, kp_ref, ko_ref, kn_ref, vp_ref, vo_ref, vn_ref,
                 bias_ref, o_ref, lg_ref, *, nb):
    n = pl.program_id(1)
    half = LANES // 2
    kt = jnp.concatenate([kp_ref[0], ko_ref[0], kn_ref[0]], axis=1)
    vb = jnp.concatenate([vp_ref[0], vo_ref[0], vn_ref[0]], axis=0)
    top_k = lax.broadcasted_iota(jnp.int32, kt.shape, 0) < half
    lo_k = lax.broadcasted_iota(jnp.int32, vb.shape, 1) < half
    lo_q = lax.broadcasted_iota(jnp.int32, (BLOCK, LANES), 1) < half
    zero = jnp.zeros_like(vb)
    zero_t = jnp.zeros_like(kt)

    k1 = (jnp.where(top_k, kt, zero_t), jnp.where(top_k, zero_t, kt))
    v0lo, v1hi = jnp.where(lo_k, vb, zero), jnp.where(lo_k, zero, vb)
    v_lo = (v0lo, pltpu.roll(v1hi, half, axis=1))
    v_hi = (pltpu.roll(v0lo, half, axis=1), v1hi)

    edge = jnp.where(n == 0, 0, jnp.where(n == nb - 1, 2, 1))
    row_lo = lax.broadcasted_iota(jnp.int32, (2 * BLOCK, 1), 0) < BLOCK

    for m in range(ATT_Q_HEADS // 2):
        j = (2 * m) // ATT_GROUP
        qp = q_ref[0, :, m * LANES:(m + 1) * LANES]
        qr = pltpu.roll(qp, half, axis=1)
        q2 = jnp.concatenate([qp, qr] if j == 0 else [qr, qp], axis=0)
        lg_ref[m] = jnp.dot(q2, k1[j], preferred_element_type=F32) + bias_ref[edge, m]

    for m in range(ATT_Q_HEADS // 2):
        j = (2 * m) // ATT_GROUP
        logits = lg_ref[m]
        sink = jnp.where(row_lo, sink_ref[2 * m], sink_ref[2 * m + 1])
        mx = jnp.maximum(jnp.max(logits, axis=-1, keepdims=True), sink)
        p = jnp.exp2(logits - mx)
        den = jnp.sum(p, axis=-1, keepdims=True) + jnp.exp2(sink - mx)
        pb = p.astype(BF16)
        rden = 1.0 / den
        o = (jnp.dot(pb[:BLOCK], v_lo[j], preferred_element_type=F32)
             + jnp.dot(pb[BLOCK:], v_hi[j], preferred_element_type=F32))
        o = o * jnp.where(lo_q, rden[:BLOCK], rden[BLOCK:])
        o_ref[0, :, m * LANES:(m + 1) * LANES] = o.astype(o_ref.dtype)


def _attn(aq, ak, av, sink2, bias):
    b, s, _ = aq.shape
    nb = s // BLOCK
    assert nb >= 2

    def neighbour(n, shift):
        return jnp.clip(n + shift, 0, nb - 1)

    def kt(shift):
        return pl.BlockSpec((1, KV_WIDTH, BLOCK), lambda i, n: (i, 0, neighbour(n, shift)))

    def v(shift):
        return pl.BlockSpec((1, BLOCK, KV_WIDTH), lambda i, n: (i, neighbour(n, shift), 0))

    return pl.pallas_call(
        functools.partial(_attn_kernel, nb=nb),
        grid=(b, nb),
        in_specs=[pl.BlockSpec(memory_space=pltpu.SMEM),
                  pl.BlockSpec((1, BLOCK, ATT_WIDTH), lambda i, n: (i, n, 0)),
                  kt(-1), kt(0), kt(1), v(-1), v(0), v(1),
                  pl.BlockSpec(bias.shape, lambda i, n: (0, 0, 0, 0))],
        out_specs=pl.BlockSpec((1, BLOCK, ATT_WIDTH), lambda i, n: (i, n, 0)),
        out_shape=jax.ShapeDtypeStruct((b, s, ATT_WIDTH), BF16),
        scratch_shapes=[pltpu.VMEM((ATT_Q_HEADS // 2, 2 * BLOCK, 3 * BLOCK), F32)],
        compiler_params=_cparams(("arbitrary", "arbitrary")),
        name="attn",
    )(sink2, aq, ak, ak, ak, av, av, av, bias)


def _mixin_weight(w):
    hw = HG_WIDTH
    cols = [w[:, hw:3 * hw], w[:, 0:hw], w[:, 4 * hw:5 * hw], w[:, 5 * hw:],
            w[:, 3 * hw:4 * hw]]
    return jnp.concatenate(cols, axis=1).astype(BF16)


def kernel(x, c, w_ada, b_ada, norm_g, w_ffn1_in, w_ffn1_out, w_ffn2_in, w_ffn2_out,
           w_mix_in, w_mix_out, hgrn_lb, hgrn_norm_g, qk_norm_g, attn_sink, rel_bias):
    b, s, d = x.shape
    depth = w_ada.shape[0]
    bias = _bias_tile(rel_bias)
    c_pad = jnp.pad(c.astype(F32), ((0, SUBLANES - b), (0, 0)))
    for l in range(depth):
        mods = _ada(c_pad, w_ada[l], b_ada[l][None, :])[:b]
        sh1, sc1, g1, sh2, sc2, g2, sh3, sc3, g3 = [
            mods[:, i * d:(i + 1) * d][:, None, :] for i in range(N_MOD)]
        x = _ffn(x, sh1, sc1, g1, norm_g[l, 0][None, :],
                 w_ffn1_in[l].astype(BF16), w_ffn1_out[l].astype(BF16))
        qg = (jnp.tile(qk_norm_g[l, 0], 2) * (LOG2E / math.sqrt(ATT_HEAD_DIM)))[None, :]
        kg = jnp.tile(qk_norm_g[l, 1], 2)[None, :]
        hg, aq, ak, av = _mixin(x, sh2, sc2, norm_g[l, 1][None, :], _mixin_weight(w_mix_in[l]),
                                hgrn_lb, qg, kg, l)
        o_hg = _hgrn(hg, hgrn_norm_g[l][None, :])
        att = _attn(aq, ak, av, attn_sink[l] * LOG2E, bias)
        x = _ffn(x, sh3, sc3, g3, norm_g[l, 2][None, :],
                 w_ffn2_in[l].astype(BF16), w_ffn2_out[l].astype(BF16),
                 mix=(o_hg, att, g2, w_mix_out[l].astype(BF16)))
    return x
```

```python
import functools
import math

import numpy as np
import jax
import jax.numpy as jnp
from jax import lax
from jax.experimental import pallas as pl
from jax.experimental.pallas import tpu as pltpu

F32 = jnp.float32
BF16 = jnp.bfloat16

D_MODEL = 1024
HG_HEADS = 4
HG_DIM = 128
HG_WIDTH = HG_HEADS * HG_DIM
ATT_Q_HEADS = 8
ATT_KV_HEADS = 2
ATT_HEAD_DIM = 64
ATT_GROUP = ATT_Q_HEADS // ATT_KV_HEADS
ATT_WIDTH = ATT_Q_HEADS * ATT_HEAD_DIM
KV_WIDTH = ATT_KV_HEADS * ATT_HEAD_DIM
WINDOW = 128
BLOCK = 128
NUM_BUCKETS = 32
MAX_DISTANCE = 128
D_FF = 2816
N_MOD = 9
EPS = 1e-6

LANES = 128
SUBLANES = 8
PACKED_ROWS = 16
MXU_COLS = 256
VMEM_LIMIT = 56 * 1024 * 1024

FFN_TM = 512
MIX_TM = 256
HG_CHUNK = 128
NEG_INF = float("-inf")
LOG2E = 1.0 / math.log(2.0)

Z_FF, Z_FB, Z_Q, Z_G, Z_AQ = (n * HG_WIDTH for n in range(5))
Z_AK = Z_AQ + ATT_WIDTH
Z_AV = Z_AK + KV_WIDTH
Z_I = Z_AV + KV_WIDTH
Z_END = Z_I + HG_WIDTH
(HK_Q, HK_I, HK_G, HK_KF, HK_KB, HK_GHF, HK_GMF, HK_GHB, HK_GMB) = range(9)
N_HK = 9


def _cparams(sem, flags=None):
    return pltpu.CompilerParams(dimension_semantics=sem, vmem_limit_bytes=VMEM_LIMIT, flags=flags)


def _sigmoid(x):
    return 1.0 / (1.0 + jnp.exp(-x))


def _ada_kernel(ct_ref, w_ref, b_ref, o_ref):
    ct = ct_ref[...]
    ca = ct * _sigmoid(ct)
    w = w_ref[...]
    for b in range(ct.shape[1]):
        o_ref[b:b + 1, :] = jnp.sum(w * ca[:, b:b + 1], axis=0, keepdims=True) + b_ref[...]


def _ada(c, w, b):
    batch, d = c.shape
    n = w.shape[1]
    tn = 1024
    return pl.pallas_call(
        _ada_kernel,
        grid=(n // tn,),
        in_specs=[pl.BlockSpec((d, batch), lambda j: (0, 0)),
                  pl.BlockSpec((d, tn), lambda j: (0, j)),
                  pl.BlockSpec((1, tn), lambda j: (0, j))],
        out_specs=pl.BlockSpec((batch, tn), lambda j: (0, j)),
        out_shape=jax.ShapeDtypeStruct((batch, n), F32),
        compiler_params=_cparams(("arbitrary",)),
        name="ada",
    )(c.T, w, b)


def _t5_bucket_np(rel):
    nb = NUM_BUCKETS // 2
    max_exact = nb // 2
    ret = (rel > 0).astype(np.int32) * nb
    n = np.abs(rel)
    ratio = np.maximum(n, 1).astype(np.float32) / np.float32(max_exact)
    large = max_exact + (np.log(ratio) / np.float32(math.log(MAX_DISTANCE / max_exact))
                         * np.float32(nb - max_exact)).astype(np.int32)
    large = np.minimum(large, nb - 1)
    return ret + np.where(n < max_exact, n, large)


def _bias_kernel(rbt_ref, oh_ref, msk_ref, o_ref):
    b = jnp.dot(rbt_ref[...], oh_ref[...], precision=lax.Precision.HIGHEST,
                preferred_element_type=F32)
    o_ref[...] = (b * LOG2E)[None] + msk_ref[...]


def _bias_tile(rel_bias):
    kcol = np.arange(3 * BLOCK)[None, :]
    rel = (kcol - BLOCK) - np.arange(BLOCK)[:, None]
    bucket = _t5_bucket_np(rel).reshape(-1)
    onehot_t = (np.arange(NUM_BUCKETS)[:, None] == bucket[None, :]).astype(np.float32)
    window = np.abs(rel) <= WINDOW
    valid = np.stack([window & (kcol >= BLOCK), window, window & (kcol < 2 * BLOCK)])
    mask = np.where(valid, 0.0, NEG_INF).astype(np.float32).reshape(3, 1, -1)
    ncol = BLOCK * 3 * BLOCK
    tc = ncol // 4
    out = pl.pallas_call(
        _bias_kernel,
        grid=(ncol // tc,),
        in_specs=[pl.BlockSpec((ATT_Q_HEADS, NUM_BUCKETS), lambda i: (0, 0)),
                  pl.BlockSpec((NUM_BUCKETS, tc), lambda i: (0, i)),
                  pl.BlockSpec((3, 1, tc), lambda i: (0, 0, i))],
        out_specs=pl.BlockSpec((3, ATT_Q_HEADS, tc), lambda i: (0, 0, i)),
        out_shape=jax.ShapeDtypeStruct((3, ATT_Q_HEADS, ncol), F32),
        compiler_params=_cparams(("arbitrary",)),
        name="bias_tile",
    )(rel_bias.astype(F32).T, jnp.asarray(onehot_t), jnp.asarray(mask))
    return out.reshape(3, ATT_Q_HEADS // 2, 2 * BLOCK, 3 * BLOCK)


def _norm_mod(x, ng, sh, sc):
    ms = jnp.mean(x * x, axis=-1, keepdims=True)
    y = x * lax.rsqrt(ms + EPS) * ng
    return y * (1.0 + sc) + sh


def _ffn_kernel(*refs, mix):
    if mix:
        hg_ref, at_ref, gm_ref, wm_ref, *refs = refs
    x_ref, sh_ref, sc_ref, gt_ref, ng_ref, win_ref, wout_ref, o_ref = refs
    x = x_ref[0]
    if mix:
        mixed = (jnp.dot(hg_ref[0], wm_ref[:HG_WIDTH, :], preferred_element_type=F32)
                 + jnp.dot(at_ref[0], wm_ref[HG_WIDTH:, :], preferred_element_type=F32))
        x = x + gm_ref[0] * mixed
    h = _norm_mod(x, ng_ref[...], sh_ref[0], sc_ref[0]).astype(BF16)
    gu = jnp.dot(h, win_ref[...], preferred_element_type=F32)
    g = gu[:, :D_FF]
    u = gu[:, D_FF:]
    act = (g * _sigmoid(g) * u).astype(BF16)
    y = jnp.dot(act, wout_ref[...], preferred_element_type=F32)
    o_ref[0] = x + 0.5 * gt_ref[0] * y


def _ffn(x, sh, sc, gt, ng, w_in, w_out, mix=None):
    b, s, d = x.shape
    tm = FFN_TM
    vec = pl.BlockSpec((1, 1, d), lambda i, j: (i, 0, 0))
    whole = lambda a: pl.BlockSpec(a.shape, lambda i, j: (0, 0), pipeline_mode=pl.Buffered(1))
    args = [x, sh, sc, gt, ng, w_in, w_out]
    specs = [pl.BlockSpec((1, tm, d), lambda i, j: (i, j, 0)), vec, vec, vec,
             whole(ng), whole(w_in), whole(w_out)]
    if mix is not None:
        o_hg, att, gm, wm = mix
        half = pl.BlockSpec((1, tm, HG_WIDTH), lambda i, j: (i, j, 0))
        args = [o_hg, att, gm, wm] + args
        specs = [half, half, vec, whole(wm)] + specs
    return pl.pallas_call(
        functools.partial(_ffn_kernel, mix=mix is not None),
        grid=(b, s // tm),
        in_specs=specs,
        out_specs=pl.BlockSpec((1, tm, d), lambda i, j: (i, j, 0)),
        out_shape=jax.ShapeDtypeStruct(x.shape, F32),
        compiler_params=_cparams(("arbitrary", "arbitrary")),
        name="ffn_mix" if mix is not None else "ffn",
    )(*args)


def _half_norm(x, gain, lo):
    x2 = x * x
    s_lo = jnp.sum(jnp.where(lo, x2, 0.0), axis=-1, keepdims=True)
    s_hi = jnp.sum(jnp.where(lo, 0.0, x2), axis=-1, keepdims=True)
    ms = jnp.where(lo, s_lo, s_hi) * (1.0 / ATT_HEAD_DIM)
    return x * lax.rsqrt(ms + EPS) * gain


def _mixin_kernel(x_ref, sh_ref, sc_ref, ng_ref, w_ref, lb_ref, qg_ref, kg_ref,
                  hg_ref, aq_ref, ak_ref, av_ref, *, layer):
    h = _norm_mod(x_ref[0], ng_ref[...], sh_ref[0], sc_ref[0]).astype(BF16)

    def proj(c0, c1):
        return jnp.dot(h, w_ref[:, c0:c1], preferred_element_type=F32)

    def put_heads(kind, val):
        for hh in range(HG_HEADS):
            hg_ref[0, kind * HG_HEADS + hh] = val[:, hh * HG_DIM:(hh + 1) * HG_DIM].astype(BF16)

    raw = lb_ref[...]
    mx = jnp.max(raw, axis=1, keepdims=True)
    ex = jnp.exp(raw - mx)
    lb = jnp.sum(ex[:, :layer + 1, :], axis=1) / jnp.sum(ex, axis=1)

    for d, (z0, kk, kh, km) in enumerate(((Z_FF, HK_KF, HK_GHF, HK_GMF),
                                          (Z_FB, HK_KB, HK_GHB, HK_GMB))):
        fr = proj(z0, z0 + HG_WIDTH)
        lbd = lb[d:d + 1]
        f = lbd + (1.0 - lbd) * _sigmoid(fr)
        g2 = jnp.log(f) * LOG2E
        hi = g2.astype(BF16)
        put_heads(kk, 1.0 - f)
        put_heads(kh, hi)
        put_heads(km, g2 - hi.astype(F32))

    q = proj(Z_Q, Z_G)
    put_heads(HK_Q, q * _sigmoid(q))
    g = proj(Z_G, Z_AQ)
    put_heads(HK_G, g * _sigmoid(g))

    lo = lax.broadcasted_iota(jnp.int32, (h.shape[0], LANES), 1) < LANES // 2
    aq = proj(Z_AQ, Z_AK)
    for m in range(ATT_WIDTH // LANES):
        cols = slice(m * LANES, (m + 1) * LANES)
        aq_ref[0, :, cols] = _half_norm(aq[:, cols], qg_ref[...], lo).astype(BF16)
    akv = proj(Z_AK, Z_I)
    ak_ref[0] = _half_norm(akv[:, :KV_WIDTH], kg_ref[...], lo).T.astype(BF16)
    av_ref[0] = akv[:, KV_WIDTH:].astype(BF16)
    put_heads(HK_I, proj(Z_I, Z_END))


def _mixin(x, sh, sc, ng, w, hgrn_lb, qg, kg, layer):
    b, s, d = x.shape
    tm = MIX_TM
    vec = pl.BlockSpec((1, 1, d), lambda i, j: (i, 0, 0))
    row = lambda n: pl.BlockSpec((1, n), lambda i, j: (0, 0))
    rows3 = lambda n: pl.BlockSpec((1, tm, n), lambda i, j: (i, j, 0))
    n_hg = N_HK * HG_HEADS
    return pl.pallas_call(
        functools.partial(_mixin_kernel, layer=layer),
        grid=(b, s // tm),
        in_specs=[pl.BlockSpec((1, tm, d), lambda i, j: (i, j, 0)),
                  vec, vec, row(d),
                  pl.BlockSpec(w.shape, lambda i, j: (0, 0)),
                  pl.BlockSpec(hgrn_lb.shape, lambda i, j: (0, 0, 0)),
                  row(LANES), row(LANES)],
        out_specs=[pl.BlockSpec((1, n_hg, tm, HG_DIM), lambda i, j: (i, 0, j, 0)),
                   rows3(ATT_WIDTH),
                   pl.BlockSpec((1, KV_WIDTH, tm), lambda i, j: (i, 0, j)),
                   rows3(KV_WIDTH)],
        out_shape=[jax.ShapeDtypeStruct((b, n_hg, s, HG_DIM), BF16),
                   jax.ShapeDtypeStruct((b, s, ATT_WIDTH), BF16),
                   jax.ShapeDtypeStruct((b, KV_WIDTH, s), BF16),
                   jax.ShapeDtypeStruct((b, s, KV_WIDTH), BF16)],
        compiler_params=_cparams(("arbitrary", "arbitrary")),
        name="mix_in",
    )(x, sh, sc, ng, w, hgrn_lb, qg, kg)


SMALL_W = (1, 2, 4, 8)
GROUP = PACKED_ROWS
FINISH_CHUNKS = 8
PAIRS_PER_BODY = 8


def _hgrn_consts(c):
    t = np.arange(c)[:, None]
    s = np.arange(c)[None, :]
    x = t ^ s
    lev = np.where(x > 0, np.floor(np.log2(np.maximum(x, 1))), -1).astype(np.int32)
    lev_f = np.where(t > s, lev, np.where(t == s, -1, -2)).astype(np.int32)
    lev_b = lev_f.T.copy()

    def exponent_rows(w, rev):
        m = np.zeros((GROUP, GROUP), np.float32)
        for r in range(GROUP):
            b0 = (r // (2 * w)) * 2 * w
            if not rev:
                ref = b0 + w - 1
                lo_u, hi_u = (ref + 1, r) if r > ref else (r + 1, ref)
            else:
                ref = b0 + w
                lo_u, hi_u = (r, ref - 1) if r < ref else (ref, r - 1)
            m[r, lo_u:hi_u + 1] = 1.0
        return m

    cums, smalls = [], []
    for rev in (False, True):
        cum = ((s >= t) if rev else (s <= t)).astype(np.float32)
        cums.append(np.concatenate([cum, cum], axis=1))
        rows = np.concatenate([exponent_rows(w, rev) for w in SMALL_W])
        smalls.append(np.concatenate([rows, rows], axis=1))
    return (jnp.asarray(np.stack([lev_f, lev_b])),
            jnp.asarray(np.stack(cums), dtype=BF16),
            jnp.asarray(np.stack(smalls), dtype=BF16))


def _neg_abs(x):
    bits = lax.bitcast_convert_type(x, jnp.uint32) | jnp.uint32(0x80000000)
    return lax.bitcast_convert_type(bits, F32)


def _ref_rows(g_cum, w, rev, c):
    idx = w if rev else w - 1
    g3 = g_cum.reshape(c // (2 * w), 2 * w, LANES)
    return jnp.broadcast_to(g3[:, idx:idx + 1, :], g3.shape).reshape(c, LANES)


def _hgrn_stage_a(tcum, tsmall, gh, gm, gc_ref, ge_ref, c):
    hm = jnp.concatenate([gh, gm], axis=0)
    half = c // 2
    for r in range(2):
        rows = slice(r * half, (r + 1) * half)
        gc_ref[rows, :] = jnp.dot(tcum[rows], hm, preferred_element_type=F32)
        yield
    per_tile = MXU_COLS // LANES
    for t in range(c // GROUP // per_tile):
        groups = [slice(g * GROUP, (g + 1) * GROUP) for g in range(t * per_tile, (t + 1) * per_tile)]
        wide = jnp.concatenate([jnp.concatenate([gh[g] for g in groups], axis=1),
                                jnp.concatenate([gm[g] for g in groups], axis=1)], axis=0)
        ge_ref[:, t * MXU_COLS:(t + 1) * MXU_COLS] = jnp.dot(tsmall, wide,
                                                              preferred_element_type=F32)
        yield


def _hgrn_stage_b(q, k, v, gc_ref, ge_ref, lev, st_ref, a_ref, ab_ref, oi_ref, rev, c):
    tn = (((0,), (0,)), ((), ()))
    g_cum = gc_ref[...]
    g_last = g_cum[0:1, :] if rev else g_cum[c - 1:c, :]
    qf, kf = q.astype(F32), k.astype(F32)

    st = st_ref[...]
    qi = (qf * jnp.exp2(g_cum)).astype(BF16)
    oi_ref[...] = jnp.dot(qi, st.T.astype(BF16), preferred_element_type=F32)
    kd = (kf * jnp.exp2(g_last - g_cum)).astype(BF16)
    u_t = lax.dot_general(v, kd, tn, preferred_element_type=F32)
    st_ref[...] = jnp.exp2(g_last) * st + u_t
    yield

    w = c // 2
    while w >= GROUP:
        x = jnp.exp2(_neg_abs(g_cum - _ref_rows(g_cum, w, rev, c)))
        q_rows, k_rows = [], []
        for b in range(c // (2 * w)):
            lo_half = slice(b * 2 * w, b * 2 * w + w)
            hi_half = slice(b * 2 * w + w, (b + 1) * 2 * w)
            qs = lo_half if rev else hi_half
            q_rows.append(qf[qs] * x[qs])
            k_rows.extend([kf[lo_half], kf[hi_half] * x[hi_half]] if rev
                          else [kf[lo_half] * x[lo_half], kf[hi_half]])
        p = jnp.dot(jnp.concatenate(q_rows, axis=0).astype(BF16),
                    jnp.concatenate(k_rows, axis=0).T.astype(BF16),
                    preferred_element_type=F32)
        for b in range(c // (2 * w)):
            lo_half = slice(b * 2 * w, b * 2 * w + w)
            hi_half = slice(b * 2 * w + w, (b + 1) * 2 * w)
            qs, ks = (lo_half, hi_half) if rev else (hi_half, lo_half)
            a_ref[qs, ks] = p[b * w:(b + 1) * w, ks]
        w //= 2
        yield

    a = jnp.where(lev == -1, jnp.dot(q, kf.T.astype(BF16), preferred_element_type=F32), 0.0)
    for li in range(len(SMALL_W)):
        e = jnp.concatenate([ge_ref[li * GROUP:(li + 1) * GROUP, g * LANES:(g + 1) * LANES]
                             for g in range(c // GROUP)], axis=0)
        x = jnp.exp2(e)
        p = jnp.dot((qf * x).astype(BF16), (kf * x).T.astype(BF16),
                    preferred_element_type=F32)
        a = jnp.where(lev == li, p, a)
        yield
    for b in range(c // GROUP):
        blk = slice(b * GROUP, (b + 1) * GROUP)
        a_ref[blk, blk] = a[blk, blk]
    ab_ref[...] = a_ref[...].astype(BF16)
    yield


def _interleave(main, fill):
    for _ in main:
        next(fill, None)
    for _ in fill:
        pass


def _hgrn_kernel(q_ref, i_ref, sg_ref, kf_ref, kb_ref, ghf_ref, gmf_ref, ghb_ref, gmb_ref,
                 ng_ref, lev_ref, tc_ref, ts_ref, o_ref,
                 acc_ref, st_ref, a_ref, ab_ref, oi_ref, gc_ref, ge_ref, *, c, nc):
    for ref in (acc_ref, st_ref, a_ref, ab_ref, oi_ref):
        ref[...] = jnp.zeros_like(ref)
    k_refs = (kf_ref, kb_ref)
    g_refs = ((ghf_ref, gmf_ref), (ghb_ref, gmb_ref))

    def rows_of(p, d):
        j = p if d == 0 else nc - 1 - p
        return pl.ds(pl.multiple_of(j * c, c), c)

    def stage_a(p, slot, d):
        rows = rows_of(p, d)
        return _hgrn_stage_a(tc_ref[d], ts_ref[d], g_refs[d][0][0, 0, rows, :],
                             g_refs[d][1][0, 0, rows, :], gc_ref.at[slot, d], ge_ref.at[slot, d], c)

    def stage_t(p, d):
        rows = rows_of(p, d)
        acc_ref[rows, :] += oi_ref[d] + jnp.dot(ab_ref[d], i_ref[0, 0, rows, :],
                                                preferred_element_type=F32)

    def stage_b(p, slot, d):
        rows = rows_of(p, d)
        return _hgrn_stage_b(q_ref[0, 0, rows, :], k_refs[d][0, 0, rows, :], i_ref[0, 0, rows, :],
                             gc_ref.at[slot, d], ge_ref.at[slot, d], lev_ref[d], st_ref.at[d],
                             a_ref.at[d], ab_ref.at[d], oi_ref.at[d], d == 1, c)

    def pair(p, slot):
        p_next, p_prev = jnp.minimum(p + 1, nc - 1), jnp.maximum(p - 1, 0)
        for d in range(2):
            stage_t(p_prev, d)
            _interleave(stage_b(p, slot, d), stage_a(p_next, 1 - slot, d))

    def pairs(i, carry):
        for u in range(PAIRS_PER_BODY):
            pair(PAIRS_PER_BODY * i + u, u % 2)
        return carry

    for d in range(2):
        for _ in stage_a(0, 0, d):
            pass
    lax.fori_loop(0, nc // PAIRS_PER_BODY, pairs, 0)
    for d in range(2):
        stage_t(nc - 1, d)

    ng = ng_ref[...]
    rows_fin = FINISH_CHUNKS * c

    def finish(j, carry):
        rows = pl.ds(pl.multiple_of(j * rows_fin, rows_fin), rows_fin)
        o = acc_ref[rows, :]
        ms = jnp.mean(o * o, axis=-1, keepdims=True)
        o = o * lax.rsqrt(ms + EPS) * ng
        o_ref[0, rows, :] = (o * sg_ref[0, 0, rows, :].astype(F32)).astype(o_ref.dtype)
        return carry

    lax.fori_loop(0, nc // FINISH_CHUNKS, finish, 0)


def _hgrn(hg, norm_g):
    b, _, s, _ = hg.shape
    c = HG_CHUNK
    nc = s // c
    assert PAIRS_PER_BODY % 2 == 0 and nc % PAIRS_PER_BODY == 0 and c % (2 * GROUP) == 0
    lev, tcum, tsmall = _hgrn_consts(c)
    n_small = len(SMALL_W) * GROUP

    def kind(k):
        return pl.BlockSpec((1, 1, s, HG_DIM), lambda i, h: (i, k * HG_HEADS + h, 0, 0))

    const3 = lambda a: pl.BlockSpec(a.shape, lambda i, h: (0, 0, 0))
    return pl.pallas_call(
        functools.partial(_hgrn_kernel, c=c, nc=nc),
        grid=(b, HG_HEADS),
        in_specs=[kind(HK_Q), kind(HK_I), kind(HK_G), kind(HK_KF), kind(HK_KB),
                  kind(HK_GHF), kind(HK_GMF), kind(HK_GHB), kind(HK_GMB),
                  pl.BlockSpec((1, LANES), lambda i, h: (0, h)),
                  const3(lev), const3(tcum), const3(tsmall)],
        out_specs=pl.BlockSpec((1, s, LANES), lambda i, h: (i, 0, h)),
        out_shape=jax.ShapeDtypeStruct((b, s, HG_WIDTH), BF16),
        scratch_shapes=[pltpu.VMEM((s, HG_DIM), F32),
                        pltpu.VMEM((2, HG_DIM, HG_DIM), F32),
                        pltpu.VMEM((2, c, c), F32),
                        pltpu.VMEM((2, c, c), BF16),
                        pltpu.VMEM((2, c, HG_DIM), F32),
                        pltpu.VMEM((2, 2, c, LANES), F32),
                        pltpu.VMEM((2, 2, n_small, (c // GROUP) * LANES), F32)],
        compiler_params=_cparams(("arbitrary", "arbitrary")),
        name="hgrn",
    )(hg, hg, hg, hg, hg, hg, hg, hg, hg, norm_g, lev, tcum, tsmall)


def _attn_kernel(sink_ref, q_ref, kp_ref, ko_ref, kn_ref, vp_ref, vo_ref, vn_ref,
                 bias_ref, o_ref, lg_ref, *, nb):
    n = pl.program_id(1)
    half = LANES // 2
    kt = jnp.concatenate([kp_ref[0], ko_ref[0], kn_ref[0]], axis=1)
    vb = jnp.concatenate([vp_ref[0], vo_ref[0], vn_ref[0]], axis=0)
    top_k = lax.broadcasted_iota(jnp.int32, kt.shape, 0) < half
    lo_k = lax.broadcasted_iota(jnp.int32, vb.shape, 1) < half
    lo_q = lax.broadcasted_iota(jnp.int32, (BLOCK, LANES), 1) < half
    zero = jnp.zeros_like(vb)
    zero_t = jnp.zeros_like(kt)

    k1 = (jnp.where(top_k, kt, zero_t), jnp.where(top_k, zero_t, kt))
    v0lo, v1hi = jnp.where(lo_k, vb, zero), jnp.where(lo_k, zero, vb)
    v_lo = (v0lo, pltpu.roll(v1hi, half, axis=1))
    v_hi = (pltpu.roll(v0lo, half, axis=1), v1hi)

    edge = jnp.where(n == 0, 0, jnp.where(n == nb - 1, 2, 1))
    row_lo = lax.broadcasted_iota(jnp.int32, (2 * BLOCK, 1), 0) < BLOCK

    for m in range(ATT_Q_HEADS // 2):
        j = (2 * m) // ATT_GROUP
        qp = q_ref[0, :, m * LANES:(m + 1) * LANES]
        qr = pltpu.roll(qp, half, axis=1)
        q2 = jnp.concatenate([qp, qr] if j == 0 else [qr, qp], axis=0)
        lg_ref[m] = jnp.dot(q2, k1[j], preferred_element_type=F32) + bias_ref[edge, m]

    for m in range(ATT_Q_HEADS // 2):
        j = (2 * m) // ATT_GROUP
        logits = lg_ref[m]
        sink = jnp.where(row_lo, sink_ref[2 * m], sink_ref[2 * m + 1])
        mx = jnp.maximum(jnp.max(logits, axis=-1, keepdims=True), sink)
        p = jnp.exp2(logits - mx)
        den = jnp.sum(p, axis=-1, keepdims=True) + jnp.exp2(sink - mx)
        pb = p.astype(BF16)
        rden = 1.0 / den
        o = (jnp.dot(pb[:BLOCK], v_lo[j], preferred_element_type=F32)
             + jnp.dot(pb[BLOCK:], v_hi[j], preferred_element_type=F32))
        o = o * jnp.where(lo_q, rden[:BLOCK], rden[BLOCK:])
        o_ref[0, :, m * LANES:(m + 1) * LANES] = o.astype(o_ref.dtype)


def _attn(aq, ak, av, sink2, bias):
    b, s, _ = aq.shape
    nb = s // BLOCK
    assert nb >= 2

    def neighbour(n, shift):
        return jnp.clip(n + shift, 0, nb - 1)

    def kt(shift):
        return pl.BlockSpec((1, KV_WIDTH, BLOCK), lambda i, n: (i, 0, neighbour(n, shift)))

    def v(shift):
        return pl.BlockSpec((1, BLOCK, KV_WIDTH), lambda i, n: (i, neighbour(n, shift), 0))

    return pl.pallas_call(
        functools.partial(_attn_kernel, nb=nb),
        grid=(b, nb),
        in_specs=[pl.BlockSpec(memory_space=pltpu.SMEM),
                  pl.BlockSpec((1, BLOCK, ATT_WIDTH), lambda i, n: (i, n, 0)),
                  kt(-1), kt(0), kt(1), v(-1), v(0), v(1),
                  pl.BlockSpec(bias.shape, lambda i, n: (0, 0, 0, 0))],
        out_specs=pl.BlockSpec((1, BLOCK, ATT_WIDTH), lambda i, n: (i, n, 0)),
        out_shape=jax.ShapeDtypeStruct((b, s, ATT_WIDTH), BF16),
        scratch_shapes=[pltpu.VMEM((ATT_Q_HEADS // 2, 2 * BLOCK, 3 * BLOCK), F32)],
        compiler_params=_cparams(("arbitrary", "arbitrary")),
        name="attn",
    )(sink2, aq, ak, ak, ak, av, av, av, bias)


def _mixin_weight(w):
    hw = HG_WIDTH
    cols = [w[:, hw:3 * hw], w[:, 0:hw], w[:, 4 * hw:5 * hw], w[:, 5 * hw:],
            w[:, 3 * hw:4 * hw]]
    return jnp.concatenate(cols, axis=1).astype(BF16)


def kernel(x, c, w_ada, b_ada, norm_g, w_ffn1_in, w_ffn1_out, w_ffn2_in, w_ffn2_out,
           w_mix_in, w_mix_out, hgrn_lb, hgrn_norm_g, qk_norm_g, attn_sink, rel_bias):
    b, s, d = x.shape
    depth = w_ada.shape[0]
    bias = _bias_tile(rel_bias)
    for l in range(depth):
        mods = _ada(c.astype(F32), w_ada[l], b_ada[l][None, :])
        sh1, sc1, g1, sh2, sc2, g2, sh3, sc3, g3 = [
            mods[:, i * d:(i + 1) * d][:, None, :] for i in range(N_MOD)]
        x = _ffn(x, sh1, sc1, g1, norm_g[l, 0][None, :],
                 w_ffn1_in[l].astype(BF16), w_ffn1_out[l].astype(BF16))
        qg = (jnp.tile(qk_norm_g[l, 0], 2) * (LOG2E / math.sqrt(ATT_HEAD_DIM)))[None, :]
        kg = jnp.tile(qk_norm_g[l, 1], 2)[None, :]
        hg, aq, ak, av = _mixin(x, sh2, sc2, norm_g[l, 1][None, :], _mixin_weight(w_mix_in[l]),
                                hgrn_lb, qg, kg, l)
        o_hg = _hgrn(hg, hgrn_norm_g[l][None, :])
        att = _attn(aq, ak, av, attn_sink[l] * LOG2E, bias)
        x = _ffn(x, sh3, sc3, g3, norm_g[l, 2][None, :],
                 w_ffn2_in[l].astype(BF16), w_ffn2_out[l].astype(BF16),
                 mix=(o_hg, att, g2, w_mix_out[l].astype(BF16)))
    return x
```

```python
import functools
import math

import numpy as np
import jax
import jax.numpy as jnp
from jax import lax
from jax.experimental import pallas as pl
from jax.experimental.pallas import tpu as pltpu

F32 = jnp.float32
BF16 = jnp.bfloat16

D_MODEL = 1024
HG_HEADS = 4
HG_DIM = 128
HG_WIDTH = HG_HEADS * HG_DIM
ATT_Q_HEADS = 8
ATT_KV_HEADS = 2
ATT_HEAD_DIM = 64
ATT_GROUP = ATT_Q_HEADS // ATT_KV_HEADS
ATT_WIDTH = ATT_Q_HEADS * ATT_HEAD_DIM
KV_WIDTH = ATT_KV_HEADS * ATT_HEAD_DIM
WINDOW = 128
BLOCK = 128
NUM_BUCKETS = 32
MAX_DISTANCE = 128
D_FF = 2816
N_MOD = 9
EPS = 1e-6

LANES = 128
SUBLANES = 8
PACKED_ROWS = 16
MXU_COLS = 256
VMEM_LIMIT = 56 * 1024 * 1024

FFN_TM = 512
MIX_TM = 256
ATT_SUB = 1
HG_CHUNK = 128
NEG_INF = float("-inf")
LOG2E = 1.0 / math.log(2.0)

Z_FF, Z_FB, Z_Q, Z_G, Z_AQ = (n * HG_WIDTH for n in range(5))
Z_AK = Z_AQ + ATT_WIDTH
Z_AV = Z_AK + KV_WIDTH
Z_I = Z_AV + KV_WIDTH
Z_END = Z_I + HG_WIDTH
(HK_Q, HK_I, HK_G, HK_KF, HK_KB, HK_GHF, HK_GMF, HK_GHB, HK_GMB) = range(9)
N_HK = 9


def _cparams(sem, flags=None):
    return pltpu.CompilerParams(dimension_semantics=sem, vmem_limit_bytes=VMEM_LIMIT, flags=flags)


def _sigmoid(x):
    return 1.0 / (1.0 + jnp.exp(-x))


def _ada_kernel(ct_ref, w_ref, b_ref, o_ref):
    ct = ct_ref[...]
    ca = ct * _sigmoid(ct)
    w = w_ref[...]
    for b in range(ct.shape[1]):
        o_ref[b:b + 1, :] = jnp.sum(w * ca[:, b:b + 1], axis=0, keepdims=True) + b_ref[...]


def _ada(c, w, b):
    batch, d = c.shape
    n = w.shape[1]
    tn = 1024
    return pl.pallas_call(
        _ada_kernel,
        grid=(n // tn,),
        in_specs=[pl.BlockSpec((d, batch), lambda j: (0, 0)),
                  pl.BlockSpec((d, tn), lambda j: (0, j)),
                  pl.BlockSpec((1, tn), lambda j: (0, j))],
        out_specs=pl.BlockSpec((batch, tn), lambda j: (0, j)),
        out_shape=jax.ShapeDtypeStruct((batch, n), F32),
        compiler_params=_cparams(("arbitrary",)),
        name="ada",
    )(c.T, w, b)


def _t5_bucket_np(rel):
    nb = NUM_BUCKETS // 2
    max_exact = nb // 2
    ret = (rel > 0).astype(np.int32) * nb
    n = np.abs(rel)
    ratio = np.maximum(n, 1).astype(np.float32) / np.float32(max_exact)
    large = max_exact + (np.log(ratio) / np.float32(math.log(MAX_DISTANCE / max_exact))
                         * np.float32(nb - max_exact)).astype(np.int32)
    large = np.minimum(large, nb - 1)
    return ret + np.where(n < max_exact, n, large)


def _bias_kernel(rbt_ref, oh_ref, msk_ref, o_ref):
    b = jnp.dot(rbt_ref[...], oh_ref[...], precision=lax.Precision.HIGHEST,
                preferred_element_type=F32)
    o_ref[...] = (b * LOG2E)[None] + msk_ref[...]


def _bias_tile(rel_bias):
    kcol = np.arange(3 * BLOCK)[None, :]
    rel = (kcol - BLOCK) - np.arange(BLOCK)[:, None]
    bucket = _t5_bucket_np(rel).reshape(-1)
    onehot_t = (np.arange(NUM_BUCKETS)[:, None] == bucket[None, :]).astype(np.float32)
    window = np.abs(rel) <= WINDOW
    valid = np.stack([window & (kcol >= BLOCK), window, window & (kcol < 2 * BLOCK)])
    mask = np.where(valid, 0.0, NEG_INF).astype(np.float32).reshape(3, 1, -1)
    ncol = BLOCK * 3 * BLOCK
    tc = ncol // 4
    out = pl.pallas_call(
        _bias_kernel,
        grid=(ncol // tc,),
        in_specs=[pl.BlockSpec((ATT_Q_HEADS, NUM_BUCKETS), lambda i: (0, 0)),
                  pl.BlockSpec((NUM_BUCKETS, tc), lambda i: (0, i)),
                  pl.BlockSpec((3, 1, tc), lambda i: (0, 0, i))],
        out_specs=pl.BlockSpec((3, ATT_Q_HEADS, tc), lambda i: (0, 0, i)),
        out_shape=jax.ShapeDtypeStruct((3, ATT_Q_HEADS, ncol), F32),
        compiler_params=_cparams(("arbitrary",)),
        name="bias_tile",
    )(rel_bias.astype(F32).T, jnp.asarray(onehot_t), jnp.asarray(mask))
    return out.reshape(3, ATT_Q_HEADS // 2, 2 * BLOCK, 3 * BLOCK)


def _norm_mod(x, ng, sh, sc):
    ms = jnp.mean(x * x, axis=-1, keepdims=True)
    y = x * lax.rsqrt(ms + EPS) * ng
    return y * (1.0 + sc) + sh


def _ffn_kernel(*refs, mix):
    if mix:
        hg_ref, at_ref, gm_ref, wm_ref, *refs = refs
    x_ref, sh_ref, sc_ref, gt_ref, ng_ref, win_ref, wout_ref, o_ref = refs
    x = x_ref[0]
    if mix:
        mixed = (jnp.dot(hg_ref[0], wm_ref[:HG_WIDTH, :], preferred_element_type=F32)
                 + jnp.dot(at_ref[0], wm_ref[HG_WIDTH:, :], preferred_element_type=F32))
        x = x + gm_ref[0] * mixed
    h = _norm_mod(x, ng_ref[...], sh_ref[0], sc_ref[0]).astype(BF16)
    gu = jnp.dot(h, win_ref[...], preferred_element_type=F32)
    g = gu[:, :D_FF]
    u = gu[:, D_FF:]
    act = (g * _sigmoid(g) * u).astype(BF16)
    y = jnp.dot(act, wout_ref[...], preferred_element_type=F32)
    o_ref[0] = x + 0.5 * gt_ref[0] * y


def _ffn(x, sh, sc, gt, ng, w_in, w_out, mix=None):
    b, s, d = x.shape
    tm = FFN_TM
    vec = pl.BlockSpec((1, 1, d), lambda i, j: (i, 0, 0))
    whole = lambda a: pl.BlockSpec(a.shape, lambda i, j: (0, 0), pipeline_mode=pl.Buffered(1))
    args = [x, sh, sc, gt, ng, w_in, w_out]
    specs = [pl.BlockSpec((1, tm, d), lambda i, j: (i, j, 0)), vec, vec, vec,
             whole(ng), whole(w_in), whole(w_out)]
    if mix is not None:
        o_hg, att, gm, wm = mix
        half = pl.BlockSpec((1, tm, HG_WIDTH), lambda i, j: (i, j, 0))
        args = [o_hg, att, gm, wm] + args
        specs = [half, half, vec, whole(wm)] + specs
    return pl.pallas_call(
        functools.partial(_ffn_kernel, mix=mix is not None),
        grid=(b, s // tm),
        in_specs=specs,
        out_specs=pl.BlockSpec((1, tm, d), lambda i, j: (i, j, 0)),
        out_shape=jax.ShapeDtypeStruct(x.shape, F32),
        compiler_params=_cparams(("arbitrary", "arbitrary")),
        name="ffn_mix" if mix is not None else "ffn",
    )(*args)


def _half_norm(x, gain, lo):
    x2 = x * x
    s_lo = jnp.sum(jnp.where(lo, x2, 0.0), axis=-1, keepdims=True)
    s_hi = jnp.sum(jnp.where(lo, 0.0, x2), axis=-1, keepdims=True)
    ms = jnp.where(lo, s_lo, s_hi) * (1.0 / ATT_HEAD_DIM)
    return x * lax.rsqrt(ms + EPS) * gain


def _mixin_kernel(x_ref, sh_ref, sc_ref, ng_ref, w_ref, lb_ref, qg_ref, kg_ref,
                  hg_ref, aq_ref, ak_ref, av_ref, *, layer):
    h = _norm_mod(x_ref[0], ng_ref[...], sh_ref[0], sc_ref[0]).astype(BF16)

    def proj(c0, c1):
        return jnp.dot(h, w_ref[:, c0:c1], preferred_element_type=F32)

    def put_heads(kind, val):
        for hh in range(HG_HEADS):
            hg_ref[0, kind * HG_HEADS + hh] = val[:, hh * HG_DIM:(hh + 1) * HG_DIM].astype(BF16)

    raw = lb_ref[...]
    mx = jnp.max(raw, axis=1, keepdims=True)
    ex = jnp.exp(raw - mx)
    lb = jnp.sum(ex[:, :layer + 1, :], axis=1) / jnp.sum(ex, axis=1)

    for d, (z0, kk, kh, km) in enumerate(((Z_FF, HK_KF, HK_GHF, HK_GMF),
                                          (Z_FB, HK_KB, HK_GHB, HK_GMB))):
        fr = proj(z0, z0 + HG_WIDTH)
        lbd = lb[d:d + 1]
        f = lbd + (1.0 - lbd) * _sigmoid(fr)
        g2 = jnp.log(f) * LOG2E
        hi = g2.astype(BF16)
        put_heads(kk, 1.0 - f)
        put_heads(kh, hi)
        put_heads(km, g2 - hi.astype(F32))

    q = proj(Z_Q, Z_G)
    put_heads(HK_Q, q * _sigmoid(q))
    g = proj(Z_G, Z_AQ)
    put_heads(HK_G, g * _sigmoid(g))

    lo = lax.broadcasted_iota(jnp.int32, (h.shape[0], LANES), 1) < LANES // 2
    aq = proj(Z_AQ, Z_AK)
    for m in range(ATT_WIDTH // LANES):
        cols = slice(m * LANES, (m + 1) * LANES)
        aq_ref[0, :, cols] = _half_norm(aq[:, cols], qg_ref[...], lo).astype(BF16)
    akv = proj(Z_AK, Z_I)
    ak_ref[0] = _half_norm(akv[:, :KV_WIDTH], kg_ref[...], lo).T.astype(BF16)
    av_ref[0] = akv[:, KV_WIDTH:].astype(BF16)
    put_heads(HK_I, proj(Z_I, Z_END))


def _mixin(x, sh, sc, ng, w, hgrn_lb, qg, kg, layer):
    b, s, d = x.shape
    tm = MIX_TM
    vec = pl.BlockSpec((1, 1, d), lambda i, j: (i, 0, 0))
    row = lambda n: pl.BlockSpec((1, n), lambda i, j: (0, 0))
    rows3 = lambda n: pl.BlockSpec((1, tm, n), lambda i, j: (i, j, 0))
    n_hg = N_HK * HG_HEADS
    return pl.pallas_call(
        functools.partial(_mixin_kernel, layer=layer),
        grid=(b, s // tm),
        in_specs=[pl.BlockSpec((1, tm, d), lambda i, j: (i, j, 0)),
                  vec, vec, row(d),
                  pl.BlockSpec(w.shape, lambda i, j: (0, 0)),
                  pl.BlockSpec(hgrn_lb.shape, lambda i, j: (0, 0, 0)),
                  row(LANES), row(LANES)],
        out_specs=[pl.BlockSpec((1, n_hg, tm, HG_DIM), lambda i, j: (i, 0, j, 0)),
                   rows3(ATT_WIDTH),
                   pl.BlockSpec((1, KV_WIDTH, tm), lambda i, j: (i, 0, j)),
                   rows3(KV_WIDTH)],
        out_shape=[jax.ShapeDtypeStruct((b, n_hg, s, HG_DIM), BF16),
                   jax.ShapeDtypeStruct((b, s, ATT_WIDTH), BF16),
                   jax.ShapeDtypeStruct((b, KV_WIDTH, s), BF16),
                   jax.ShapeDtypeStruct((b, s, KV_WIDTH), BF16)],
        compiler_params=_cparams(("arbitrary", "arbitrary")),
        name="mix_in",
    )(x, sh, sc, ng, w, hgrn_lb, qg, kg)


SMALL_W = (1, 2, 4, 8)
GROUP = PACKED_ROWS
FINISH_CHUNKS = 8
PAIRS_PER_BODY = 8


def _hgrn_consts(c):
    t = np.arange(c)[:, None]
    s = np.arange(c)[None, :]
    x = t ^ s
    lev = np.where(x > 0, np.floor(np.log2(np.maximum(x, 1))), -1).astype(np.int32)
    lev_f = np.where(t > s, lev, np.where(t == s, -1, -2)).astype(np.int32)
    lev_b = lev_f.T.copy()

    def exponent_rows(w, rev):
        m = np.zeros((GROUP, GROUP), np.float32)
        for r in range(GROUP):
            b0 = (r // (2 * w)) * 2 * w
            if not rev:
                ref = b0 + w - 1
                lo_u, hi_u = (ref + 1, r) if r > ref else (r + 1, ref)
            else:
                ref = b0 + w
                lo_u, hi_u = (r, ref - 1) if r < ref else (ref, r - 1)
            m[r, lo_u:hi_u + 1] = 1.0
        return m

    cums, smalls = [], []
    for rev in (False, True):
        cum = ((s >= t) if rev else (s <= t)).astype(np.float32)
        cums.append(np.concatenate([cum, cum], axis=1))
        rows = np.concatenate([exponent_rows(w, rev) for w in SMALL_W])
        smalls.append(np.concatenate([rows, rows], axis=1))
    return (jnp.asarray(np.stack([lev_f, lev_b])),
            jnp.asarray(np.stack(cums), dtype=BF16),
            jnp.asarray(np.stack(smalls), dtype=BF16))


def _neg_abs(x):
    bits = lax.bitcast_convert_type(x, jnp.uint32) | jnp.uint32(0x80000000)
    return lax.bitcast_convert_type(bits, F32)


def _ref_rows(g_cum, w, rev, c):
    idx = w if rev else w - 1
    g3 = g_cum.reshape(c // (2 * w), 2 * w, LANES)
    return jnp.broadcast_to(g3[:, idx:idx + 1, :], g3.shape).reshape(c, LANES)


def _hgrn_stage_a(tcum, tsmall, gh, gm, gc_ref, ge_ref, c):
    hm = jnp.concatenate([gh, gm], axis=0)
    half = c // 2
    for r in range(2):
        rows = slice(r * half, (r + 1) * half)
        gc_ref[rows, :] = jnp.dot(tcum[rows], hm, preferred_element_type=F32)
        yield
    per_tile = MXU_COLS // LANES
    for t in range(c // GROUP // per_tile):
        groups = [slice(g * GROUP, (g + 1) * GROUP) for g in range(t * per_tile, (t + 1) * per_tile)]
        wide = jnp.concatenate([jnp.concatenate([gh[g] for g in groups], axis=1),
                                jnp.concatenate([gm[g] for g in groups], axis=1)], axis=0)
        ge_ref[:, t * MXU_COLS:(t + 1) * MXU_COLS] = jnp.dot(tsmall, wide,
                                                              preferred_element_type=F32)
        yield


def _hgrn_stage_b(q, k, v, gc_ref, ge_ref, lev, st_ref, a_ref, ab_ref, oi_ref, rev, c):
    tn = (((0,), (0,)), ((), ()))
    g_cum = gc_ref[...]
    g_last = g_cum[0:1, :] if rev else g_cum[c - 1:c, :]
    qf, kf = q.astype(F32), k.astype(F32)

    st = st_ref[...]
    qi = (qf * jnp.exp2(g_cum)).astype(BF16)
    oi_ref[...] = jnp.dot(qi, st.T.astype(BF16), preferred_element_type=F32)
    kd = (kf * jnp.exp2(g_last - g_cum)).astype(BF16)
    u_t = lax.dot_general(v, kd, tn, preferred_element_type=F32)
    st_ref[...] = jnp.exp2(g_last) * st + u_t
    yield

    w = c // 2
    while w >= GROUP:
        x = jnp.exp2(_neg_abs(g_cum - _ref_rows(g_cum, w, rev, c)))
        q_rows, k_rows = [], []
        for b in range(c // (2 * w)):
            lo_half = slice(b * 2 * w, b * 2 * w + w)
            hi_half = slice(b * 2 * w + w, (b + 1) * 2 * w)
            qs = lo_half if rev else hi_half
            q_rows.append(qf[qs] * x[qs])
            k_rows.extend([kf[lo_half], kf[hi_half] * x[hi_half]] if rev
                          else [kf[lo_half] * x[lo_half], kf[hi_half]])
        p = jnp.dot(jnp.concatenate(q_rows, axis=0).astype(BF16),
                    jnp.concatenate(k_rows, axis=0).T.astype(BF16),
                    preferred_element_type=F32)
        for b in range(c // (2 * w)):
            lo_half = slice(b * 2 * w, b * 2 * w + w)
            hi_half = slice(b * 2 * w + w, (b + 1) * 2 * w)
            qs, ks = (lo_half, hi_half) if rev else (hi_half, lo_half)
            a_ref[qs, ks] = p[b * w:(b + 1) * w, ks]
        w //= 2
        yield

    a = jnp.where(lev == -1, jnp.dot(q, kf.T.astype(BF16), preferred_element_type=F32), 0.0)
    for li in range(len(SMALL_W)):
        e = jnp.concatenate([ge_ref[li * GROUP:(li + 1) * GROUP, g * LANES:(g + 1) * LANES]
                             for g in range(c // GROUP)], axis=0)
        x = jnp.exp2(e)
        p = jnp.dot((qf * x).astype(BF16), (kf * x).T.astype(BF16),
                    preferred_element_type=F32)
        a = jnp.where(lev == li, p, a)
        yield
    for b in range(c // GROUP):
        blk = slice(b * GROUP, (b + 1) * GROUP)
        a_ref[blk, blk] = a[blk, blk]
    ab_ref[...] = a_ref[...].astype(BF16)
    yield


def _interleave(main, fill):
    for _ in main:
        next(fill, None)
    for _ in fill:
        pass


def _hgrn_kernel(q_ref, i_ref, sg_ref, kf_ref, kb_ref, ghf_ref, gmf_ref, ghb_ref, gmb_ref,
                 ng_ref, lev_ref, tc_ref, ts_ref, o_ref,
                 acc_ref, st_ref, a_ref, ab_ref, oi_ref, gc_ref, ge_ref, *, c, nc):
    for ref in (acc_ref, st_ref, a_ref, ab_ref, oi_ref):
        ref[...] = jnp.zeros_like(ref)
    k_refs = (kf_ref, kb_ref)
    g_refs = ((ghf_ref, gmf_ref), (ghb_ref, gmb_ref))

    def rows_of(p, d):
        j = p if d == 0 else nc - 1 - p
        return pl.ds(pl.multiple_of(j * c, c), c)

    def stage_a(p, slot, d):
        rows = rows_of(p, d)
        return _hgrn_stage_a(tc_ref[d], ts_ref[d], g_refs[d][0][0, 0, rows, :],
                             g_refs[d][1][0, 0, rows, :], gc_ref.at[slot, d], ge_ref.at[slot, d], c)

    def stage_t(p, d):
        rows = rows_of(p, d)
        acc_ref[rows, :] += oi_ref[d] + jnp.dot(ab_ref[d], i_ref[0, 0, rows, :],
                                                preferred_element_type=F32)

    def stage_b(p, slot, d):
        rows = rows_of(p, d)
        return _hgrn_stage_b(q_ref[0, 0, rows, :], k_refs[d][0, 0, rows, :], i_ref[0, 0, rows, :],
                             gc_ref.at[slot, d], ge_ref.at[slot, d], lev_ref[d], st_ref.at[d],
                             a_ref.at[d], ab_ref.at[d], oi_ref.at[d], d == 1, c)

    def pair(p, slot):
        p_next, p_prev = jnp.minimum(p + 1, nc - 1), jnp.maximum(p - 1, 0)
        for d in range(2):
            stage_t(p_prev, d)
            _interleave(stage_b(p, slot, d), stage_a(p_next, 1 - slot, d))

    def pairs(i, carry):
        for u in range(PAIRS_PER_BODY):
            pair(PAIRS_PER_BODY * i + u, u % 2)
        return carry

    for d in range(2):
        for _ in stage_a(0, 0, d):
            pass
    lax.fori_loop(0, nc // PAIRS_PER_BODY, pairs, 0)
    for d in range(2):
        stage_t(nc - 1, d)

    ng = ng_ref[...]
    rows_fin = FINISH_CHUNKS * c

    def finish(j, carry):
        rows = pl.ds(pl.multiple_of(j * rows_fin, rows_fin), rows_fin)
        o = acc_ref[rows, :]
        ms = jnp.mean(o * o, axis=-1, keepdims=True)
        o = o * lax.rsqrt(ms + EPS) * ng
        o_ref[0, rows, :] = (o * sg_ref[0, 0, rows, :].astype(F32)).astype(o_ref.dtype)
        return carry

    lax.fori_loop(0, nc // FINISH_CHUNKS, finish, 0)


def _hgrn(hg, norm_g):
    b, _, s, _ = hg.shape
    c = HG_CHUNK
    nc = s // c
    assert PAIRS_PER_BODY % 2 == 0 and nc % PAIRS_PER_BODY == 0 and c % (2 * GROUP) == 0
    lev, tcum, tsmall = _hgrn_consts(c)
    n_small = len(SMALL_W) * GROUP

    def kind(k):
        return pl.BlockSpec((1, 1, s, HG_DIM), lambda i, h: (i, k * HG_HEADS + h, 0, 0))

    const3 = lambda a: pl.BlockSpec(a.shape, lambda i, h: (0, 0, 0))
    return pl.pallas_call(
        functools.partial(_hgrn_kernel, c=c, nc=nc),
        grid=(b, HG_HEADS),
        in_specs=[kind(HK_Q), kind(HK_I), kind(HK_G), kind(HK_KF), kind(HK_KB),
                  kind(HK_GHF), kind(HK_GMF), kind(HK_GHB), kind(HK_GMB),
                  pl.BlockSpec((1, LANES), lambda i, h: (0, h)),
                  const3(lev), const3(tcum), const3(tsmall)],
        out_specs=pl.BlockSpec((1, s, LANES), lambda i, h: (i, 0, h)),
        out_shape=jax.ShapeDtypeStruct((b, s, HG_WIDTH), BF16),
        scratch_shapes=[pltpu.VMEM((s, HG_DIM), F32),
                        pltpu.VMEM((2, HG_DIM, HG_DIM), F32),
                        pltpu.VMEM((2, c, c), F32),
                        pltpu.VMEM((2, c, c), BF16),
                        pltpu.VMEM((2, c, HG_DIM), F32),
                        pltpu.VMEM((2, 2, c, LANES), F32),
                        pltpu.VMEM((2, 2, n_small, (c // GROUP) * LANES), F32)],
        compiler_params=_cparams(("arbitrary", "arbitrary")),
        name="hgrn",
    )(hg, hg, hg, hg, hg, hg, hg, hg, hg, norm_g, lev, tcum, tsmall)


def _attn_kernel(sink_ref, q_ref, kp_ref, ko_ref, kn_ref, vp_ref, vo_ref, vn_ref,
                 bias_ref, o_ref, lg_ref, *, n_steps):
    n = pl.program_id(1)
    half = LANES // 2
    nk = 3 * BLOCK
    kt = jnp.concatenate([kp_ref[0], ko_ref[0], kn_ref[0]], axis=1)
    vb = jnp.concatenate([vp_ref[0], vo_ref[0], vn_ref[0]], axis=0)
    lo = lax.broadcasted_iota(jnp.int32, (BLOCK, LANES), 1) < half
    lo_v = lax.broadcasted_iota(jnp.int32, vb.shape, 1) < half
    zero_v = jnp.zeros_like(vb)
    zero_k = jnp.zeros((half, nk), kt.dtype)
    kts = (kt[:half], kt[half:])
    v0lo, v1hi = jnp.where(lo_v, vb, zero_v), jnp.where(lo_v, zero_v, vb)
    v_lo = (v0lo, pltpu.roll(v1hi, half, axis=1))
    v_hi = (pltpu.roll(v0lo, half, axis=1), v1hi)

    def logits(i):
        keys = slice(i * BLOCK, i * BLOCK + nk)
        rows = slice(i * BLOCK, (i + 1) * BLOCK)
        for m in range(ATT_Q_HEADS // 2):
            kj = kts[(2 * m) // ATT_GROUP][:, keys]
            k_both = jnp.concatenate([jnp.concatenate([kj, zero_k], axis=0),
                                      jnp.concatenate([zero_k, kj], axis=0)], axis=1)
            lg_ref[i, m] = jnp.dot(q_ref[0, rows, m * LANES:(m + 1) * LANES], k_both,
                                   preferred_element_type=F32)

    def softmax_pv(i):
        keys = slice(i * BLOCK, i * BLOCK + nk)
        rows = slice(i * BLOCK, (i + 1) * BLOCK)
        edge = 1
        if i == ATT_SUB - 1:
            edge = jnp.where(n == n_steps - 1, 2, edge)
        if i == 0:
            edge = jnp.where(n == 0, 0, edge)
        for m in range(ATT_Q_HEADS // 2):
            j = (2 * m) // ATT_GROUP
            ps, rdens = [], []
            for hh in range(2):
                lg = (lg_ref[i, m, :, hh * nk:(hh + 1) * nk]
                      + bias_ref[edge, m, hh * BLOCK:(hh + 1) * BLOCK, :])
                sink = sink_ref[2 * m + hh]
                mx = jnp.maximum(jnp.max(lg, axis=-1, keepdims=True), sink)
                p = jnp.exp2(lg - mx)
                rdens.append(1.0 / (jnp.sum(p, axis=-1, keepdims=True) + jnp.exp2(sink - mx)))
                ps.append(p.astype(BF16))
            v_both = jnp.concatenate([v_lo[j][keys], v_hi[j][keys]], axis=0)
            o = jnp.dot(jnp.concatenate(ps, axis=1), v_both, preferred_element_type=F32)
            o_ref[0, rows, m * LANES:(m + 1) * LANES] = (
                o * jnp.where(lo, rdens[0], rdens[1])).astype(o_ref.dtype)

    logits(0)
    for i in range(ATT_SUB):
        if i + 1 < ATT_SUB:
            logits(i + 1)
        softmax_pv(i)


def _attn(aq, ak, av, sink2, bias):
    b, s, _ = aq.shape
    rows = ATT_SUB * BLOCK
    n_steps = s // rows
    nb = s // BLOCK
    assert s % rows == 0 and nb >= 2

    def edge_block(n, shift):
        return jnp.clip(n * ATT_SUB + shift, 0, nb - 1)

    kt_edge = lambda shift: pl.BlockSpec((1, KV_WIDTH, BLOCK), lambda i, n: (i, 0, edge_block(n, shift)))
    v_edge = lambda shift: pl.BlockSpec((1, BLOCK, KV_WIDTH), lambda i, n: (i, edge_block(n, shift), 0))
    return pl.pallas_call(
        functools.partial(_attn_kernel, n_steps=n_steps),
        grid=(b, n_steps),
        in_specs=[pl.BlockSpec(memory_space=pltpu.SMEM),
                  pl.BlockSpec((1, rows, ATT_WIDTH), lambda i, n: (i, n, 0)),
                  kt_edge(-1), pl.BlockSpec((1, KV_WIDTH, rows), lambda i, n: (i, 0, n)), kt_edge(ATT_SUB),
                  v_edge(-1), pl.BlockSpec((1, rows, KV_WIDTH), lambda i, n: (i, n, 0)), v_edge(ATT_SUB),
                  pl.BlockSpec(bias.shape, lambda i, n: (0, 0, 0, 0))],
        out_specs=pl.BlockSpec((1, rows, ATT_WIDTH), lambda i, n: (i, n, 0)),
        out_shape=jax.ShapeDtypeStruct((b, s, ATT_WIDTH), BF16),
        scratch_shapes=[pltpu.VMEM((ATT_SUB, ATT_Q_HEADS // 2, BLOCK, 2 * 3 * BLOCK), F32)],
        compiler_params=_cparams(("arbitrary", "arbitrary")),
        name="attn",
    )(sink2, aq, ak, ak, ak, av, av, av, bias)


def _mixin_weight(w):
    hw = HG_WIDTH
    cols = [w[:, hw:3 * hw], w[:, 0:hw], w[:, 4 * hw:5 * hw], w[:, 5 * hw:],
            w[:, 3 * hw:4 * hw]]
    return jnp.concatenate(cols, axis=1).astype(BF16)


def kernel(x, c, w_ada, b_ada, norm_g, w_ffn1_in, w_ffn1_out, w_ffn2_in, w_ffn2_out,
           w_mix_in, w_mix_out, hgrn_lb, hgrn_norm_g, qk_norm_g, attn_sink, rel_bias):
    b, s, d = x.shape
    depth = w_ada.shape[0]
    bias = _bias_tile(rel_bias)
    for l in range(depth):
        mods = _ada(c.astype(F32), w_ada[l], b_ada[l][None, :])
        sh1, sc1, g1, sh2, sc2, g2, sh3, sc3, g3 = [
            mods[:, i * d:(i + 1) * d][:, None, :] for i in range(N_MOD)]
        x = _ffn(x, sh1, sc1, g1, norm_g[l, 0][None, :],
                 w_ffn1_in[l].astype(BF16), w_ffn1_out[l].astype(BF16))
        qg = (jnp.tile(qk_norm_g[l, 0], 2) * (LOG2E / math.sqrt(ATT_HEAD_DIM)))[None, :]
        kg = jnp.tile(qk_norm_g[l, 1], 2)[None, :]
        hg, aq, ak, av = _mixin(x, sh2, sc2, norm_g[l, 1][None, :], _mixin_weight(w_mix_in[l]),
                                hgrn_lb, qg, kg, l)
        o_hg = _hgrn(hg, hgrn_norm_g[l][None, :])
        att = _attn(aq, ak, av, attn_sink[l] * LOG2E, bias)
        x = _ffn(x, sh3, sc3, g3, norm_g[l, 2][None, :],
                 w_ffn2_in[l].astype(BF16), w_ffn2_out[l].astype(BF16),
                 mix=(o_hg, att, g2, w_mix_out[l].astype(BF16)))
    return x
```

```python
import functools
import math

import numpy as np
import jax
import jax.numpy as jnp
from jax import lax
from jax.experimental import pallas as pl
from jax.experimental.pallas import tpu as pltpu

F32 = jnp.float32
BF16 = jnp.bfloat16

D_MODEL = 1024
HG_HEADS = 4
HG_DIM = 128
HG_WIDTH = HG_HEADS * HG_DIM
ATT_Q_HEADS = 8
ATT_KV_HEADS = 2
ATT_HEAD_DIM = 64
ATT_GROUP = ATT_Q_HEADS // ATT_KV_HEADS
ATT_WIDTH = ATT_Q_HEADS * ATT_HEAD_DIM
KV_WIDTH = ATT_KV_HEADS * ATT_HEAD_DIM
WINDOW = 128
BLOCK = 128
NUM_BUCKETS = 32
MAX_DISTANCE = 128
D_FF = 2816
N_MOD = 9
EPS = 1e-6

LANES = 128
SUBLANES = 8
PACKED_ROWS = 16
MXU_COLS = 256
VMEM_LIMIT = 56 * 1024 * 1024

FFN_TM = 512
MIX_TM = 256
ATT_SUB = 1
HG_CHUNK = 128
NEG_INF = float("-inf")
LOG2E = 1.0 / math.log(2.0)

Z_FF, Z_FB, Z_Q, Z_G, Z_AQ = (n * HG_WIDTH for n in range(5))
Z_AK = Z_AQ + ATT_WIDTH
Z_AV = Z_AK + KV_WIDTH
Z_I = Z_AV + KV_WIDTH
Z_END = Z_I + HG_WIDTH
(HK_Q, HK_I, HK_G, HK_KF, HK_KB, HK_GHF, HK_GMF, HK_GHB, HK_GMB) = range(9)
N_HK = 9


def _cparams(sem, flags=None):
    return pltpu.CompilerParams(dimension_semantics=sem, vmem_limit_bytes=VMEM_LIMIT, flags=flags)


def _sigmoid(x):
    return 1.0 / (1.0 + jnp.exp2(x * (-LOG2E)))


def _ada_kernel(ct_ref, w_ref, b_ref, o_ref):
    ct = ct_ref[...]
    ca = ct * _sigmoid(ct)
    w = w_ref[...]
    for b in range(ct.shape[1]):
        o_ref[b:b + 1, :] = jnp.sum(w * ca[:, b:b + 1], axis=0, keepdims=True) + b_ref[...]


def _ada(c, w, b):
    batch, d = c.shape
    n = w.shape[1]
    tn = 1024
    return pl.pallas_call(
        _ada_kernel,
        grid=(n // tn,),
        in_specs=[pl.BlockSpec((d, batch), lambda j: (0, 0)),
                  pl.BlockSpec((d, tn), lambda j: (0, j)),
                  pl.BlockSpec((1, tn), lambda j: (0, j))],
        out_specs=pl.BlockSpec((batch, tn), lambda j: (0, j)),
        out_shape=jax.ShapeDtypeStruct((batch, n), F32),
        compiler_params=_cparams(("arbitrary",)),
        name="ada",
    )(c.T, w, b)


def _t5_bucket_np(rel):
    nb = NUM_BUCKETS // 2
    max_exact = nb // 2
    ret = (rel > 0).astype(np.int32) * nb
    n = np.abs(rel)
    ratio = np.maximum(n, 1).astype(np.float32) / np.float32(max_exact)
    large = max_exact + (np.log(ratio) / np.float32(math.log(MAX_DISTANCE / max_exact))
                         * np.float32(nb - max_exact)).astype(np.int32)
    large = np.minimum(large, nb - 1)
    return ret + np.where(n < max_exact, n, large)


def _bias_kernel(rbt_ref, oh_ref, msk_ref, o_ref):
    b = jnp.dot(rbt_ref[...], oh_ref[...], precision=lax.Precision.HIGHEST,
                preferred_element_type=F32)
    o_ref[...] = (b * LOG2E)[None] + msk_ref[...]


def _bias_tile(rel_bias):
    kcol = np.arange(3 * BLOCK)[None, :]
    rel = (kcol - BLOCK) - np.arange(BLOCK)[:, None]
    bucket = _t5_bucket_np(rel).reshape(-1)
    onehot_t = (np.arange(NUM_BUCKETS)[:, None] == bucket[None, :]).astype(np.float32)
    window = np.abs(rel) <= WINDOW
    valid = np.stack([window & (kcol >= BLOCK), window, window & (kcol < 2 * BLOCK)])
    mask = np.where(valid, 0.0, NEG_INF).astype(np.float32).reshape(3, 1, -1)
    ncol = BLOCK * 3 * BLOCK
    tc = ncol // 4
    out = pl.pallas_call(
        _bias_kernel,
        grid=(ncol // tc,),
        in_specs=[pl.BlockSpec((ATT_Q_HEADS, NUM_BUCKETS), lambda i: (0, 0)),
                  pl.BlockSpec((NUM_BUCKETS, tc), lambda i: (0, i)),
                  pl.BlockSpec((3, 1, tc), lambda i: (0, 0, i))],
        out_specs=pl.BlockSpec((3, ATT_Q_HEADS, tc), lambda i: (0, 0, i)),
        out_shape=jax.ShapeDtypeStruct((3, ATT_Q_HEADS, ncol), F32),
        compiler_params=_cparams(("arbitrary",)),
        name="bias_tile",
    )(rel_bias.astype(F32).T, jnp.asarray(onehot_t), jnp.asarray(mask))
    return out.reshape(3, ATT_Q_HEADS // 2, 2 * BLOCK, 3 * BLOCK)


def _norm_mod(x, ng, sh, sc):
    ms = jnp.mean(x * x, axis=-1, keepdims=True)
    return (x * lax.rsqrt(ms + EPS)) * (ng * (1.0 + sc)) + sh


def _ffn_kernel(*refs, mix):
    if mix:
        hg_ref, at_ref, gm_ref, wm_ref, *refs = refs
    x_ref, sh_ref, sc_ref, gt_ref, ng_ref, win_ref, wout_ref, o_ref = refs
    x = x_ref[0]
    if mix:
        mixed = (jnp.dot(hg_ref[0], wm_ref[:HG_WIDTH, :], preferred_element_type=F32)
                 + jnp.dot(at_ref[0], wm_ref[HG_WIDTH:, :], preferred_element_type=F32))
        x = x + gm_ref[0] * mixed
    h = _norm_mod(x, ng_ref[...], sh_ref[0], sc_ref[0]).astype(BF16)
    gu = jnp.dot(h, win_ref[...], preferred_element_type=F32)
    g = gu[:, :D_FF]
    u = gu[:, D_FF:]
    act = (g * _sigmoid(g) * u).astype(BF16)
    y = jnp.dot(act, wout_ref[...], preferred_element_type=F32)
    o_ref[0] = x + 0.5 * gt_ref[0] * y


def _ffn(x, sh, sc, gt, ng, w_in, w_out, mix=None):
    b, s, d = x.shape
    tm = FFN_TM
    vec = pl.BlockSpec((1, 1, d), lambda i, j: (i, 0, 0))
    whole = lambda a: pl.BlockSpec(a.shape, lambda i, j: (0, 0), pipeline_mode=pl.Buffered(1))
    args = [x, sh, sc, gt, ng, w_in, w_out]
    specs = [pl.BlockSpec((1, tm, d), lambda i, j: (i, j, 0)), vec, vec, vec,
             whole(ng), whole(w_in), whole(w_out)]
    if mix is not None:
        o_hg, att, gm, wm = mix
        half = pl.BlockSpec((1, tm, HG_WIDTH), lambda i, j: (i, j, 0))
        args = [o_hg, att, gm, wm] + args
        specs = [half, half, vec, whole(wm)] + specs
    return pl.pallas_call(
        functools.partial(_ffn_kernel, mix=mix is not None),
        grid=(b, s // tm),
        in_specs=specs,
        out_specs=pl.BlockSpec((1, tm, d), lambda i, j: (i, j, 0)),
        out_shape=jax.ShapeDtypeStruct(x.shape, F32),
        compiler_params=_cparams(("arbitrary", "arbitrary")),
        name="ffn_mix" if mix is not None else "ffn",
    )(*args)


def _half_norm(x, gain, lo):
    x2 = x * x
    s_lo = jnp.sum(jnp.where(lo, x2, 0.0), axis=-1, keepdims=True)
    s_hi = jnp.sum(jnp.where(lo, 0.0, x2), axis=-1, keepdims=True)
    ms = jnp.where(lo, s_lo, s_hi) * (1.0 / ATT_HEAD_DIM)
    return x * lax.rsqrt(ms + EPS) * gain


def _mixin_kernel(x_ref, sh_ref, sc_ref, ng_ref, w_ref, lb_ref, qg_ref, kg_ref,
                  hg_ref, aq_ref, ak_ref, av_ref, *, layer):
    h = _norm_mod(x_ref[0], ng_ref[...], sh_ref[0], sc_ref[0]).astype(BF16)

    def proj(c0, c1):
        return jnp.dot(h, w_ref[:, c0:c1], preferred_element_type=F32)

    part_w = HG_WIDTH // 2

    def put_heads(kind, val, part):
        for hh in range(part_w // HG_DIM):
            head = part * (part_w // HG_DIM) + hh
            hg_ref[0, kind * HG_HEADS + head] = val[:, hh * HG_DIM:(hh + 1) * HG_DIM].astype(BF16)

    raw = lb_ref[...]
    mx = jnp.max(raw, axis=1, keepdims=True)
    ex = jnp.exp(raw - mx)
    lb = jnp.sum(ex[:, :layer + 1, :], axis=1) / jnp.sum(ex, axis=1)

    def forget(d, z0, kk, kh, km, part):
        cols = slice(part * part_w, (part + 1) * part_w)
        fr = proj(z0 + cols.start, z0 + cols.stop)
        lbd = lb[d:d + 1, cols]
        f = lbd + (1.0 - lbd) * _sigmoid(fr)
        g2 = jnp.log(f) * LOG2E
        hi = g2.astype(BF16)
        put_heads(kk, 1.0 - f, part)
        put_heads(kh, hi, part)
        put_heads(km, g2 - hi.astype(F32), part)

    def plain(kind, z0, part, silu):
        val = proj(z0 + part * part_w, z0 + (part + 1) * part_w)
        put_heads(kind, val * _sigmoid(val) if silu else val, part)

    lo = lax.broadcasted_iota(jnp.int32, (h.shape[0], LANES), 1) < LANES // 2

    def queries(part):
        aq = proj(Z_AQ + part * part_w, Z_AQ + (part + 1) * part_w)
        for m in range(part_w // LANES):
            cols = slice(m * LANES, (m + 1) * LANES)
            aq_ref[0, :, part * part_w + m * LANES:part * part_w + (m + 1) * LANES] = (
                _half_norm(aq[:, cols], qg_ref[...], lo).astype(BF16))

    def keys_values():
        akv = proj(Z_AK, Z_I)
        ak_ref[0] = _half_norm(akv[:, :KV_WIDTH], kg_ref[...], lo).T.astype(BF16)
        av_ref[0] = akv[:, KV_WIDTH:].astype(BF16)

    forget(0, Z_FF, HK_KF, HK_GHF, HK_GMF, 0)
    plain(HK_I, Z_I, 0, False)
    forget(0, Z_FF, HK_KF, HK_GHF, HK_GMF, 1)
    plain(HK_I, Z_I, 1, False)
    forget(1, Z_FB, HK_KB, HK_GHB, HK_GMB, 0)
    keys_values()
    forget(1, Z_FB, HK_KB, HK_GHB, HK_GMB, 1)
    plain(HK_Q, Z_Q, 0, True)
    queries(0)
    plain(HK_Q, Z_Q, 1, True)
    queries(1)
    plain(HK_G, Z_G, 0, True)
    plain(HK_G, Z_G, 1, True)


def _mixin(x, sh, sc, ng, w, hgrn_lb, qg, kg, layer):
    b, s, d = x.shape
    tm = MIX_TM
    vec = pl.BlockSpec((1, 1, d), lambda i, j: (i, 0, 0))
    row = lambda n: pl.BlockSpec((1, n), lambda i, j: (0, 0))
    rows3 = lambda n: pl.BlockSpec((1, tm, n), lambda i, j: (i, j, 0))
    n_hg = N_HK * HG_HEADS
    return pl.pallas_call(
        functools.partial(_mixin_kernel, layer=layer),
        grid=(b, s // tm),
        in_specs=[pl.BlockSpec((1, tm, d), lambda i, j: (i, j, 0)),
                  vec, vec, row(d),
                  pl.BlockSpec(w.shape, lambda i, j: (0, 0)),
                  pl.BlockSpec(hgrn_lb.shape, lambda i, j: (0, 0, 0)),
                  row(LANES), row(LANES)],
        out_specs=[pl.BlockSpec((1, n_hg, tm, HG_DIM), lambda i, j: (i, 0, j, 0)),
                   rows3(ATT_WIDTH),
                   pl.BlockSpec((1, KV_WIDTH, tm), lambda i, j: (i, 0, j)),
                   rows3(KV_WIDTH)],
        out_shape=[jax.ShapeDtypeStruct((b, n_hg, s, HG_DIM), BF16),
                   jax.ShapeDtypeStruct((b, s, ATT_WIDTH), BF16),
                   jax.ShapeDtypeStruct((b, KV_WIDTH, s), BF16),
                   jax.ShapeDtypeStruct((b, s, KV_WIDTH), BF16)],
        compiler_params=_cparams(("arbitrary", "arbitrary")),
        name="mix_in",
    )(x, sh, sc, ng, w, hgrn_lb, qg, kg)


SMALL_W = (1, 2, 4, 8)
GROUP = PACKED_ROWS
FINISH_CHUNKS = 8
PAIRS_PER_BODY = 8


def _hgrn_consts(c):
    t = np.arange(c)[:, None]
    s = np.arange(c)[None, :]
    x = t ^ s
    lev = np.where(x > 0, np.floor(np.log2(np.maximum(x, 1))), -1).astype(np.int32)
    lev_f = np.where(t > s, lev, np.where(t == s, -1, -2)).astype(np.int32)
    lev_b = lev_f.T.copy()

    def exponent_rows(w, rev):
        m = np.zeros((GROUP, GROUP), np.float32)
        for r in range(GROUP):
            b0 = (r // (2 * w)) * 2 * w
            if not rev:
                ref = b0 + w - 1
                lo_u, hi_u = (ref + 1, r) if r > ref else (r + 1, ref)
            else:
                ref = b0 + w
                lo_u, hi_u = (r, ref - 1) if r < ref else (ref, r - 1)
            m[r, lo_u:hi_u + 1] = 1.0
        return m

    cums, smalls = [], []
    for rev in (False, True):
        cum = ((s >= t) if rev else (s <= t)).astype(np.float32)
        cums.append(np.concatenate([cum, cum], axis=1))
        rows = np.concatenate([exponent_rows(w, rev) for w in SMALL_W])
        smalls.append(np.concatenate([rows, rows], axis=1))
    return (jnp.asarray(np.stack([lev_f, lev_b])),
            jnp.asarray(np.stack(cums), dtype=BF16),
            jnp.asarray(np.stack(smalls), dtype=BF16))


def _neg_abs(x):
    bits = lax.bitcast_convert_type(x, jnp.uint32) | jnp.uint32(0x80000000)
    return lax.bitcast_convert_type(bits, F32)


def _ref_rows(g_cum, w, rev, c):
    idx = w if rev else w - 1
    g3 = g_cum.reshape(c // (2 * w), 2 * w, LANES)
    return jnp.broadcast_to(g3[:, idx:idx + 1, :], g3.shape).reshape(c, LANES)


def _hgrn_stage_a(tcum, tsmall, gh, gm, gc_ref, ge_ref, c):
    hm = jnp.concatenate([gh, gm], axis=0)
    half = c // 2
    for r in range(2):
        rows = slice(r * half, (r + 1) * half)
        gc_ref[rows, :] = jnp.dot(tcum[rows], hm, preferred_element_type=F32)
        yield
    per_tile = MXU_COLS // LANES
    for t in range(c // GROUP // per_tile):
        groups = [slice(g * GROUP, (g + 1) * GROUP) for g in range(t * per_tile, (t + 1) * per_tile)]
        wide = jnp.concatenate([jnp.concatenate([gh[g] for g in groups], axis=1),
                                jnp.concatenate([gm[g] for g in groups], axis=1)], axis=0)
        ge_ref[:, t * MXU_COLS:(t + 1) * MXU_COLS] = jnp.dot(tsmall, wide,
                                                              preferred_element_type=F32)
        yield


def _hgrn_stage_b(q, k, v, gc_ref, ge_ref, lev, st_ref, a_ref, ab_ref, oi_ref, rev, c):
    tn = (((0,), (0,)), ((), ()))
    g_cum = gc_ref[...]
    g_last = g_cum[0:1, :] if rev else g_cum[c - 1:c, :]
    qf, kf = q.astype(F32), k.astype(F32)

    st = st_ref[...]
    qi = (qf * jnp.exp2(g_cum)).astype(BF16)
    oi_ref[...] = jnp.dot(qi, st.T.astype(BF16), preferred_element_type=F32)
    kd = (kf * jnp.exp2(g_last - g_cum)).astype(BF16)
    u_t = lax.dot_general(v, kd, tn, preferred_element_type=F32)
    st_ref[...] = jnp.exp2(g_last) * st + u_t
    yield

    w = c // 2
    while w >= GROUP:
        x = jnp.exp2(_neg_abs(g_cum - _ref_rows(g_cum, w, rev, c)))
        q_rows, k_rows = [], []
        for b in range(c // (2 * w)):
            lo_half = slice(b * 2 * w, b * 2 * w + w)
            hi_half = slice(b * 2 * w + w, (b + 1) * 2 * w)
            qs = lo_half if rev else hi_half
            q_rows.append(qf[qs] * x[qs])
            k_rows.extend([kf[lo_half], kf[hi_half] * x[hi_half]] if rev
                          else [kf[lo_half] * x[lo_half], kf[hi_half]])
        p = jnp.dot(jnp.concatenate(q_rows, axis=0).astype(BF16),
                    jnp.concatenate(k_rows, axis=0).T.astype(BF16),
                    preferred_element_type=F32)
        for b in range(c // (2 * w)):
            lo_half = slice(b * 2 * w, b * 2 * w + w)
            hi_half = slice(b * 2 * w + w, (b + 1) * 2 * w)
            qs, ks = (lo_half, hi_half) if rev else (hi_half, lo_half)
            a_ref[qs, ks] = p[b * w:(b + 1) * w, ks]
        w //= 2
        yield

    a = jnp.where(lev == -1, jnp.dot(q, kf.T.astype(BF16), preferred_element_type=F32), 0.0)
    for li in range(len(SMALL_W)):
        e = jnp.concatenate([ge_ref[li * GROUP:(li + 1) * GROUP, g * LANES:(g + 1) * LANES]
                             for g in range(c // GROUP)], axis=0)
        x = jnp.exp2(e)
        p = jnp.dot((qf * x).astype(BF16), (kf * x).T.astype(BF16),
                    preferred_element_type=F32)
        a = jnp.where(lev == li, p, a)
        yield
    for b in range(c // GROUP):
        blk = slice(b * GROUP, (b + 1) * GROUP)
        a_ref[blk, blk] = a[blk, blk]
    ab_ref[...] = a_ref[...].astype(BF16)
    yield


def _interleave(main, fill):
    for _ in main:
        next(fill, None)
    for _ in fill:
        pass


def _hgrn_kernel(q_ref, i_ref, sg_ref, kf_ref, kb_ref, ghf_ref, gmf_ref, ghb_ref, gmb_ref,
                 ng_ref, lev_ref, tc_ref, ts_ref, o_ref,
                 acc_ref, st_ref, a_ref, ab_ref, oi_ref, gc_ref, ge_ref, *, c, nc):
    for ref in (acc_ref, st_ref, a_ref, ab_ref, oi_ref):
        ref[...] = jnp.zeros_like(ref)
    k_refs = (kf_ref, kb_ref)
    g_refs = ((ghf_ref, gmf_ref), (ghb_ref, gmb_ref))

    def rows_of(p, d):
        j = p if d == 0 else nc - 1 - p
        return pl.ds(pl.multiple_of(j * c, c), c)

    def stage_a(p, slot, d):
        rows = rows_of(p, d)
        return _hgrn_stage_a(tc_ref[d], ts_ref[d], g_refs[d][0][0, 0, rows, :],
                             g_refs[d][1][0, 0, rows, :], gc_ref.at[slot, d], ge_ref.at[slot, d], c)

    def stage_t(p, d):
        rows = rows_of(p, d)
        acc_ref[rows, :] += oi_ref[d] + jnp.dot(ab_ref[d], i_ref[0, 0, rows, :],
                                                preferred_element_type=F32)

    def stage_b(p, slot, d):
        rows = rows_of(p, d)
        return _hgrn_stage_b(q_ref[0, 0, rows, :], k_refs[d][0, 0, rows, :], i_ref[0, 0, rows, :],
                             gc_ref.at[slot, d], ge_ref.at[slot, d], lev_ref[d], st_ref.at[d],
                             a_ref.at[d], ab_ref.at[d], oi_ref.at[d], d == 1, c)

    def pair(p, slot):
        p_next, p_prev = jnp.minimum(p + 1, nc - 1), jnp.maximum(p - 1, 0)
        for d in range(2):
            stage_t(p_prev, d)
            _interleave(stage_b(p, slot, d), stage_a(p_next, 1 - slot, d))

    def pairs(i, carry):
        for u in range(PAIRS_PER_BODY):
            pair(PAIRS_PER_BODY * i + u, u % 2)
        return carry

    for d in range(2):
        for _ in stage_a(0, 0, d):
            pass
    lax.fori_loop(0, nc // PAIRS_PER_BODY, pairs, 0)
    for d in range(2):
        stage_t(nc - 1, d)

    ng = ng_ref[...]
    rows_fin = FINISH_CHUNKS * c

    def finish(j, carry):
        rows = pl.ds(pl.multiple_of(j * rows_fin, rows_fin), rows_fin)
        o = acc_ref[rows, :]
        ms = jnp.mean(o * o, axis=-1, keepdims=True)
        o = o * lax.rsqrt(ms + EPS) * ng
        o_ref[0, rows, :] = (o * sg_ref[0, 0, rows, :].astype(F32)).astype(o_ref.dtype)
        return carry

    lax.fori_loop(0, nc // FINISH_CHUNKS, finish, 0)


def _hgrn(hg, norm_g):
    b, _, s, _ = hg.shape
    c = HG_CHUNK
    nc = s // c
    assert PAIRS_PER_BODY % 2 == 0 and nc % PAIRS_PER_BODY == 0 and c % (2 * GROUP) == 0
    lev, tcum, tsmall = _hgrn_consts(c)
    n_small = len(SMALL_W) * GROUP

    def kind(k):
        return pl.BlockSpec((1, 1, s, HG_DIM), lambda i, h: (i, k * HG_HEADS + h, 0, 0))

    const3 = lambda a: pl.BlockSpec(a.shape, lambda i, h: (0, 0, 0))
    return pl.pallas_call(
        functools.partial(_hgrn_kernel, c=c, nc=nc),
        grid=(b, HG_HEADS),
        in_specs=[kind(HK_Q), kind(HK_I), kind(HK_G), kind(HK_KF), kind(HK_KB),
                  kind(HK_GHF), kind(HK_GMF), kind(HK_GHB), kind(HK_GMB),
                  pl.BlockSpec((1, LANES), lambda i, h: (0, h)),
                  const3(lev), const3(tcum), const3(tsmall)],
        out_specs=pl.BlockSpec((1, s, LANES), lambda i, h: (i, 0, h)),
        out_shape=jax.ShapeDtypeStruct((b, s, HG_WIDTH), BF16),
        scratch_shapes=[pltpu.VMEM((s, HG_DIM), F32),
                        pltpu.VMEM((2, HG_DIM, HG_DIM), F32),
                        pltpu.VMEM((2, c, c), F32),
                        pltpu.VMEM((2, c, c), BF16),
                        pltpu.VMEM((2, c, HG_DIM), F32),
                        pltpu.VMEM((2, 2, c, LANES), F32),
                        pltpu.VMEM((2, 2, n_small, (c // GROUP) * LANES), F32)],
        compiler_params=_cparams(("arbitrary", "arbitrary")),
        name="hgrn",
    )(hg, hg, hg, hg, hg, hg, hg, hg, hg, norm_g, lev, tcum, tsmall)


def _attn_kernel(sink_ref, q_ref, kp_ref, ko_ref, kn_ref, vp_ref, vo_ref, vn_ref,
                 bias_ref, o_ref, lg_ref, *, n_steps):
    n = pl.program_id(1)
    half = LANES // 2
    nk = 3 * BLOCK
    kt = jnp.concatenate([kp_ref[0], ko_ref[0], kn_ref[0]], axis=1)
    vb = jnp.concatenate([vp_ref[0], vo_ref[0], vn_ref[0]], axis=0)
    lo = lax.broadcasted_iota(jnp.int32, (BLOCK, LANES), 1) < half
    lo_v = lax.broadcasted_iota(jnp.int32, vb.shape, 1) < half
    zero_v = jnp.zeros_like(vb)
    zero_k = jnp.zeros((half, nk), kt.dtype)
    kts = (kt[:half], kt[half:])
    v0lo, v1hi = jnp.where(lo_v, vb, zero_v), jnp.where(lo_v, zero_v, vb)
    v_lo = (v0lo, pltpu.roll(v1hi, half, axis=1))
    v_hi = (pltpu.roll(v0lo, half, axis=1), v1hi)

    def logits(i):
        keys = slice(i * BLOCK, i * BLOCK + nk)
        rows = slice(i * BLOCK, (i + 1) * BLOCK)
        for m in range(ATT_Q_HEADS // 2):
            kj = kts[(2 * m) // ATT_GROUP][:, keys]
            k_both = jnp.concatenate([jnp.concatenate([kj, zero_k], axis=0),
                                      jnp.concatenate([zero_k, kj], axis=0)], axis=1)
            lg_ref[i, m] = jnp.dot(q_ref[0, rows, m * LANES:(m + 1) * LANES], k_both,
                                   preferred_element_type=F32)

    def softmax_pv(i):
        keys = slice(i * BLOCK, i * BLOCK + nk)
        rows = slice(i * BLOCK, (i + 1) * BLOCK)
        edge = 1
        if i == ATT_SUB - 1:
            edge = jnp.where(n == n_steps - 1, 2, edge)
        if i == 0:
            edge = jnp.where(n == 0, 0, edge)
        for m in range(ATT_Q_HEADS // 2):
            j = (2 * m) // ATT_GROUP
            ps, rdens = [], []
            for hh in range(2):
                lg = (lg_ref[i, m, :, hh * nk:(hh + 1) * nk]
                      + bias_ref[edge, m, hh * BLOCK:(hh + 1) * BLOCK, :])
                sink = sink_ref[2 * m + hh]
                mx = jnp.maximum(jnp.max(lg, axis=-1, keepdims=True), sink)
                p = jnp.exp2(lg - mx)
                rdens.append(1.0 / (jnp.sum(p, axis=-1, keepdims=True) + jnp.exp2(sink - mx)))
                ps.append(p.astype(BF16))
            v_both = jnp.concatenate([v_lo[j][keys], v_hi[j][keys]], axis=0)
            o = jnp.dot(jnp.concatenate(ps, axis=1), v_both, preferred_element_type=F32)
            o_ref[0, rows, m * LANES:(m + 1) * LANES] = (
                o * jnp.where(lo, rdens[0], rdens[1])).astype(o_ref.dtype)

    logits(0)
    for i in range(ATT_SUB):
        if i + 1 < ATT_SUB:
            logits(i + 1)
        softmax_pv(i)


def _attn(aq, ak, av, sink2, bias):
    b, s, _ = aq.shape
    rows = ATT_SUB * BLOCK
    n_steps = s // rows
    nb = s // BLOCK
    assert s % rows == 0 and nb >= 2

    def edge_block(n, shift):
        return jnp.clip(n * ATT_SUB + shift, 0, nb - 1)

    kt_edge = lambda shift: pl.BlockSpec((1, KV_WIDTH, BLOCK), lambda i, n: (i, 0, edge_block(n, shift)))
    v_edge = lambda shift: pl.BlockSpec((1, BLOCK, KV_WIDTH), lambda i, n: (i, edge_block(n, shift), 0))
    return pl.pallas_call(
        functools.partial(_attn_kernel, n_steps=n_steps),
        grid=(b, n_steps),
        in_specs=[pl.BlockSpec(memory_space=pltpu.SMEM),
                  pl.BlockSpec((1, rows, ATT_WIDTH), lambda i, n: (i, n, 0)),
                  kt_edge(-1), pl.BlockSpec((1, KV_WIDTH, rows), lambda i, n: (i, 0, n)), kt_edge(ATT_SUB),
                  v_edge(-1), pl.BlockSpec((1, rows, KV_WIDTH), lambda i, n: (i, n, 0)), v_edge(ATT_SUB),
                  pl.BlockSpec(bias.shape, lambda i, n: (0, 0, 0, 0))],
        out_specs=pl.BlockSpec((1, rows, ATT_WIDTH), lambda i, n: (i, n, 0)),
        out_shape=jax.ShapeDtypeStruct((b, s, ATT_WIDTH), BF16),
        scratch_shapes=[pltpu.VMEM((ATT_SUB, ATT_Q_HEADS // 2, BLOCK, 2 * 3 * BLOCK), F32)],
        compiler_params=_cparams(("arbitrary", "arbitrary")),
        name="attn",
    )(sink2, aq, ak, ak, ak, av, av, av, bias)


def _mixin_weight(w):
    hw = HG_WIDTH
    cols = [w[:, hw:3 * hw], w[:, 0:hw], w[:, 4 * hw:5 * hw], w[:, 5 * hw:],
            w[:, 3 * hw:4 * hw]]
    return jnp.concatenate(cols, axis=1).astype(BF16)


def kernel(x, c, w_ada, b_ada, norm_g, w_ffn1_in, w_ffn1_out, w_ffn2_in, w_ffn2_out,
           w_mix_in, w_mix_out, hgrn_lb, hgrn_norm_g, qk_norm_g, attn_sink, rel_bias):
    b, s, d = x.shape
    depth = w_ada.shape[0]
    bias = _bias_tile(rel_bias)
    for l in range(depth):
        mods = _ada(c.astype(F32), w_ada[l], b_ada[l][None, :])
        sh1, sc1, g1, sh2, sc2, g2, sh3, sc3, g3 = [
            mods[:, i * d:(i + 1) * d][:, None, :] for i in range(N_MOD)]
        x = _ffn(x, sh1, sc1, g1, norm_g[l, 0][None, :],
                 w_ffn1_in[l].astype(BF16), w_ffn1_out[l].astype(BF16))
        qg = (jnp.tile(qk_norm_g[l, 0], 2) * (LOG2E / math.sqrt(ATT_HEAD_DIM)))[None, :]
        kg = jnp.tile(qk_norm_g[l, 1], 2)[None, :]
        hg, aq, ak, av = _mixin(x, sh2, sc2, norm_g[l, 1][None, :], _mixin_weight(w_mix_in[l]),
                                hgrn_lb, qg, kg, l)
        o_hg = _hgrn(hg, hgrn_norm_g[l][None, :])
        att = _attn(aq, ak, av, attn_sink[l] * LOG2E, bias)
        x = _ffn(x, sh3, sc3, g3, norm_g[l, 2][None, :],
                 w_ffn2_in[l].astype(BF16), w_ffn2_out[l].astype(BF16),
                 mix=(o_hg, att, g2, w_mix_out[l].astype(BF16)))
    return x
```

```python
import functools
import math

import numpy as np
import jax
import jax.numpy as jnp
from jax import lax
from jax.experimental import pallas as pl
from jax.experimental.pallas import tpu as pltpu

F32 = jnp.float32
BF16 = jnp.bfloat16

D_MODEL = 1024
HG_HEADS = 4
HG_DIM = 128
HG_WIDTH = HG_HEADS * HG_DIM
ATT_Q_HEADS = 8
ATT_KV_HEADS = 2
ATT_HEAD_DIM = 64
ATT_GROUP = ATT_Q_HEADS // ATT_KV_HEADS
ATT_WIDTH = ATT_Q_HEADS * ATT_HEAD_DIM
KV_WIDTH = ATT_KV_HEADS * ATT_HEAD_DIM
WINDOW = 128
BLOCK = 128
NUM_BUCKETS = 32
MAX_DISTANCE = 128
D_FF = 2816
N_MOD = 9
EPS = 1e-6

LANES = 128
SUBLANES = 8
PACKED_ROWS = 16
MXU_COLS = 256
VMEM_LIMIT = 56 * 1024 * 1024

FFN_TM = 512
MIX_TM = 256
ATT_SUB = 1
HG_CHUNK = 128
NEG_INF = float("-inf")
LOG2E = 1.0 / math.log(2.0)

Z_FF, Z_FB, Z_Q, Z_G, Z_AQ = (n * HG_WIDTH for n in range(5))
Z_AK = Z_AQ + ATT_WIDTH
Z_AV = Z_AK + KV_WIDTH
Z_I = Z_AV + KV_WIDTH
Z_END = Z_I + HG_WIDTH
(HK_Q, HK_I, HK_G, HK_KF, HK_KB, HK_GHF, HK_GMF, HK_GHB, HK_GMB) = range(9)
N_HK = 9


def _cparams(sem, flags=None):
    return pltpu.CompilerParams(dimension_semantics=sem, vmem_limit_bytes=VMEM_LIMIT, flags=flags)


def _sigmoid(x):
    return 1.0 / (1.0 + jnp.exp2(x * (-LOG2E)))


def _ada_kernel(ct_ref, w_ref, b_ref, o_ref):
    ct = ct_ref[...]
    ca = ct * _sigmoid(ct)
    w = w_ref[...]
    for b in range(ct.shape[1]):
        o_ref[b:b + 1, :] = jnp.sum(w * ca[:, b:b + 1], axis=0, keepdims=True) + b_ref[...]


def _ada(c, w, b):
    batch, d = c.shape
    n = w.shape[1]
    tn = 1024
    return pl.pallas_call(
        _ada_kernel,
        grid=(n // tn,),
        in_specs=[pl.BlockSpec((d, batch), lambda j: (0, 0)),
                  pl.BlockSpec((d, tn), lambda j: (0, j)),
                  pl.BlockSpec((1, tn), lambda j: (0, j))],
        out_specs=pl.BlockSpec((batch, tn), lambda j: (0, j)),
        out_shape=jax.ShapeDtypeStruct((batch, n), F32),
        compiler_params=_cparams(("arbitrary",)),
        name="ada",
    )(c.T, w, b)


def _t5_bucket_np(rel):
    nb = NUM_BUCKETS // 2
    max_exact = nb // 2
    ret = (rel > 0).astype(np.int32) * nb
    n = np.abs(rel)
    ratio = np.maximum(n, 1).astype(np.float32) / np.float32(max_exact)
    large = max_exact + (np.log(ratio) / np.float32(math.log(MAX_DISTANCE / max_exact))
                         * np.float32(nb - max_exact)).astype(np.int32)
    large = np.minimum(large, nb - 1)
    return ret + np.where(n < max_exact, n, large)


def _bias_kernel(rbt_ref, oh_ref, msk_ref, o_ref):
    b = jnp.dot(rbt_ref[...], oh_ref[...], precision=lax.Precision.HIGHEST,
                preferred_element_type=F32)
    o_ref[...] = (b * LOG2E)[None] + msk_ref[...]


def _bias_tile(rel_bias):
    kcol = np.arange(3 * BLOCK)[None, :]
    rel = (kcol - BLOCK) - np.arange(BLOCK)[:, None]
    bucket = _t5_bucket_np(rel).reshape(-1)
    onehot_t = (np.arange(NUM_BUCKETS)[:, None] == bucket[None, :]).astype(np.float32)
    window = np.abs(rel) <= WINDOW
    valid = np.stack([window & (kcol >= BLOCK), window, window & (kcol < 2 * BLOCK)])
    mask = np.where(valid, 0.0, NEG_INF).astype(np.float32).reshape(3, 1, -1)
    ncol = BLOCK * 3 * BLOCK
    tc = ncol // 4
    out = pl.pallas_call(
        _bias_kernel,
        grid=(ncol // tc,),
        in_specs=[pl.BlockSpec((ATT_Q_HEADS, NUM_BUCKETS), lambda i: (0, 0)),
                  pl.BlockSpec((NUM_BUCKETS, tc), lambda i: (0, i)),
                  pl.BlockSpec((3, 1, tc), lambda i: (0, 0, i))],
        out_specs=pl.BlockSpec((3, ATT_Q_HEADS, tc), lambda i: (0, 0, i)),
        out_shape=jax.ShapeDtypeStruct((3, ATT_Q_HEADS, ncol), F32),
        compiler_params=_cparams(("arbitrary",)),
        name="bias_tile",
    )(rel_bias.astype(F32).T, jnp.asarray(onehot_t), jnp.asarray(mask))
    return out.reshape(3, ATT_Q_HEADS // 2, 2 * BLOCK, 3 * BLOCK)


def _norm_mod(x, ng, sh, sc):
    ms = jnp.mean(x * x, axis=-1, keepdims=True)
    return (x * lax.rsqrt(ms + EPS)) * (ng * (1.0 + sc)) + sh


def _ffn_kernel(*refs, mix, casts):
    if mix:
        hg_ref, at_ref, gm_ref, wm_ref, *refs = refs
    x_ref, sh_ref, sc_ref, gt_ref, ng_ref, win_ref, wout_ref, *refs = refs
    src_refs, (o_ref, *dst_refs) = refs[:len(casts)], refs[len(casts):]
    x = x_ref[0]
    if mix:
        mixed = (jnp.dot(hg_ref[0], wm_ref[:HG_WIDTH, :], preferred_element_type=F32)
                 + jnp.dot(at_ref[0], wm_ref[HG_WIDTH:, :], preferred_element_type=F32))
        x = x + gm_ref[0] * mixed
    h = _norm_mod(x, ng_ref[...], sh_ref[0], sc_ref[0]).astype(BF16)
    gu = jnp.dot(h, win_ref[...], preferred_element_type=F32)
    g = gu[:, :D_FF]
    u = gu[:, D_FF:]
    act = (g * _sigmoid(g) * u).astype(BF16)
    y = jnp.dot(act, wout_ref[...], preferred_element_type=F32)
    o_ref[0] = x + 0.5 * gt_ref[0] * y
    for src_ref, dst_ref, col_map in zip(src_refs, dst_refs, casts):
        for dst0, src0, width in col_map:
            dst_ref[:, dst0:dst0 + width] = src_ref[:, src0:src0 + width].astype(BF16)


def _ffn(x, sh, sc, gt, ng, w_in, w_out, mix=None, casts=()):
    b, s, d = x.shape
    tm = FFN_TM
    n_steps = b * (s // tm)
    vec = pl.BlockSpec((1, 1, d), lambda i, j: (i, 0, 0))
    whole = lambda a: pl.BlockSpec(a.shape, lambda i, j: (0, 0), pipeline_mode=pl.Buffered(1))
    args = [x, sh, sc, gt, ng, w_in, w_out]
    specs = [pl.BlockSpec((1, tm, d), lambda i, j: (i, j, 0)), vec, vec, vec,
             whole(ng), whole(w_in), whole(w_out)]
    if mix is not None:
        o_hg, att, gm, wm = mix
        half = pl.BlockSpec((1, tm, HG_WIDTH), lambda i, j: (i, j, 0))
        args = [o_hg, att, gm, wm] + args
        specs = [half, half, vec, whole(wm)] + specs
    out_specs = [pl.BlockSpec((1, tm, d), lambda i, j: (i, j, 0))]
    out_shape = [jax.ShapeDtypeStruct(x.shape, F32)]
    col_maps = []
    for mat, col_map in casts:
        rows, cols = mat.shape
        assert rows % (n_steps * PACKED_ROWS) == 0 and cols % LANES == 0
        slab = pl.BlockSpec((rows // n_steps, cols), lambda i, j: (i * (s // tm) + j, 0))
        args.append(mat)
        specs.append(slab)
        out_specs.append(slab)
        out_shape.append(jax.ShapeDtypeStruct(mat.shape, BF16))
        col_maps.append(tuple(col_map) if col_map is not None else ((0, 0, cols),))
    out = pl.pallas_call(
        functools.partial(_ffn_kernel, mix=mix is not None, casts=tuple(col_maps)),
        grid=(b, s // tm),
        in_specs=specs,
        out_specs=out_specs,
        out_shape=out_shape,
        compiler_params=_cparams(("arbitrary", "arbitrary")),
        name="ffn_mix" if mix is not None else "ffn",
    )(*args)
    return out if casts else out[0]


def _half_norm(x, gain, lo):
    x2 = x * x
    s_lo = jnp.sum(jnp.where(lo, x2, 0.0), axis=-1, keepdims=True)
    s_hi = jnp.sum(jnp.where(lo, 0.0, x2), axis=-1, keepdims=True)
    ms = jnp.where(lo, s_lo, s_hi) * (1.0 / ATT_HEAD_DIM)
    return x * lax.rsqrt(ms + EPS) * gain


def _mixin_kernel(x_ref, sh_ref, sc_ref, ng_ref, w_ref, lb_ref, qg_ref, kg_ref,
                  hg_ref, aq_ref, ak_ref, av_ref, *, layer):
    h = _norm_mod(x_ref[0], ng_ref[...], sh_ref[0], sc_ref[0]).astype(BF16)

    def proj(c0, c1):
        return jnp.dot(h, w_ref[:, c0:c1], preferred_element_type=F32)

    part_w = HG_WIDTH // 2

    def put_heads(kind, val, part):
        for hh in range(part_w // HG_DIM):
            head = part * (part_w // HG_DIM) + hh
            hg_ref[0, kind * HG_HEADS + head] = val[:, hh * HG_DIM:(hh + 1) * HG_DIM].astype(BF16)

    raw = lb_ref[...]
    mx = jnp.max(raw, axis=1, keepdims=True)
    ex = jnp.exp(raw - mx)
    lb = jnp.sum(ex[:, :layer + 1, :], axis=1) / jnp.sum(ex, axis=1)

    def forget(d, z0, kk, kh, km, part):
        cols = slice(part * part_w, (part + 1) * part_w)
        fr = proj(z0 + cols.start, z0 + cols.stop)
        lbd = lb[d:d + 1, cols]
        f = lbd + (1.0 - lbd) * _sigmoid(fr)
        g2 = jnp.log(f) * LOG2E
        hi = g2.astype(BF16)
        put_heads(kk, 1.0 - f, part)
        put_heads(kh, hi, part)
        put_heads(km, g2 - hi.astype(F32), part)

    def plain(kind, z0, part, silu):
        val = proj(z0 + part * part_w, z0 + (part + 1) * part_w)
        put_heads(kind, val * _sigmoid(val) if silu else val, part)

    lo = lax.broadcasted_iota(jnp.int32, (h.shape[0], LANES), 1) < LANES // 2

    def queries(part):
        aq = proj(Z_AQ + part * part_w, Z_AQ + (part + 1) * part_w)
        for m in range(part_w // LANES):
            cols = slice(m * LANES, (m + 1) * LANES)
            aq_ref[0, :, part * part_w + m * LANES:part * part_w + (m + 1) * LANES] = (
                _half_norm(aq[:, cols], qg_ref[...], lo).astype(BF16))

    def keys_values():
        akv = proj(Z_AK, Z_I)
        ak_ref[0] = _half_norm(akv[:, :KV_WIDTH], kg_ref[...], lo).T.astype(BF16)
        av_ref[0] = akv[:, KV_WIDTH:].astype(BF16)

    forget(0, Z_FF, HK_KF, HK_GHF, HK_GMF, 0)
    plain(HK_I, Z_I, 0, False)
    forget(0, Z_FF, HK_KF, HK_GHF, HK_GMF, 1)
    plain(HK_I, Z_I, 1, False)
    forget(1, Z_FB, HK_KB, HK_GHB, HK_GMB, 0)
    keys_values()
    forget(1, Z_FB, HK_KB, HK_GHB, HK_GMB, 1)
    plain(HK_Q, Z_Q, 0, True)
    queries(0)
    plain(HK_Q, Z_Q, 1, True)
    queries(1)
    plain(HK_G, Z_G, 0, True)
    plain(HK_G, Z_G, 1, True)


def _mixin(x, sh, sc, ng, w, hgrn_lb, qg, kg, layer):
    b, s, d = x.shape
    tm = MIX_TM
    vec = pl.BlockSpec((1, 1, d), lambda i, j: (i, 0, 0))
    row = lambda n: pl.BlockSpec((1, n), lambda i, j: (0, 0))
    rows3 = lambda n: pl.BlockSpec((1, tm, n), lambda i, j: (i, j, 0))
    n_hg = N_HK * HG_HEADS
    return pl.pallas_call(
        functools.partial(_mixin_kernel, layer=layer),
        grid=(b, s // tm),
        in_specs=[pl.BlockSpec((1, tm, d), lambda i, j: (i, j, 0)),
                  vec, vec, row(d),
                  pl.BlockSpec(w.shape, lambda i, j: (0, 0)),
                  pl.BlockSpec(hgrn_lb.shape, lambda i, j: (0, 0, 0)),
                  row(LANES), row(LANES)],
        out_specs=[pl.BlockSpec((1, n_hg, tm, HG_DIM), lambda i, j: (i, 0, j, 0)),
                   rows3(ATT_WIDTH),
                   pl.BlockSpec((1, KV_WIDTH, tm), lambda i, j: (i, 0, j)),
                   rows3(KV_WIDTH)],
        out_shape=[jax.ShapeDtypeStruct((b, n_hg, s, HG_DIM), BF16),
                   jax.ShapeDtypeStruct((b, s, ATT_WIDTH), BF16),
                   jax.ShapeDtypeStruct((b, KV_WIDTH, s), BF16),
                   jax.ShapeDtypeStruct((b, s, KV_WIDTH), BF16)],
        compiler_params=_cparams(("arbitrary", "arbitrary")),
        name="mix_in",
    )(x, sh, sc, ng, w, hgrn_lb, qg, kg)


SMALL_W = (1, 2, 4, 8)
GROUP = PACKED_ROWS
FINISH_CHUNKS = 8
PAIRS_PER_BODY = 8


def _hgrn_consts(c):
    t = np.arange(c)[:, None]
    s = np.arange(c)[None, :]
    x = t ^ s
    lev = np.where(x > 0, np.floor(np.log2(np.maximum(x, 1))), -1).astype(np.int32)
    lev_f = np.where(t > s, lev, np.where(t == s, -1, -2)).astype(np.int32)
    lev_b = lev_f.T.copy()

    def exponent_rows(w, rev):
        m = np.zeros((GROUP, GROUP), np.float32)
        for r in range(GROUP):
            b0 = (r // (2 * w)) * 2 * w
            if not rev:
                ref = b0 + w - 1
                lo_u, hi_u = (ref + 1, r) if r > ref else (r + 1, ref)
            else:
                ref = b0 + w
                lo_u, hi_u = (r, ref - 1) if r < ref else (ref, r - 1)
            m[r, lo_u:hi_u + 1] = 1.0
        return m

    cums, smalls = [], []
    for rev in (False, True):
        cum = ((s >= t) if rev else (s <= t)).astype(np.float32)
        cums.append(np.concatenate([cum, cum], axis=1))
        rows = np.concatenate([exponent_rows(w, rev) for w in SMALL_W])
        smalls.append(np.concatenate([rows, rows], axis=1))
    return (jnp.asarray(np.stack([lev_f, lev_b])),
            jnp.asarray(np.stack(cums), dtype=BF16),
            jnp.asarray(np.stack(smalls), dtype=BF16))


def _neg_abs(x):
    bits = lax.bitcast_convert_type(x, jnp.uint32) | jnp.uint32(0x80000000)
    return lax.bitcast_convert_type(bits, F32)


def _ref_rows(g_cum, w, rev, c):
    idx = w if rev else w - 1
    g3 = g_cum.reshape(c // (2 * w), 2 * w, LANES)
    return jnp.broadcast_to(g3[:, idx:idx + 1, :], g3.shape).reshape(c, LANES)


def _hgrn_stage_a(tcum, tsmall, gh, gm, gc_ref, ge_ref, c):
    hm = jnp.concatenate([gh, gm], axis=0)
    half = c // 2
    for r in range(2):
        rows = slice(r * half, (r + 1) * half)
        gc_ref[rows, :] = jnp.dot(tcum[rows], hm, preferred_element_type=F32)
        yield
    per_tile = MXU_COLS // LANES
    for t in range(c // GROUP // per_tile):
        groups = [slice(g * GROUP, (g + 1) * GROUP) for g in range(t * per_tile, (t + 1) * per_tile)]
        wide = jnp.concatenate([jnp.concatenate([gh[g] for g in groups], axis=1),
                                jnp.concatenate([gm[g] for g in groups], axis=1)], axis=0)
        ge_ref[:, t * MXU_COLS:(t + 1) * MXU_COLS] = jnp.dot(tsmall, wide,
                                                              preferred_element_type=F32)
        yield


def _hgrn_stage_b(q, k, v, gc_ref, ge_ref, lev, st_ref, a_ref, ab_ref, oi_ref, rev, c):
    tn = (((0,), (0,)), ((), ()))
    g_cum = gc_ref[...]
    g_last = g_cum[0:1, :] if rev else g_cum[c - 1:c, :]
    qf, kf = q.astype(F32), k.astype(F32)

    st = st_ref[...]
    qi = (qf * jnp.exp2(g_cum)).astype(BF16)
    oi_ref[...] = jnp.dot(qi, st.T.astype(BF16), preferred_element_type=F32)
    kd = (kf * jnp.exp2(g_last - g_cum)).astype(BF16)
    u_t = lax.dot_general(v, kd, tn, preferred_element_type=F32)
    st_ref[...] = jnp.exp2(g_last) * st + u_t
    yield

    w = c // 2
    while w >= GROUP:
        x = jnp.exp2(_neg_abs(g_cum - _ref_rows(g_cum, w, rev, c)))
        q_rows, k_rows = [], []
        for b in range(c // (2 * w)):
            lo_half = slice(b * 2 * w, b * 2 * w + w)
            hi_half = slice(b * 2 * w + w, (b + 1) * 2 * w)
            qs = lo_half if rev else hi_half
            q_rows.append(qf[qs] * x[qs])
            k_rows.extend([kf[lo_half], kf[hi_half] * x[hi_half]] if rev
                          else [kf[lo_half] * x[lo_half], kf[hi_half]])
        p = jnp.dot(jnp.concatenate(q_rows, axis=0).astype(BF16),
                    jnp.concatenate(k_rows, axis=0).T.astype(BF16),
                    preferred_element_type=F32)
        for b in range(c // (2 * w)):
            lo_half = slice(b * 2 * w, b * 2 * w + w)
            hi_half = slice(b * 2 * w + w, (b + 1) * 2 * w)
            qs, ks = (lo_half, hi_half) if rev else (hi_half, lo_half)
            a_ref[qs, ks] = p[b * w:(b + 1) * w, ks]
        w //= 2
        yield

    a = jnp.where(lev == -1, jnp.dot(q, kf.T.astype(BF16), preferred_element_type=F32), 0.0)
    for li in range(len(SMALL_W)):
        e = jnp.concatenate([ge_ref[li * GROUP:(li + 1) * GROUP, g * LANES:(g + 1) * LANES]
                             for g in range(c // GROUP)], axis=0)
        x = jnp.exp2(e)
        p = jnp.dot((qf * x).astype(BF16), (kf * x).T.astype(BF16),
                    preferred_element_type=F32)
        a = jnp.where(lev == li, p, a)
        yield
    for b in range(c // GROUP):
        blk = slice(b * GROUP, (b + 1) * GROUP)
        a_ref[blk, blk] = a[blk, blk]
    ab_ref[...] = a_ref[...].astype(BF16)
    yield


def _interleave(main, fill):
    for _ in main:
        next(fill, None)
    for _ in fill:
        pass


def _hgrn_kernel(q_ref, i_ref, sg_ref, kf_ref, kb_ref, ghf_ref, gmf_ref, ghb_ref, gmb_ref,
                 ng_ref, lev_ref, tc_ref, ts_ref, o_ref,
                 acc_ref, st_ref, a_ref, ab_ref, oi_ref, gc_ref, ge_ref, *, c, nc):
    for ref in (acc_ref, st_ref, a_ref, ab_ref, oi_ref):
        ref[...] = jnp.zeros_like(ref)
    k_refs = (kf_ref, kb_ref)
    g_refs = ((ghf_ref, gmf_ref), (ghb_ref, gmb_ref))

    def rows_of(p, d):
        j = p if d == 0 else nc - 1 - p
        return pl.ds(pl.multiple_of(j * c, c), c)

    def stage_a(p, slot, d):
        rows = rows_of(p, d)
        return _hgrn_stage_a(tc_ref[d], ts_ref[d], g_refs[d][0][0, 0, rows, :],
                             g_refs[d][1][0, 0, rows, :], gc_ref.at[slot, d], ge_ref.at[slot, d], c)

    def stage_t(p, d):
        rows = rows_of(p, d)
        acc_ref[rows, :] += oi_ref[d] + jnp.dot(ab_ref[d], i_ref[0, 0, rows, :],
                                                preferred_element_type=F32)

    def stage_b(p, slot, d):
        rows = rows_of(p, d)
        return _hgrn_stage_b(q_ref[0, 0, rows, :], k_refs[d][0, 0, rows, :], i_ref[0, 0, rows, :],
                             gc_ref.at[slot, d], ge_ref.at[slot, d], lev_ref[d], st_ref.at[d],
                             a_ref.at[d], ab_ref.at[d], oi_ref.at[d], d == 1, c)

    def pair(p, slot):
        p_next, p_prev = jnp.minimum(p + 1, nc - 1), jnp.maximum(p - 1, 0)
        for d in range(2):
            stage_t(p_prev, d)
            _interleave(stage_b(p, slot, d), stage_a(p_next, 1 - slot, d))

    def pairs(i, carry):
        for u in range(PAIRS_PER_BODY):
            pair(PAIRS_PER_BODY * i + u, u % 2)
        return carry

    for d in range(2):
        for _ in stage_a(0, 0, d):
            pass
    lax.fori_loop(0, nc // PAIRS_PER_BODY, pairs, 0)
    for d in range(2):
        stage_t(nc - 1, d)

    ng = ng_ref[...]
    rows_fin = FINISH_CHUNKS * c

    def finish(j, carry):
        rows = pl.ds(pl.multiple_of(j * rows_fin, rows_fin), rows_fin)
        o = acc_ref[rows, :]
        ms = jnp.mean(o * o, axis=-1, keepdims=True)
        o = o * lax.rsqrt(ms + EPS) * ng
        o_ref[0, rows, :] = (o * sg_ref[0, 0, rows, :].astype(F32)).astype(o_ref.dtype)
        return carry

    lax.fori_loop(0, nc // FINISH_CHUNKS, finish, 0)


def _hgrn(hg, norm_g):
    b, _, s, _ = hg.shape
    c = HG_CHUNK
    nc = s // c
    assert PAIRS_PER_BODY % 2 == 0 and nc % PAIRS_PER_BODY == 0 and c % (2 * GROUP) == 0
    lev, tcum, tsmall = _hgrn_consts(c)
    n_small = len(SMALL_W) * GROUP

    def kind(k):
        return pl.BlockSpec((1, 1, s, HG_DIM), lambda i, h: (i, k * HG_HEADS + h, 0, 0))

    const3 = lambda a: pl.BlockSpec(a.shape, lambda i, h: (0, 0, 0))
    return pl.pallas_call(
        functools.partial(_hgrn_kernel, c=c, nc=nc),
        grid=(b, HG_HEADS),
        in_specs=[kind(HK_Q), kind(HK_I), kind(HK_G), kind(HK_KF), kind(HK_KB),
                  kind(HK_GHF), kind(HK_GMF), kind(HK_GHB), kind(HK_GMB),
                  pl.BlockSpec((1, LANES), lambda i, h: (0, h)),
                  const3(lev), const3(tcum), const3(tsmall)],
        out_specs=pl.BlockSpec((1, s, LANES), lambda i, h: (i, 0, h)),
        out_shape=jax.ShapeDtypeStruct((b, s, HG_WIDTH), BF16),
        scratch_shapes=[pltpu.VMEM((s, HG_DIM), F32),
                        pltpu.VMEM((2, HG_DIM, HG_DIM), F32),
                        pltpu.VMEM((2, c, c), F32),
                        pltpu.VMEM((2, c, c), BF16),
                        pltpu.VMEM((2, c, HG_DIM), F32),
                        pltpu.VMEM((2, 2, c, LANES), F32),
                        pltpu.VMEM((2, 2, n_small, (c // GROUP) * LANES), F32)],
        compiler_params=_cparams(("arbitrary", "arbitrary")),
        name="hgrn",
    )(hg, hg, hg, hg, hg, hg, hg, hg, hg, norm_g, lev, tcum, tsmall)


def _attn_kernel(sink_ref, q_ref, kp_ref, ko_ref, kn_ref, vp_ref, vo_ref, vn_ref,
                 bias_ref, o_ref, lg_ref, *, n_steps):
    n = pl.program_id(1)
    half = LANES // 2
    nk = 3 * BLOCK
    kt = jnp.concatenate([kp_ref[0], ko_ref[0], kn_ref[0]], axis=1)
    vb = jnp.concatenate([vp_ref[0], vo_ref[0], vn_ref[0]], axis=0)
    lo = lax.broadcasted_iota(jnp.int32, (BLOCK, LANES), 1) < half
    lo_v = lax.broadcasted_iota(jnp.int32, vb.shape, 1) < half
    zero_v = jnp.zeros_like(vb)
    zero_k = jnp.zeros((half, nk), kt.dtype)
    kts = (kt[:half], kt[half:])
    v0lo, v1hi = jnp.where(lo_v, vb, zero_v), jnp.where(lo_v, zero_v, vb)
    v_lo = (v0lo, pltpu.roll(v1hi, half, axis=1))
    v_hi = (pltpu.roll(v0lo, half, axis=1), v1hi)

    def logits(i):
        keys = slice(i * BLOCK, i * BLOCK + nk)
        rows = slice(i * BLOCK, (i + 1) * BLOCK)
        for m in range(ATT_Q_HEADS // 2):
            kj = kts[(2 * m) // ATT_GROUP][:, keys]
            k_both = jnp.concatenate([jnp.concatenate([kj, zero_k], axis=0),
                                      jnp.concatenate([zero_k, kj], axis=0)], axis=1)
            lg_ref[i, m] = jnp.dot(q_ref[0, rows, m * LANES:(m + 1) * LANES], k_both,
                                   preferred_element_type=F32)

    def softmax_pv(i):
        keys = slice(i * BLOCK, i * BLOCK + nk)
        rows = slice(i * BLOCK, (i + 1) * BLOCK)
        edge = 1
        if i == ATT_SUB - 1:
            edge = jnp.where(n == n_steps - 1, 2, edge)
        if i == 0:
            edge = jnp.where(n == 0, 0, edge)
        for m in range(ATT_Q_HEADS // 2):
            j = (2 * m) // ATT_GROUP
            ps, rdens = [], []
            for hh in range(2):
                lg = (lg_ref[i, m, :, hh * nk:(hh + 1) * nk]
                      + bias_ref[edge, m, hh * BLOCK:(hh + 1) * BLOCK, :])
                sink = sink_ref[2 * m + hh]
                mx = jnp.maximum(jnp.max(lg, axis=-1, keepdims=True), sink)
                p = jnp.exp2(lg - mx)
                rdens.append(1.0 / (jnp.sum(p, axis=-1, keepdims=True) + jnp.exp2(sink - mx)))
                ps.append(p.astype(BF16))
            v_both = jnp.concatenate([v_lo[j][keys], v_hi[j][keys]], axis=0)
            o = jnp.dot(jnp.concatenate(ps, axis=1), v_both, preferred_element_type=F32)
            o_ref[0, rows, m * LANES:(m + 1) * LANES] = (
                o * jnp.where(lo, rdens[0], rdens[1])).astype(o_ref.dtype)

    logits(0)
    for i in range(ATT_SUB):
        if i + 1 < ATT_SUB:
            logits(i + 1)
        softmax_pv(i)


def _attn(aq, ak, av, sink2, bias):
    b, s, _ = aq.shape
    rows = ATT_SUB * BLOCK
    n_steps = s // rows
    nb = s // BLOCK
    assert s % rows == 0 and nb >= 2

    def edge_block(n, shift):
        return jnp.clip(n * ATT_SUB + shift, 0, nb - 1)

    kt_edge = lambda shift: pl.BlockSpec((1, KV_WIDTH, BLOCK), lambda i, n: (i, 0, edge_block(n, shift)))
    v_edge = lambda shift: pl.BlockSpec((1, BLOCK, KV_WIDTH), lambda i, n: (i, edge_block(n, shift), 0))
    return pl.pallas_call(
        functools.partial(_attn_kernel, n_steps=n_steps),
        grid=(b, n_steps),
        in_specs=[pl.BlockSpec(memory_space=pltpu.SMEM),
                  pl.BlockSpec((1, rows, ATT_WIDTH), lambda i, n: (i, n, 0)),
                  kt_edge(-1), pl.BlockSpec((1, KV_WIDTH, rows), lambda i, n: (i, 0, n)), kt_edge(ATT_SUB),
                  v_edge(-1), pl.BlockSpec((1, rows, KV_WIDTH), lambda i, n: (i, n, 0)), v_edge(ATT_SUB),
                  pl.BlockSpec(bias.shape, lambda i, n: (0, 0, 0, 0))],
        out_specs=pl.BlockSpec((1, rows, ATT_WIDTH), lambda i, n: (i, n, 0)),
        out_shape=jax.ShapeDtypeStruct((b, s, ATT_WIDTH), BF16),
        scratch_shapes=[pltpu.VMEM((ATT_SUB, ATT_Q_HEADS // 2, BLOCK, 2 * 3 * BLOCK), F32)],
        compiler_params=_cparams(("arbitrary", "arbitrary")),
        name="attn",
    )(sink2, aq, ak, ak, ak, av, av, av, bias)


def _mixin_col_map():
    hw = HG_WIDTH
    att = ATT_WIDTH + 2 * KV_WIDTH
    return ((Z_FF, hw, 2 * hw), (Z_Q, 0, hw), (Z_G, 4 * hw, hw), (Z_AQ, 5 * hw, att),
            (Z_I, 3 * hw, hw))


def kernel(x, c, w_ada, b_ada, norm_g, w_ffn1_in, w_ffn1_out, w_ffn2_in, w_ffn2_out,
           w_mix_in, w_mix_out, hgrn_lb, hgrn_norm_g, qk_norm_g, attn_sink, rel_bias):
    b, s, d = x.shape
    depth = w_ada.shape[0]
    bias = _bias_tile(rel_bias)
    for l in range(depth):
        mods = _ada(c.astype(F32), w_ada[l], b_ada[l][None, :])
        sh1, sc1, g1, sh2, sc2, g2, sh3, sc3, g3 = [
            mods[:, i * d:(i + 1) * d][:, None, :] for i in range(N_MOD)]
        f_ff = w_ffn2_out.shape[1]
        x, w2_in, w2_out, w_mi, w_mo = _ffn(
            x, sh1, sc1, g1, norm_g[l, 0][None, :],
            w_ffn1_in[l].astype(BF16), w_ffn1_out[l].astype(BF16),
            casts=((w_ffn2_in[l], None), (w_ffn2_out[l].reshape(d, f_ff), None),
                   (w_mix_in[l], _mixin_col_map()), (w_mix_out[l], None)))
        qg = (jnp.tile(qk_norm_g[l, 0], 2) * (LOG2E / math.sqrt(ATT_HEAD_DIM)))[None, :]
        kg = jnp.tile(qk_norm_g[l, 1], 2)[None, :]
        hg, aq, ak, av = _mixin(x, sh2, sc2, norm_g[l, 1][None, :], w_mi, hgrn_lb, qg, kg, l)
        o_hg = _hgrn(hg, hgrn_norm_g[l][None, :])
        att = _attn(aq, ak, av, attn_sink[l] * LOG2E, bias)
        x = _ffn(x, sh3, sc3, g3, norm_g[l, 2][None, :],
                 w2_in, w2_out.reshape(f_ff, d), mix=(o_hg, att, g2, w_mo))
    return x
```

```python
import functools
import math

import numpy as np
import jax
import jax.numpy as jnp
from jax import lax
from jax.experimental import pallas as pl
from jax.experimental.pallas import tpu as pltpu

F32 = jnp.float32
BF16 = jnp.bfloat16

D_MODEL = 1024
HG_HEADS = 4
HG_DIM = 128
HG_WIDTH = HG_HEADS * HG_DIM
ATT_Q_HEADS = 8
ATT_KV_HEADS = 2
ATT_HEAD_DIM = 64
ATT_GROUP = ATT_Q_HEADS // ATT_KV_HEADS
ATT_WIDTH = ATT_Q_HEADS * ATT_HEAD_DIM
KV_WIDTH = ATT_KV_HEADS * ATT_HEAD_DIM
WINDOW = 128
BLOCK = 128
NUM_BUCKETS = 32
MAX_DISTANCE = 128
D_FF = 2816
N_MOD = 9
EPS = 1e-6

LANES = 128
SUBLANES = 8
PACKED_ROWS = 16
MXU_COLS = 256
VMEM_LIMIT = 56 * 1024 * 1024

FFN_TM = 512
MIX_TM = 256
ATT_SUB = 1
HG_CHUNK = 128
NEG_INF = float("-inf")
LOG2E = 1.0 / math.log(2.0)

Z_FF, Z_FB, Z_Q, Z_G, Z_AQ = (n * HG_WIDTH for n in range(5))
Z_AK = Z_AQ + ATT_WIDTH
Z_AV = Z_AK + KV_WIDTH
Z_I = Z_AV + KV_WIDTH
Z_END = Z_I + HG_WIDTH
(HK_Q, HK_I, HK_G, HK_KF, HK_KB, HK_GHF, HK_GMF, HK_GHB, HK_GMB) = range(9)
N_HK = 9


def _cparams(sem, flags=None):
    return pltpu.CompilerParams(dimension_semantics=sem, vmem_limit_bytes=VMEM_LIMIT, flags=flags)


def _sigmoid(x):
    return 1.0 / (1.0 + jnp.exp2(x * (-LOG2E)))


def _ada_kernel(ct_ref, w_ref, b_ref, o_ref):
    ct = ct_ref[...]
    ca = ct * _sigmoid(ct)
    w = w_ref[...]
    for b in range(ct.shape[1]):
        o_ref[b:b + 1, :] = jnp.sum(w * ca[:, b:b + 1], axis=0, keepdims=True) + b_ref[...]


def _ada(c, w, b):
    batch, d = c.shape
    n = w.shape[1]
    tn = 1024
    return pl.pallas_call(
        _ada_kernel,
        grid=(n // tn,),
        in_specs=[pl.BlockSpec((d, batch), lambda j: (0, 0)),
                  pl.BlockSpec((d, tn), lambda j: (0, j)),
                  pl.BlockSpec((1, tn), lambda j: (0, j))],
        out_specs=pl.BlockSpec((batch, tn), lambda j: (0, j)),
        out_shape=jax.ShapeDtypeStruct((batch, n), F32),
        compiler_params=_cparams(("arbitrary",)),
        name="ada",
    )(c.T, w, b)


def _t5_bucket_np(rel):
    nb = NUM_BUCKETS // 2
    max_exact = nb // 2
    ret = (rel > 0).astype(np.int32) * nb
    n = np.abs(rel)
    ratio = np.maximum(n, 1).astype(np.float32) / np.float32(max_exact)
    large = max_exact + (np.log(ratio) / np.float32(math.log(MAX_DISTANCE / max_exact))
                         * np.float32(nb - max_exact)).astype(np.int32)
    large = np.minimum(large, nb - 1)
    return ret + np.where(n < max_exact, n, large)


def _bias_kernel(rbt_ref, oh_ref, msk_ref, o_ref):
    b = jnp.dot(rbt_ref[...], oh_ref[...], precision=lax.Precision.HIGHEST,
                preferred_element_type=F32)
    o_ref[...] = (b * LOG2E)[None] + msk_ref[...]


def _bias_tile(rel_bias):
    kcol = np.arange(3 * BLOCK)[None, :]
    rel = (kcol - BLOCK) - np.arange(BLOCK)[:, None]
    bucket = _t5_bucket_np(rel).reshape(-1)
    onehot_t = (np.arange(NUM_BUCKETS)[:, None] == bucket[None, :]).astype(np.float32)
    window = np.abs(rel) <= WINDOW
    valid = np.stack([window & (kcol >= BLOCK), window, window & (kcol < 2 * BLOCK)])
    mask = np.where(valid, 0.0, NEG_INF).astype(np.float32).reshape(3, 1, -1)
    ncol = BLOCK * 3 * BLOCK
    tc = ncol // 4
    out = pl.pallas_call(
        _bias_kernel,
        grid=(ncol // tc,),
        in_specs=[pl.BlockSpec((ATT_Q_HEADS, NUM_BUCKETS), lambda i: (0, 0)),
                  pl.BlockSpec((NUM_BUCKETS, tc), lambda i: (0, i)),
                  pl.BlockSpec((3, 1, tc), lambda i: (0, 0, i))],
        out_specs=pl.BlockSpec((3, ATT_Q_HEADS, tc), lambda i: (0, 0, i)),
        out_shape=jax.ShapeDtypeStruct((3, ATT_Q_HEADS, ncol), F32),
        compiler_params=_cparams(("arbitrary",)),
        name="bias_tile",
    )(rel_bias.astype(F32).T, jnp.asarray(onehot_t), jnp.asarray(mask))
    return out.reshape(3, ATT_Q_HEADS // 2, 2 * BLOCK, 3 * BLOCK)


def _norm_mod(x, ng, sh, sc):
    ms = jnp.mean(x * x, axis=-1, keepdims=True)
    return (x * lax.rsqrt(ms + EPS)) * (ng * (1.0 + sc)) + sh


def _ffn_kernel(*refs, mix, casts):
    if mix:
        hg_ref, at_ref, gm_ref, wm_ref, *refs = refs
    x_ref, sh_ref, sc_ref, gt_ref, ng_ref, win_ref, wout_ref, *refs = refs
    src_refs, (o_ref, *dst_refs) = refs[:len(casts)], refs[len(casts):]
    x = x_ref[0]
    if mix:
        mixed = (jnp.dot(hg_ref[0], wm_ref[:HG_WIDTH, :], preferred_element_type=F32)
                 + jnp.dot(at_ref[0], wm_ref[HG_WIDTH:, :], preferred_element_type=F32))
        x = x + gm_ref[0] * mixed
    h = _norm_mod(x, ng_ref[...], sh_ref[0], sc_ref[0]).astype(BF16)
    gu = jnp.dot(h, win_ref[...], preferred_element_type=F32)
    g = gu[:, :D_FF]
    u = gu[:, D_FF:]
    act = (g * _sigmoid(g) * u).astype(BF16)
    y = jnp.dot(act, wout_ref[...], preferred_element_type=F32)
    o_ref[0] = x + 0.5 * gt_ref[0] * y
    for src_ref, dst_ref, col_map in zip(src_refs, dst_refs, casts):
        for dst0, src0, width in col_map:
            dst_ref[:, dst0:dst0 + width] = src_ref[0, :, src0:src0 + width].astype(BF16)


def _ffn(x, sh, sc, gt, ng, w_in, w_out, mix=None, casts=()):
    b, s, d = x.shape
    tm = FFN_TM
    n_steps = b * (s // tm)
    vec = pl.BlockSpec((1, 1, d), lambda i, j: (i, 0, 0))
    whole = lambda a: pl.BlockSpec(a.shape, lambda i, j: (0, 0), pipeline_mode=pl.Buffered(1))
    args = [x, sh, sc, gt, ng, w_in, w_out]
    specs = [pl.BlockSpec((1, tm, d), lambda i, j: (i, j, 0)), vec, vec, vec,
             whole(ng), whole(w_in), whole(w_out)]
    if mix is not None:
        o_hg, att, gm, wm = mix
        half = pl.BlockSpec((1, tm, HG_WIDTH), lambda i, j: (i, j, 0))
        args = [o_hg, att, gm, wm] + args
        specs = [half, half, vec, whole(wm)] + specs
    out_specs = [pl.BlockSpec((1, tm, d), lambda i, j: (i, j, 0))]
    out_shape = [jax.ShapeDtypeStruct(x.shape, F32)]
    col_maps = []
    for mats, layer, col_map, dup in casts:
        _, rows, cols = mats.shape
        assert n_steps % dup == 0
        n_slabs = n_steps // dup
        assert rows % (n_slabs * PACKED_ROWS) == 0 and cols % LANES == 0
        slab = rows // n_slabs
        args.append(mats)
        specs.append(pl.BlockSpec(
            (1, slab, cols), lambda i, j, layer=layer, dup=dup: (layer, (i * (s // tm) + j) // dup, 0)))
        out_specs.append(pl.BlockSpec(
            (slab, cols), lambda i, j, dup=dup: ((i * (s // tm) + j) // dup, 0)))
        out_shape.append(jax.ShapeDtypeStruct((rows, cols), BF16))
        col_maps.append(tuple(col_map) if col_map is not None else ((0, 0, cols),))
    out = pl.pallas_call(
        functools.partial(_ffn_kernel, mix=mix is not None, casts=tuple(col_maps)),
        grid=(b, s // tm),
        in_specs=specs,
        out_specs=out_specs,
        out_shape=out_shape,
        compiler_params=_cparams(("arbitrary", "arbitrary")),
        name="ffn_mix" if mix is not None else "ffn",
    )(*args)
    return out if casts else out[0]


def _half_norm(x, gain, lo):
    x2 = x * x
    s_lo = jnp.sum(jnp.where(lo, x2, 0.0), axis=-1, keepdims=True)
    s_hi = jnp.sum(jnp.where(lo, 0.0, x2), axis=-1, keepdims=True)
    ms = jnp.where(lo, s_lo, s_hi) * (1.0 / ATT_HEAD_DIM)
    return x * lax.rsqrt(ms + EPS) * gain


def _mixin_kernel(x_ref, sh_ref, sc_ref, ng_ref, w_ref, lb_ref, qg_ref, kg_ref,
                  hg_ref, aq_ref, ak_ref, av_ref, *, layer):
    h = _norm_mod(x_ref[0], ng_ref[...], sh_ref[0], sc_ref[0]).astype(BF16)

    def proj(c0, c1):
        return jnp.dot(h, w_ref[:, c0:c1], preferred_element_type=F32)

    part_w = HG_WIDTH // 2

    def put_heads(kind, val, part):
        for hh in range(part_w // HG_DIM):
            head = part * (part_w // HG_DIM) + hh
            hg_ref[0, kind * HG_HEADS + head] = val[:, hh * HG_DIM:(hh + 1) * HG_DIM].astype(BF16)

    raw = lb_ref[...]
    mx = jnp.max(raw, axis=1, keepdims=True)
    ex = jnp.exp(raw - mx)
    lb = jnp.sum(ex[:, :layer + 1, :], axis=1) / jnp.sum(ex, axis=1)

    def forget(d, z0, kk, kh, km, part):
        cols = slice(part * part_w, (part + 1) * part_w)
        fr = proj(z0 + cols.start, z0 + cols.stop)
        lbd = lb[d:d + 1, cols]
        f = lbd + (1.0 - lbd) * _sigmoid(fr)
        g2 = jnp.log(f) * LOG2E
        hi = g2.astype(BF16)
        put_heads(kk, 1.0 - f, part)
        put_heads(kh, hi, part)
        put_heads(km, g2 - hi.astype(F32), part)

    def plain(kind, z0, part, silu):
        val = proj(z0 + part * part_w, z0 + (part + 1) * part_w)
        put_heads(kind, val * _sigmoid(val) if silu else val, part)

    lo = lax.broadcasted_iota(jnp.int32, (h.shape[0], LANES), 1) < LANES // 2

    def queries(part):
        aq = proj(Z_AQ + part * part_w, Z_AQ + (part + 1) * part_w)
        for m in range(part_w // LANES):
            cols = slice(m * LANES, (m + 1) * LANES)
            aq_ref[0, :, part * part_w + m * LANES:part * part_w + (m + 1) * LANES] = (
                _half_norm(aq[:, cols], qg_ref[...], lo).astype(BF16))

    def keys_values():
        akv = proj(Z_AK, Z_I)
        ak_ref[0] = _half_norm(akv[:, :KV_WIDTH], kg_ref[...], lo).T.astype(BF16)
        av_ref[0] = akv[:, KV_WIDTH:].astype(BF16)

    forget(0, Z_FF, HK_KF, HK_GHF, HK_GMF, 0)
    plain(HK_I, Z_I, 0, False)
    forget(0, Z_FF, HK_KF, HK_GHF, HK_GMF, 1)
    plain(HK_I, Z_I, 1, False)
    forget(1, Z_FB, HK_KB, HK_GHB, HK_GMB, 0)
    keys_values()
    forget(1, Z_FB, HK_KB, HK_GHB, HK_GMB, 1)
    plain(HK_Q, Z_Q, 0, True)
    queries(0)
    plain(HK_Q, Z_Q, 1, True)
    queries(1)
    plain(HK_G, Z_G, 0, True)
    plain(HK_G, Z_G, 1, True)


def _mixin(x, sh, sc, ng, w, hgrn_lb, qg, kg, layer):
    b, s, d = x.shape
    tm = MIX_TM
    vec = pl.BlockSpec((1, 1, d), lambda i, j: (i, 0, 0))
    row = lambda n: pl.BlockSpec((1, n), lambda i, j: (0, 0))
    rows3 = lambda n: pl.BlockSpec((1, tm, n), lambda i, j: (i, j, 0))
    n_hg = N_HK * HG_HEADS
    return pl.pallas_call(
        functools.partial(_mixin_kernel, layer=layer),
        grid=(b, s // tm),
        in_specs=[pl.BlockSpec((1, tm, d), lambda i, j: (i, j, 0)),
                  vec, vec, row(d),
                  pl.BlockSpec(w.shape, lambda i, j: (0, 0)),
                  pl.BlockSpec(hgrn_lb.shape, lambda i, j: (0, 0, 0)),
                  row(LANES), row(LANES)],
        out_specs=[pl.BlockSpec((1, n_hg, tm, HG_DIM), lambda i, j: (i, 0, j, 0)),
                   rows3(ATT_WIDTH),
                   pl.BlockSpec((1, KV_WIDTH, tm), lambda i, j: (i, 0, j)),
                   rows3(KV_WIDTH)],
        out_shape=[jax.ShapeDtypeStruct((b, n_hg, s, HG_DIM), BF16),
                   jax.ShapeDtypeStruct((b, s, ATT_WIDTH), BF16),
                   jax.ShapeDtypeStruct((b, KV_WIDTH, s), BF16),
                   jax.ShapeDtypeStruct((b, s, KV_WIDTH), BF16)],
        compiler_params=_cparams(("arbitrary", "arbitrary")),
        name="mix_in",
    )(x, sh, sc, ng, w, hgrn_lb, qg, kg)


SMALL_W = (1, 2, 4, 8)
GROUP = PACKED_ROWS
FINISH_CHUNKS = 8
PAIRS_PER_BODY = 8


def _hgrn_consts(c):
    t = np.arange(c)[:, None]
    s = np.arange(c)[None, :]
    x = t ^ s
    lev = np.where(x > 0, np.floor(np.log2(np.maximum(x, 1))), -1).astype(np.int32)
    lev_f = np.where(t > s, lev, np.where(t == s, -1, -2)).astype(np.int32)
    lev_b = lev_f.T.copy()

    def exponent_rows(w, rev):
        m = np.zeros((GROUP, GROUP), np.float32)
        for r in range(GROUP):
            b0 = (r // (2 * w)) * 2 * w
            if not rev:
                ref = b0 + w - 1
                lo_u, hi_u = (ref + 1, r) if r > ref else (r + 1, ref)
            else:
                ref = b0 + w
                lo_u, hi_u = (r, ref - 1) if r < ref else (ref, r - 1)
            m[r, lo_u:hi_u + 1] = 1.0
        return m

    cums, smalls = [], []
    for rev in (False, True):
        cum = ((s >= t) if rev else (s <= t)).astype(np.float32)
        cums.append(np.concatenate([cum, cum], axis=1))
        rows = np.concatenate([exponent_rows(w, rev) for w in SMALL_W])
        smalls.append(np.concatenate([rows, rows], axis=1))
    return (jnp.asarray(np.stack([lev_f, lev_b])),
            jnp.asarray(np.stack(cums), dtype=BF16),
            jnp.asarray(np.stack(smalls), dtype=BF16))


def _neg_abs(x):
    bits = lax.bitcast_convert_type(x, jnp.uint32) | jnp.uint32(0x80000000)
    return lax.bitcast_convert_type(bits, F32)


def _ref_rows(g_cum, w, rev, c):
    idx = w if rev else w - 1
    g3 = g_cum.reshape(c // (2 * w), 2 * w, LANES)
    return jnp.broadcast_to(g3[:, idx:idx + 1, :], g3.shape).reshape(c, LANES)


def _hgrn_stage_a(tcum, tsmall, gh, gm, gc_ref, ge_ref, c):
    hm = jnp.concatenate([gh, gm], axis=0)
    half = c // 2
    for r in range(2):
        rows = slice(r * half, (r + 1) * half)
        gc_ref[rows, :] = jnp.dot(tcum[rows], hm, preferred_element_type=F32)
        yield
    per_tile = MXU_COLS // LANES
    for t in range(c // GROUP // per_tile):
        groups = [slice(g * GROUP, (g + 1) * GROUP) for g in range(t * per_tile, (t + 1) * per_tile)]
        wide = jnp.concatenate([jnp.concatenate([gh[g] for g in groups], axis=1),
                                jnp.concatenate([gm[g] for g in groups], axis=1)], axis=0)
        ge_ref[:, t * MXU_COLS:(t + 1) * MXU_COLS] = jnp.dot(tsmall, wide,
                                                              preferred_element_type=F32)
        yield


def _hgrn_stage_b(q, k, v, gc_ref, ge_ref, lev, st_ref, a_ref, ab_ref, oi_ref, rev, c):
    tn = (((0,), (0,)), ((), ()))
    g_cum = gc_ref[...]
    g_last = g_cum[0:1, :] if rev else g_cum[c - 1:c, :]
    qf, kf = q.astype(F32), k.astype(F32)

    st = st_ref[...]
    qi = (qf * jnp.exp2(g_cum)).astype(BF16)
    oi_ref[...] = jnp.dot(qi, st.T.astype(BF16), preferred_element_type=F32)
    kd = (kf * jnp.exp2(g_last - g_cum)).astype(BF16)
    u_t = lax.dot_general(v, kd, tn, preferred_element_type=F32)
    st_ref[...] = jnp.exp2(g_last) * st + u_t
    yield

    w = c // 2
    while w >= GROUP:
        x = jnp.exp2(_neg_abs(g_cum - _ref_rows(g_cum, w, rev, c)))
        q_rows, k_rows = [], []
        for b in range(c // (2 * w)):
            lo_half = slice(b * 2 * w, b * 2 * w + w)
            hi_half = slice(b * 2 * w + w, (b + 1) * 2 * w)
            qs = lo_half if rev else hi_half
            q_rows.append(qf[qs] * x[qs])
            k_rows.extend([kf[lo_half], kf[hi_half] * x[hi_half]] if rev
                          else [kf[lo_half] * x[lo_half], kf[hi_half]])
        p = jnp.dot(jnp.concatenate(q_rows, axis=0).astype(BF16),
                    jnp.concatenate(k_rows, axis=0).T.astype(BF16),
                    preferred_element_type=F32)
        for b in range(c // (2 * w)):
            lo_half = slice(b * 2 * w, b * 2 * w + w)
            hi_half = slice(b * 2 * w + w, (b + 1) * 2 * w)
            qs, ks = (lo_half, hi_half) if rev else (hi_half, lo_half)
            a_ref[qs, ks] = p[b * w:(b + 1) * w, ks]
        w //= 2
        yield

    a = jnp.where(lev == -1, jnp.dot(q, kf.T.astype(BF16), preferred_element_type=F32), 0.0)
    for li in range(len(SMALL_W)):
        e = jnp.concatenate([ge_ref[li * GROUP:(li + 1) * GROUP, g * LANES:(g + 1) * LANES]
                             for g in range(c // GROUP)], axis=0)
        x = jnp.exp2(e)
        p = jnp.dot((qf * x).astype(BF16), (kf * x).T.astype(BF16),
                    preferred_element_type=F32)
        a = jnp.where(lev == li, p, a)
        yield
    for b in range(c // GROUP):
        blk = slice(b * GROUP, (b + 1) * GROUP)
        a_ref[blk, blk] = a[blk, blk]
    ab_ref[...] = a_ref[...].astype(BF16)
    yield


def _interleave(main, fill):
    for _ in main:
        next(fill, None)
    for _ in fill:
        pass


def _hgrn_kernel(q_ref, i_ref, sg_ref, kf_ref, kb_ref, ghf_ref, gmf_ref, ghb_ref, gmb_ref,
                 ng_ref, lev_ref, tc_ref, ts_ref, o_ref,
                 acc_ref, st_ref, a_ref, ab_ref, oi_ref, gc_ref, ge_ref, *, c, nc):
    for ref in (acc_ref, st_ref, a_ref, ab_ref, oi_ref):
        ref[...] = jnp.zeros_like(ref)
    k_refs = (kf_ref, kb_ref)
    g_refs = ((ghf_ref, gmf_ref), (ghb_ref, gmb_ref))

    def rows_of(p, d):
        j = p if d == 0 else nc - 1 - p
        return pl.ds(pl.multiple_of(j * c, c), c)

    def stage_a(p, slot, d):
        rows = rows_of(p, d)
        return _hgrn_stage_a(tc_ref[d], ts_ref[d], g_refs[d][0][0, 0, rows, :],
                             g_refs[d][1][0, 0, rows, :], gc_ref.at[slot, d], ge_ref.at[slot, d], c)

    def stage_t(p, d):
        rows = rows_of(p, d)
        acc_ref[rows, :] += oi_ref[d] + jnp.dot(ab_ref[d], i_ref[0, 0, rows, :],
                                                preferred_element_type=F32)

    def stage_b(p, slot, d):
        rows = rows_of(p, d)
        return _hgrn_stage_b(q_ref[0, 0, rows, :], k_refs[d][0, 0, rows, :], i_ref[0, 0, rows, :],
                             gc_ref.at[slot, d], ge_ref.at[slot, d], lev_ref[d], st_ref.at[d],
                             a_ref.at[d], ab_ref.at[d], oi_ref.at[d], d == 1, c)

    def pair(p, slot):
        p_next, p_prev = jnp.minimum(p + 1, nc - 1), jnp.maximum(p - 1, 0)
        for d in range(2):
            stage_t(p_prev, d)
            _interleave(stage_b(p, slot, d), stage_a(p_next, 1 - slot, d))

    def pairs(i, carry):
        for u in range(PAIRS_PER_BODY):
            pair(PAIRS_PER_BODY * i + u, u % 2)
        return carry

    for d in range(2):
        for _ in stage_a(0, 0, d):
            pass
    lax.fori_loop(0, nc // PAIRS_PER_BODY, pairs, 0)
    for d in range(2):
        stage_t(nc - 1, d)

    ng = ng_ref[...]
    rows_fin = FINISH_CHUNKS * c

    def finish(j, carry):
        rows = pl.ds(pl.multiple_of(j * rows_fin, rows_fin), rows_fin)
        o = acc_ref[rows, :]
        ms = jnp.mean(o * o, axis=-1, keepdims=True)
        o = o * lax.rsqrt(ms + EPS) * ng
        o_ref[0, rows, :] = (o * sg_ref[0, 0, rows, :].astype(F32)).astype(o_ref.dtype)
        return carry

    lax.fori_loop(0, nc // FINISH_CHUNKS, finish, 0)


def _hgrn(hg, norm_g):
    b, _, s, _ = hg.shape
    c = HG_CHUNK
    nc = s // c
    assert PAIRS_PER_BODY % 2 == 0 and nc % PAIRS_PER_BODY == 0 and c % (2 * GROUP) == 0
    lev, tcum, tsmall = _hgrn_consts(c)
    n_small = len(SMALL_W) * GROUP

    def kind(k):
        return pl.BlockSpec((1, 1, s, HG_DIM), lambda i, h: (i, k * HG_HEADS + h, 0, 0))

    const3 = lambda a: pl.BlockSpec(a.shape, lambda i, h: (0, 0, 0))
    return pl.pallas_call(
        functools.partial(_hgrn_kernel, c=c, nc=nc),
        grid=(b, HG_HEADS),
        in_specs=[kind(HK_Q), kind(HK_I), kind(HK_G), kind(HK_KF), kind(HK_KB),
                  kind(HK_GHF), kind(HK_GMF), kind(HK_GHB), kind(HK_GMB),
                  pl.BlockSpec((1, LANES), lambda i, h: (0, h)),
                  const3(lev), const3(tcum), const3(tsmall)],
        out_specs=pl.BlockSpec((1, s, LANES), lambda i, h: (i, 0, h)),
        out_shape=jax.ShapeDtypeStruct((b, s, HG_WIDTH), BF16),
        scratch_shapes=[pltpu.VMEM((s, HG_DIM), F32),
                        pltpu.VMEM((2, HG_DIM, HG_DIM), F32),
                        pltpu.VMEM((2, c, c), F32),
                        pltpu.VMEM((2, c, c), BF16),
                        pltpu.VMEM((2, c, HG_DIM), F32),
                        pltpu.VMEM((2, 2, c, LANES), F32),
                        pltpu.VMEM((2, 2, n_small, (c // GROUP) * LANES), F32)],
        compiler_params=_cparams(("arbitrary", "arbitrary")),
        name="hgrn",
    )(hg, hg, hg, hg, hg, hg, hg, hg, hg, norm_g, lev, tcum, tsmall)


def _attn_kernel(sink_ref, q_ref, kp_ref, ko_ref, kn_ref, vp_ref, vo_ref, vn_ref,
                 bias_ref, o_ref, lg_ref, *, n_steps):
    n = pl.program_id(1)
    half = LANES // 2
    nk = 3 * BLOCK
    kt = jnp.concatenate([kp_ref[0], ko_ref[0], kn_ref[0]], axis=1)
    vb = jnp.concatenate([vp_ref[0], vo_ref[0], vn_ref[0]], axis=0)
    lo = lax.broadcasted_iota(jnp.int32, (BLOCK, LANES), 1) < half
    lo_v = lax.broadcasted_iota(jnp.int32, vb.shape, 1) < half
    zero_v = jnp.zeros_like(vb)
    zero_k = jnp.zeros((half, nk), kt.dtype)
    kts = (kt[:half], kt[half:])
    v0lo, v1hi = jnp.where(lo_v, vb, zero_v), jnp.where(lo_v, zero_v, vb)
    v_lo = (v0lo, pltpu.roll(v1hi, half, axis=1))
    v_hi = (pltpu.roll(v0lo, half, axis=1), v1hi)

    def logits(i):
        keys = slice(i * BLOCK, i * BLOCK + nk)
        rows = slice(i * BLOCK, (i + 1) * BLOCK)
        for m in range(ATT_Q_HEADS // 2):
            kj = kts[(2 * m) // ATT_GROUP][:, keys]
            k_both = jnp.concatenate([jnp.concatenate([kj, zero_k], axis=0),
                                      jnp.concatenate([zero_k, kj], axis=0)], axis=1)
            lg_ref[i, m] = jnp.dot(q_ref[0, rows, m * LANES:(m + 1) * LANES], k_both,
                                   preferred_element_type=F32)

    def softmax_pv(i):
        keys = slice(i * BLOCK, i * BLOCK + nk)
        rows = slice(i * BLOCK, (i + 1) * BLOCK)
        edge = 1
        if i == ATT_SUB - 1:
            edge = jnp.where(n == n_steps - 1, 2, edge)
        if i == 0:
            edge = jnp.where(n == 0, 0, edge)
        for m in range(ATT_Q_HEADS // 2):
            j = (2 * m) // ATT_GROUP
            ps, rdens = [], []
            for hh in range(2):
                lg = (lg_ref[i, m, :, hh * nk:(hh + 1) * nk]
                      + bias_ref[edge, m, hh * BLOCK:(hh + 1) * BLOCK, :])
                sink = sink_ref[2 * m + hh]
                mx = jnp.maximum(jnp.max(lg, axis=-1, keepdims=True), sink)
                p = jnp.exp2(lg - mx)
                rdens.append(1.0 / (jnp.sum(p, axis=-1, keepdims=True) + jnp.exp2(sink - mx)))
                ps.append(p.astype(BF16))
            v_both = jnp.concatenate([v_lo[j][keys], v_hi[j][keys]], axis=0)
            o = jnp.dot(jnp.concatenate(ps, axis=1), v_both, preferred_element_type=F32)
            o_ref[0, rows, m * LANES:(m + 1) * LANES] = (
                o * jnp.where(lo, rdens[0], rdens[1])).astype(o_ref.dtype)

    logits(0)
    for i in range(ATT_SUB):
        if i + 1 < ATT_SUB:
            logits(i + 1)
        softmax_pv(i)


def _attn(aq, ak, av, sink2, bias):
    b, s, _ = aq.shape
    rows = ATT_SUB * BLOCK
    n_steps = s // rows
    nb = s // BLOCK
    assert s % rows == 0 and nb >= 2

    def edge_block(n, shift):
        return jnp.clip(n * ATT_SUB + shift, 0, nb - 1)

    kt_edge = lambda shift: pl.BlockSpec((1, KV_WIDTH, BLOCK), lambda i, n: (i, 0, edge_block(n, shift)))
    v_edge = lambda shift: pl.BlockSpec((1, BLOCK, KV_WIDTH), lambda i, n: (i, edge_block(n, shift), 0))
    return pl.pallas_call(
        functools.partial(_attn_kernel, n_steps=n_steps),
        grid=(b, n_steps),
        in_specs=[pl.BlockSpec(memory_space=pltpu.SMEM),
                  pl.BlockSpec((1, rows, ATT_WIDTH), lambda i, n: (i, n, 0)),
                  kt_edge(-1), pl.BlockSpec((1, KV_WIDTH, rows), lambda i, n: (i, 0, n)), kt_edge(ATT_SUB),
                  v_edge(-1), pl.BlockSpec((1, rows, KV_WIDTH), lambda i, n: (i, n, 0)), v_edge(ATT_SUB),
                  pl.BlockSpec(bias.shape, lambda i, n: (0, 0, 0, 0))],
        out_specs=pl.BlockSpec((1, rows, ATT_WIDTH), lambda i, n: (i, n, 0)),
        out_shape=jax.ShapeDtypeStruct((b, s, ATT_WIDTH), BF16),
        scratch_shapes=[pltpu.VMEM((ATT_SUB, ATT_Q_HEADS // 2, BLOCK, 2 * 3 * BLOCK), F32)],
        compiler_params=_cparams(("arbitrary", "arbitrary")),
        name="attn",
    )(sink2, aq, ak, ak, ak, av, av, av, bias)


def _mixin_col_map():
    hw = HG_WIDTH
    att = ATT_WIDTH + 2 * KV_WIDTH
    return ((Z_FF, hw, 2 * hw), (Z_Q, 0, hw), (Z_G, 4 * hw, hw), (Z_AQ, 5 * hw, att),
            (Z_I, 3 * hw, hw))


def kernel(x, c, w_ada, b_ada, norm_g, w_ffn1_in, w_ffn1_out, w_ffn2_in, w_ffn2_out,
           w_mix_in, w_mix_out, hgrn_lb, hgrn_norm_g, qk_norm_g, attn_sink, rel_bias):
    b, s, d = x.shape
    depth = w_ada.shape[0]
    bias = _bias_tile(rel_bias)
    for l in range(depth):
        mods = _ada(c.astype(F32), w_ada[l], b_ada[l][None, :])
        sh1, sc1, g1, sh2, sc2, g2, sh3, sc3, g3 = [
            mods[:, i * d:(i + 1) * d][:, None, :] for i in range(N_MOD)]
        x, w2_in, w2_out, w_mi, w_mo = _ffn(
            x, sh1, sc1, g1, norm_g[l, 0][None, :],
            w_ffn1_in[l].astype(BF16), w_ffn1_out[l].astype(BF16),
            casts=((w_ffn2_in, l, None, 1), (w_ffn2_out, l, None, 2),
                   (w_mix_in, l, _mixin_col_map(), 1), (w_mix_out, l, None, 1)))
        qg = (jnp.tile(qk_norm_g[l, 0], 2) * (LOG2E / math.sqrt(ATT_HEAD_DIM)))[None, :]
        kg = jnp.tile(qk_norm_g[l, 1], 2)[None, :]
        hg, aq, ak, av = _mixin(x, sh2, sc2, norm_g[l, 1][None, :], w_mi, hgrn_lb, qg, kg, l)
        o_hg = _hgrn(hg, hgrn_norm_g[l][None, :])
        att = _attn(aq, ak, av, attn_sink[l] * LOG2E, bias)
        x = _ffn(x, sh3, sc3, g3, norm_g[l, 2][None, :],
                 w2_in, w2_out, mix=(o_hg, att, g2, w_mo))
    return x
```

```python
import functools
import math

import numpy as np
import jax
import jax.numpy as jnp
from jax import lax
from jax.experimental import pallas as pl
from jax.experimental.pallas import tpu as pltpu

F32 = jnp.float32
BF16 = jnp.bfloat16

D_MODEL = 1024
HG_HEADS = 4
HG_DIM = 128
HG_WIDTH = HG_HEADS * HG_DIM
ATT_Q_HEADS = 8
ATT_KV_HEADS = 2
ATT_HEAD_DIM = 64
ATT_GROUP = ATT_Q_HEADS // ATT_KV_HEADS
ATT_WIDTH = ATT_Q_HEADS * ATT_HEAD_DIM
KV_WIDTH = ATT_KV_HEADS * ATT_HEAD_DIM
WINDOW = 128
BLOCK = 128
NUM_BUCKETS = 32
MAX_DISTANCE = 128
D_FF = 2816
N_MOD = 9
EPS = 1e-6

LANES = 128
SUBLANES = 8
PACKED_ROWS = 16
MXU_COLS = 256
VMEM_LIMIT = 56 * 1024 * 1024

FFN_TM = 512
MIX_TM = 256
ATT_SUB = 1
HG_CHUNK = 128
NEG_INF = float("-inf")
LOG2E = 1.0 / math.log(2.0)

Z_FF, Z_FB, Z_Q, Z_G, Z_AQ = (n * HG_WIDTH for n in range(5))
Z_AK = Z_AQ + ATT_WIDTH
Z_AV = Z_AK + KV_WIDTH
Z_I = Z_AV + KV_WIDTH
Z_END = Z_I + HG_WIDTH
(HK_Q, HK_I, HK_G, HK_KF, HK_KB, HK_GHF, HK_GMF, HK_GHB, HK_GMB) = range(9)
N_HK = 9


def _cparams(sem, flags=None):
    return pltpu.CompilerParams(dimension_semantics=sem, vmem_limit_bytes=VMEM_LIMIT, flags=flags)


def _sigmoid(x):
    return 1.0 / (1.0 + jnp.exp2(x * (-LOG2E)))


def _ada_kernel(ct_ref, w_ref, b_ref, o_ref):
    ct = ct_ref[...]
    ca = ct * _sigmoid(ct)
    w = w_ref[...]
    for b in range(ct.shape[1]):
        o_ref[b:b + 1, :] = jnp.sum(w * ca[:, b:b + 1], axis=0, keepdims=True) + b_ref[...]


def _ada(c, w, b):
    batch, d = c.shape
    n = w.shape[1]
    tn = 1024
    return pl.pallas_call(
        _ada_kernel,
        grid=(n // tn,),
        in_specs=[pl.BlockSpec((d, batch), lambda j: (0, 0)),
                  pl.BlockSpec((d, tn), lambda j: (0, j)),
                  pl.BlockSpec((1, tn), lambda j: (0, j))],
        out_specs=pl.BlockSpec((batch, tn), lambda j: (0, j)),
        out_shape=jax.ShapeDtypeStruct((batch, n), F32),
        compiler_params=_cparams(("arbitrary",)),
        name="ada",
    )(c.T, w, b)


def _t5_bucket_np(rel):
    nb = NUM_BUCKETS // 2
    max_exact = nb // 2
    ret = (rel > 0).astype(np.int32) * nb
    n = np.abs(rel)
    ratio = np.maximum(n, 1).astype(np.float32) / np.float32(max_exact)
    large = max_exact + (np.log(ratio) / np.float32(math.log(MAX_DISTANCE / max_exact))
                         * np.float32(nb - max_exact)).astype(np.int32)
    large = np.minimum(large, nb - 1)
    return ret + np.where(n < max_exact, n, large)


def _bias_kernel(rb_ref, bucket_ref, mask_ref, o_ref):
    pair = pl.program_id(0)
    bucket = bucket_ref[...]
    for hh in range(2):
        head = 2 * pair + hh
        tile = jnp.zeros(bucket.shape, F32)
        for b in range(NUM_BUCKETS):
            tile = jnp.where(bucket == b, rb_ref[b, head], tile)
        tile = tile * LOG2E
        for e in range(3):
            o_ref[e, 0, hh * BLOCK:(hh + 1) * BLOCK, :] = tile + mask_ref[e]


def _bias_tile(rel_bias):
    kcol = np.arange(3 * BLOCK)[None, :]
    rel = (kcol - BLOCK) - np.arange(BLOCK)[:, None]
    bucket = _t5_bucket_np(rel).astype(np.int32)
    window = np.abs(rel) <= WINDOW
    valid = np.stack([window & (kcol >= BLOCK), window, window & (kcol < 2 * BLOCK)])
    mask = np.where(valid, 0.0, NEG_INF).astype(np.float32)
    n_pairs = ATT_Q_HEADS // 2
    return pl.pallas_call(
        _bias_kernel,
        grid=(n_pairs,),
        in_specs=[pl.BlockSpec(memory_space=pltpu.SMEM),
                  pl.BlockSpec(bucket.shape, lambda p: (0, 0)),
                  pl.BlockSpec(mask.shape, lambda p: (0, 0, 0))],
        out_specs=pl.BlockSpec((3, 1, 2 * BLOCK, 3 * BLOCK), lambda p: (0, p, 0, 0)),
        out_shape=jax.ShapeDtypeStruct((3, n_pairs, 2 * BLOCK, 3 * BLOCK), F32),
        compiler_params=_cparams(("arbitrary",)),
        name="bias_tile",
    )(rel_bias.astype(F32), jnp.asarray(bucket), jnp.asarray(mask))


def _norm_mod(x, ng, sh, sc):
    ms = jnp.mean(x * x, axis=-1, keepdims=True)
    return (x * lax.rsqrt(ms + EPS)) * (ng * (1.0 + sc)) + sh


def _ffn_kernel(*refs, mix, casts):
    if mix:
        hg_ref, at_ref, gm_ref, wm_ref, *refs = refs
    x_ref, sh_ref, sc_ref, gt_ref, ng_ref, win_ref, wout_ref, *refs = refs
    src_refs, (o_ref, *dst_refs) = refs[:len(casts)], refs[len(casts):]
    x = x_ref[0]
    if mix:
        mixed = (jnp.dot(hg_ref[0], wm_ref[:HG_WIDTH, :], preferred_element_type=F32)
                 + jnp.dot(at_ref[0], wm_ref[HG_WIDTH:, :], preferred_element_type=F32))
        x = x + gm_ref[0] * mixed
    h = _norm_mod(x, ng_ref[...], sh_ref[0], sc_ref[0]).astype(BF16)
    gu = jnp.dot(h, win_ref[...], preferred_element_type=F32)
    g = gu[:, :D_FF]
    u = gu[:, D_FF:]
    act = (g * _sigmoid(g) * u).astype(BF16)
    y = jnp.dot(act, wout_ref[...], preferred_element_type=F32)
    o_ref[0] = x + 0.5 * gt_ref[0] * y
    for src_ref, dst_ref, col_map in zip(src_refs, dst_refs, casts):
        for dst0, src0, width in col_map:
            dst_ref[:, dst0:dst0 + width] = src_ref[0, :, src0:src0 + width].astype(BF16)


def _ffn(x, sh, sc, gt, ng, w_in, w_out, mix=None, casts=()):
    b, s, d = x.shape
    tm = FFN_TM
    n_steps = b * (s // tm)
    vec = pl.BlockSpec((1, 1, d), lambda i, j: (i, 0, 0))
    whole = lambda a: pl.BlockSpec(a.shape, lambda i, j: (0, 0), pipeline_mode=pl.Buffered(1))
    args = [x, sh, sc, gt, ng, w_in, w_out]
    specs = [pl.BlockSpec((1, tm, d), lambda i, j: (i, j, 0)), vec, vec, vec,
             whole(ng), whole(w_in), whole(w_out)]
    if mix is not None:
        o_hg, att, gm, wm = mix
        half = pl.BlockSpec((1, tm, HG_WIDTH), lambda i, j: (i, j, 0))
        args = [o_hg, att, gm, wm] + args
        specs = [half, half, vec, whole(wm)] + specs
    out_specs = [pl.BlockSpec((1, tm, d), lambda i, j: (i, j, 0))]
    out_shape = [jax.ShapeDtypeStruct(x.shape, F32)]
    col_maps = []
    for mats, layer, col_map, dup in casts:
        _, rows, cols = mats.shape
        assert n_steps % dup == 0
        n_slabs = n_steps // dup
        assert rows % (n_slabs * PACKED_ROWS) == 0 and cols % LANES == 0
        slab = rows // n_slabs
        args.append(mats)
        specs.append(pl.BlockSpec(
            (1, slab, cols), lambda i, j, layer=layer, dup=dup: (layer, (i * (s // tm) + j) // dup, 0)))
        out_specs.append(pl.BlockSpec(
            (slab, cols), lambda i, j, dup=dup: ((i * (s // tm) + j) // dup, 0)))
        out_shape.append(jax.ShapeDtypeStruct((rows, cols), BF16))
        col_maps.append(tuple(col_map) if col_map is not None else ((0, 0, cols),))
    out = pl.pallas_call(
        functools.partial(_ffn_kernel, mix=mix is not None, casts=tuple(col_maps)),
        grid=(b, s // tm),
        in_specs=specs,
        out_specs=out_specs,
        out_shape=out_shape,
        compiler_params=_cparams(("arbitrary", "arbitrary")),
        name="ffn_mix" if mix is not None else "ffn",
    )(*args)
    return out if casts else out[0]


def _half_norm(x, gain, lo):
    x2 = x * x
    s_lo = jnp.sum(jnp.where(lo, x2, 0.0), axis=-1, keepdims=True)
    s_hi = jnp.sum(jnp.where(lo, 0.0, x2), axis=-1, keepdims=True)
    ms = jnp.where(lo, s_lo, s_hi) * (1.0 / ATT_HEAD_DIM)
    return x * lax.rsqrt(ms + EPS) * gain


def _mixin_kernel(x_ref, sh_ref, sc_ref, ng_ref, w_ref, lb_ref, qg_ref, kg_ref,
                  hg_ref, aq_ref, ak_ref, av_ref, *, layer):
    h = _norm_mod(x_ref[0], ng_ref[...], sh_ref[0], sc_ref[0]).astype(BF16)

    def proj(c0, c1):
        return jnp.dot(h, w_ref[:, c0:c1], preferred_element_type=F32)

    part_w = HG_WIDTH // 2

    def put_heads(kind, val, part):
        for hh in range(part_w // HG_DIM):
            head = part * (part_w // HG_DIM) + hh
            hg_ref[0, kind * HG_HEADS + head] = val[:, hh * HG_DIM:(hh + 1) * HG_DIM].astype(BF16)

    raw = lb_ref[...]
    mx = jnp.max(raw, axis=1, keepdims=True)
    ex = jnp.exp(raw - mx)
    lb = jnp.sum(ex[:, :layer + 1, :], axis=1) / jnp.sum(ex, axis=1)

    def forget(d, z0, kk, kh, km, part):
        cols = slice(part * part_w, (part + 1) * part_w)
        fr = proj(z0 + cols.start, z0 + cols.stop)
        lbd = lb[d:d + 1, cols]
        f = lbd + (1.0 - lbd) * _sigmoid(fr)
        g2 = jnp.log(f) * LOG2E
        hi = g2.astype(BF16)
        put_heads(kk, 1.0 - f, part)
        put_heads(kh, hi, part)
        put_heads(km, g2 - hi.astype(F32), part)

    def plain(kind, z0, part, silu):
        val = proj(z0 + part * part_w, z0 + (part + 1) * part_w)
        put_heads(kind, val * _sigmoid(val) if silu else val, part)

    lo = lax.broadcasted_iota(jnp.int32, (h.shape[0], LANES), 1) < LANES // 2

    def queries(part):
        aq = proj(Z_AQ + part * part_w, Z_AQ + (part + 1) * part_w)
        for m in range(part_w // LANES):
            cols = slice(m * LANES, (m + 1) * LANES)
            aq_ref[0, :, part * part_w + m * LANES:part * part_w + (m + 1) * LANES] = (
                _half_norm(aq[:, cols], qg_ref[...], lo).astype(BF16))

    def keys_values():
        akv = proj(Z_AK, Z_I)
        ak_ref[0] = _half_norm(akv[:, :KV_WIDTH], kg_ref[...], lo).T.astype(BF16)
        av_ref[0] = akv[:, KV_WIDTH:].astype(BF16)

    forget(0, Z_FF, HK_KF, HK_GHF, HK_GMF, 0)
    plain(HK_I, Z_I, 0, False)
    forget(0, Z_FF, HK_KF, HK_GHF, HK_GMF, 1)
    plain(HK_I, Z_I, 1, False)
    forget(1, Z_FB, HK_KB, HK_GHB, HK_GMB, 0)
    keys_values()
    forget(1, Z_FB, HK_KB, HK_GHB, HK_GMB, 1)
    plain(HK_Q, Z_Q, 0, True)
    queries(0)
    plain(HK_Q, Z_Q, 1, True)
    queries(1)
    plain(HK_G, Z_G, 0, True)
    plain(HK_G, Z_G, 1, True)


def _mixin(x, sh, sc, ng, w, hgrn_lb, qg, kg, layer):
    b, s, d = x.shape
    tm = MIX_TM
    vec = pl.BlockSpec((1, 1, d), lambda i, j: (i, 0, 0))
    row = lambda n: pl.BlockSpec((1, n), lambda i, j: (0, 0))
    rows3 = lambda n: pl.BlockSpec((1, tm, n), lambda i, j: (i, j, 0))
    n_hg = N_HK * HG_HEADS
    return pl.pallas_call(
        functools.partial(_mixin_kernel, layer=layer),
        grid=(b, s // tm),
        in_specs=[pl.BlockSpec((1, tm, d), lambda i, j: (i, j, 0)),
                  vec, vec, row(d),
                  pl.BlockSpec(w.shape, lambda i, j: (0, 0)),
                  pl.BlockSpec(hgrn_lb.shape, lambda i, j: (0, 0, 0)),
                  row(LANES), row(LANES)],
        out_specs=[pl.BlockSpec((1, n_hg, tm, HG_DIM), lambda i, j: (i, 0, j, 0)),
                   rows3(ATT_WIDTH),
                   pl.BlockSpec((1, KV_WIDTH, tm), lambda i, j: (i, 0, j)),
                   rows3(KV_WIDTH)],
        out_shape=[jax.ShapeDtypeStruct((b, n_hg, s, HG_DIM), BF16),
                   jax.ShapeDtypeStruct((b, s, ATT_WIDTH), BF16),
                   jax.ShapeDtypeStruct((b, KV_WIDTH, s), BF16),
                   jax.ShapeDtypeStruct((b, s, KV_WIDTH), BF16)],
        compiler_params=_cparams(("arbitrary", "arbitrary")),
        name="mix_in",
    )(x, sh, sc, ng, w, hgrn_lb, qg, kg)


SMALL_W = (1, 2, 4, 8)
GROUP = PACKED_ROWS
FINISH_CHUNKS = 8
PAIRS_PER_BODY = 8


def _hgrn_consts(c):
    t = np.arange(c)[:, None]
    s = np.arange(c)[None, :]
    x = t ^ s
    lev = np.where(x > 0, np.floor(np.log2(np.maximum(x, 1))), -1).astype(np.int32)
    lev_f = np.where(t > s, lev, np.where(t == s, -1, -2)).astype(np.int32)
    lev_b = lev_f.T.copy()

    def exponent_rows(w, rev):
        m = np.zeros((GROUP, GROUP), np.float32)
        for r in range(GROUP):
            b0 = (r // (2 * w)) * 2 * w
            if not rev:
                ref = b0 + w - 1
                lo_u, hi_u = (ref + 1, r) if r > ref else (r + 1, ref)
            else:
                ref = b0 + w
                lo_u, hi_u = (r, ref - 1) if r < ref else (ref, r - 1)
            m[r, lo_u:hi_u + 1] = 1.0
        return m

    cums, smalls = [], []
    for rev in (False, True):
        cum = ((s >= t) if rev else (s <= t)).astype(np.float32)
        cums.append(np.concatenate([cum, cum], axis=1))
        rows = np.concatenate([exponent_rows(w, rev) for w in SMALL_W])
        smalls.append(np.concatenate([rows, rows], axis=1))
    return (jnp.asarray(np.stack([lev_f, lev_b])),
            jnp.asarray(np.stack(cums), dtype=BF16),
            jnp.asarray(np.stack(smalls), dtype=BF16))


def _neg_abs(x):
    bits = lax.bitcast_convert_type(x, jnp.uint32) | jnp.uint32(0x80000000)
    return lax.bitcast_convert_type(bits, F32)


def _ref_rows(g_cum, w, rev, c):
    idx = w if rev else w - 1
    g3 = g_cum.reshape(c // (2 * w), 2 * w, LANES)
    return jnp.broadcast_to(g3[:, idx:idx + 1, :], g3.shape).reshape(c, LANES)


def _hgrn_stage_a(tcum, tsmall, gh, gm, gc_ref, ge_ref, c):
    hm = jnp.concatenate([gh, gm], axis=0)
    half = c // 2
    for r in range(2):
        rows = slice(r * half, (r + 1) * half)
        gc_ref[rows, :] = jnp.dot(tcum[rows], hm, preferred_element_type=F32)
        yield
    per_tile = MXU_COLS // LANES
    for t in range(c // GROUP // per_tile):
        groups = [slice(g * GROUP, (g + 1) * GROUP) for g in range(t * per_tile, (t + 1) * per_tile)]
        wide = jnp.concatenate([jnp.concatenate([gh[g] for g in groups], axis=1),
                                jnp.concatenate([gm[g] for g in groups], axis=1)], axis=0)
        ge_ref[:, t * MXU_COLS:(t + 1) * MXU_COLS] = jnp.dot(tsmall, wide,
                                                              preferred_element_type=F32)
        yield


def _hgrn_stage_b(q, k, v, gc_ref, ge_ref, lev, st_ref, a_ref, ab_ref, oi_ref, rev, c):
    tn = (((0,), (0,)), ((), ()))
    g_cum = gc_ref[...]
    g_last = g_cum[0:1, :] if rev else g_cum[c - 1:c, :]
    qf, kf = q.astype(F32), k.astype(F32)

    st = st_ref[...]
    qi = (qf * jnp.exp2(g_cum)).astype(BF16)
    oi_ref[...] = jnp.dot(qi, st.T.astype(BF16), preferred_element_type=F32)
    kd = (kf * jnp.exp2(g_last - g_cum)).astype(BF16)
    u_t = lax.dot_general(v, kd, tn, preferred_element_type=F32)
    st_ref[...] = jnp.exp2(g_last) * st + u_t
    yield

    w = c // 2
    while w >= GROUP:
        x = jnp.exp2(_neg_abs(g_cum - _ref_rows(g_cum, w, rev, c)))
        q_rows, k_rows = [], []
        for b in range(c // (2 * w)):
            lo_half = slice(b * 2 * w, b * 2 * w + w)
            hi_half = slice(b * 2 * w + w, (b + 1) * 2 * w)
            qs = lo_half if rev else hi_half
            q_rows.append(qf[qs] * x[qs])
            k_rows.extend([kf[lo_half], kf[hi_half] * x[hi_half]] if rev
                          else [kf[lo_half] * x[lo_half], kf[hi_half]])
        p = jnp.dot(jnp.concatenate(q_rows, axis=0).astype(BF16),
                    jnp.concatenate(k_rows, axis=0).T.astype(BF16),
                    preferred_element_type=F32)
        for b in range(c // (2 * w)):
            lo_half = slice(b * 2 * w, b * 2 * w + w)
            hi_half = slice(b * 2 * w + w, (b + 1) * 2 * w)
            qs, ks = (lo_half, hi_half) if rev else (hi_half, lo_half)
            a_ref[qs, ks] = p[b * w:(b + 1) * w, ks]
        w //= 2
        yield

    a = jnp.where(lev == -1, jnp.dot(q, kf.T.astype(BF16), preferred_element_type=F32), 0.0)
    for li in range(len(SMALL_W)):
        e = jnp.concatenate([ge_ref[li * GROUP:(li + 1) * GROUP, g * LANES:(g + 1) * LANES]
                             for g in range(c // GROUP)], axis=0)
        x = jnp.exp2(e)
        p = jnp.dot((qf * x).astype(BF16), (kf * x).T.astype(BF16),
                    preferred_element_type=F32)
        a = jnp.where(lev == li, p, a)
        yield
    for b in range(c // GROUP):
        blk = slice(b * GROUP, (b + 1) * GROUP)
        a_ref[blk, blk] = a[blk, blk]
    ab_ref[...] = a_ref[...].astype(BF16)
    yield


def _interleave(main, fill):
    for _ in main:
        next(fill, None)
    for _ in fill:
        pass


def _hgrn_kernel(q_ref, i_ref, sg_ref, kf_ref, kb_ref, ghf_ref, gmf_ref, ghb_ref, gmb_ref,
                 ng_ref, lev_ref, tc_ref, ts_ref, o_ref,
                 acc_ref, st_ref, a_ref, ab_ref, oi_ref, gc_ref, ge_ref, *, c, nc):
    for ref in (acc_ref, st_ref, a_ref, ab_ref, oi_ref):
        ref[...] = jnp.zeros_like(ref)
    k_refs = (kf_ref, kb_ref)
    g_refs = ((ghf_ref, gmf_ref), (ghb_ref, gmb_ref))

    def rows_of(p, d):
        j = p if d == 0 else nc - 1 - p
        return pl.ds(pl.multiple_of(j * c, c), c)

    def stage_a(p, slot, d):
        rows = rows_of(p, d)
        return _hgrn_stage_a(tc_ref[d], ts_ref[d], g_refs[d][0][0, 0, rows, :],
                             g_refs[d][1][0, 0, rows, :], gc_ref.at[slot, d], ge_ref.at[slot, d], c)

    def stage_t(p, d):
        rows = rows_of(p, d)
        acc_ref[rows, :] += oi_ref[d] + jnp.dot(ab_ref[d], i_ref[0, 0, rows, :],
                                                preferred_element_type=F32)

    def stage_b(p, slot, d):
        rows = rows_of(p, d)
        return _hgrn_stage_b(q_ref[0, 0, rows, :], k_refs[d][0, 0, rows, :], i_ref[0, 0, rows, :],
                             gc_ref.at[slot, d], ge_ref.at[slot, d], lev_ref[d], st_ref.at[d],
                             a_ref.at[d], ab_ref.at[d], oi_ref.at[d], d == 1, c)

    def pair(p, slot):
        p_next, p_prev = jnp.minimum(p + 1, nc - 1), jnp.maximum(p - 1, 0)
        for d in range(2):
            stage_t(p_prev, d)
            _interleave(stage_b(p, slot, d), stage_a(p_next, 1 - slot, d))

    def pairs(i, carry):
        for u in range(PAIRS_PER_BODY):
            pair(PAIRS_PER_BODY * i + u, u % 2)
        return carry

    for d in range(2):
        for _ in stage_a(0, 0, d):
            pass
    lax.fori_loop(0, nc // PAIRS_PER_BODY, pairs, 0)
    for d in range(2):
        stage_t(nc - 1, d)

    ng = ng_ref[...]
    rows_fin = FINISH_CHUNKS * c

    def finish(j, carry):
        rows = pl.ds(pl.multiple_of(j * rows_fin, rows_fin), rows_fin)
        o = acc_ref[rows, :]
        ms = jnp.mean(o * o, axis=-1, keepdims=True)
        o = o * lax.rsqrt(ms + EPS) * ng
        o_ref[0, rows, :] = (o * sg_ref[0, 0, rows, :].astype(F32)).astype(o_ref.dtype)
        return carry

    lax.fori_loop(0, nc // FINISH_CHUNKS, finish, 0)


def _hgrn(hg, norm_g):
    b, _, s, _ = hg.shape
    c = HG_CHUNK
    nc = s // c
    assert PAIRS_PER_BODY % 2 == 0 and nc % PAIRS_PER_BODY == 0 and c % (2 * GROUP) == 0
    lev, tcum, tsmall = _hgrn_consts(c)
    n_small = len(SMALL_W) * GROUP

    def kind(k):
        return pl.BlockSpec((1, 1, s, HG_DIM), lambda i, h: (i, k * HG_HEADS + h, 0, 0))

    const3 = lambda a: pl.BlockSpec(a.shape, lambda i, h: (0, 0, 0))
    return pl.pallas_call(
        functools.partial(_hgrn_kernel, c=c, nc=nc),
        grid=(b, HG_HEADS),
        in_specs=[kind(HK_Q), kind(HK_I), kind(HK_G), kind(HK_KF), kind(HK_KB),
                  kind(HK_GHF), kind(HK_GMF), kind(HK_GHB), kind(HK_GMB),
                  pl.BlockSpec((1, LANES), lambda i, h: (0, h)),
                  const3(lev), const3(tcum), const3(tsmall)],
        out_specs=pl.BlockSpec((1, s, LANES), lambda i, h: (i, 0, h)),
        out_shape=jax.ShapeDtypeStruct((b, s, HG_WIDTH), BF16),
        scratch_shapes=[pltpu.VMEM((s, HG_DIM), F32),
                        pltpu.VMEM((2, HG_DIM, HG_DIM), F32),
                        pltpu.VMEM((2, c, c), F32),
                        pltpu.VMEM((2, c, c), BF16),
                        pltpu.VMEM((2, c, HG_DIM), F32),
                        pltpu.VMEM((2, 2, c, LANES), F32),
                        pltpu.VMEM((2, 2, n_small, (c // GROUP) * LANES), F32)],
        compiler_params=_cparams(("arbitrary", "arbitrary")),
        name="hgrn",
    )(hg, hg, hg, hg, hg, hg, hg, hg, hg, norm_g, lev, tcum, tsmall)


def _attn_kernel(sink_ref, q_ref, kp_ref, ko_ref, kn_ref, vp_ref, vo_ref, vn_ref,
                 bias_ref, o_ref, lg_ref, *, n_steps):
    n = pl.program_id(1)
    half = LANES // 2
    nk = 3 * BLOCK
    kt = jnp.concatenate([kp_ref[0], ko_ref[0], kn_ref[0]], axis=1)
    vb = jnp.concatenate([vp_ref[0], vo_ref[0], vn_ref[0]], axis=0)
    lo = lax.broadcasted_iota(jnp.int32, (BLOCK, LANES), 1) < half
    lo_v = lax.broadcasted_iota(jnp.int32, vb.shape, 1) < half
    zero_v = jnp.zeros_like(vb)
    zero_k = jnp.zeros((half, nk), kt.dtype)
    kts = (kt[:half], kt[half:])
    v0lo, v1hi = jnp.where(lo_v, vb, zero_v), jnp.where(lo_v, zero_v, vb)
    v_lo = (v0lo, pltpu.roll(v1hi, half, axis=1))
    v_hi = (pltpu.roll(v0lo, half, axis=1), v1hi)

    def logits(i):
        keys = slice(i * BLOCK, i * BLOCK + nk)
        rows = slice(i * BLOCK, (i + 1) * BLOCK)
        for m in range(ATT_Q_HEADS // 2):
            kj = kts[(2 * m) // ATT_GROUP][:, keys]
            k_both = jnp.concatenate([jnp.concatenate([kj, zero_k], axis=0),
                                      jnp.concatenate([zero_k, kj], axis=0)], axis=1)
            lg_ref[i, m] = jnp.dot(q_ref[0, rows, m * LANES:(m + 1) * LANES], k_both,
                                   preferred_element_type=F32)

    def softmax_pv(i):
        keys = slice(i * BLOCK, i * BLOCK + nk)
        rows = slice(i * BLOCK, (i + 1) * BLOCK)
        edge = 1
        if i == ATT_SUB - 1:
            edge = jnp.where(n == n_steps - 1, 2, edge)
        if i == 0:
            edge = jnp.where(n == 0, 0, edge)
        for m in range(ATT_Q_HEADS // 2):
            j = (2 * m) // ATT_GROUP
            ps, rdens = [], []
            for hh in range(2):
                lg = (lg_ref[i, m, :, hh * nk:(hh + 1) * nk]
                      + bias_ref[edge, m, hh * BLOCK:(hh + 1) * BLOCK, :])
                sink = sink_ref[2 * m + hh]
                mx = jnp.maximum(jnp.max(lg, axis=-1, keepdims=True), sink)
                p = jnp.exp2(lg - mx)
                rdens.append(1.0 / (jnp.sum(p, axis=-1, keepdims=True) + jnp.exp2(sink - mx)))
                ps.append(p.astype(BF16))
            v_both = jnp.concatenate([v_lo[j][keys], v_hi[j][keys]], axis=0)
            o = jnp.dot(jnp.concatenate(ps, axis=1), v_both, preferred_element_type=F32)
            o_ref[0, rows, m * LANES:(m + 1) * LANES] = (
                o * jnp.where(lo, rdens[0], rdens[1])).astype(o_ref.dtype)

    logits(0)
    for i in range(ATT_SUB):
        if i + 1 < ATT_SUB:
            logits(i + 1)
        softmax_pv(i)


def _attn(aq, ak, av, sink2, bias):
    b, s, _ = aq.shape
    rows = ATT_SUB * BLOCK
    n_steps = s // rows
    nb = s // BLOCK
    assert s % rows == 0 and nb >= 2

    def edge_block(n, shift):
        return jnp.clip(n * ATT_SUB + shift, 0, nb - 1)

    kt_edge = lambda shift: pl.BlockSpec((1, KV_WIDTH, BLOCK), lambda i, n: (i, 0, edge_block(n, shift)))
    v_edge = lambda shift: pl.BlockSpec((1, BLOCK, KV_WIDTH), lambda i, n: (i, edge_block(n, shift), 0))
    return pl.pallas_call(
        functools.partial(_attn_kernel, n_steps=n_steps),
        grid=(b, n_steps),
        in_specs=[pl.BlockSpec(memory_space=pltpu.SMEM),
                  pl.BlockSpec((1, rows, ATT_WIDTH), lambda i, n: (i, n, 0)),
                  kt_edge(-1), pl.BlockSpec((1, KV_WIDTH, rows), lambda i, n: (i, 0, n)), kt_edge(ATT_SUB),
                  v_edge(-1), pl.BlockSpec((1, rows, KV_WIDTH), lambda i, n: (i, n, 0)), v_edge(ATT_SUB),
                  pl.BlockSpec(bias.shape, lambda i, n: (0, 0, 0, 0))],
        out_specs=pl.BlockSpec((1, rows, ATT_WIDTH), lambda i, n: (i, n, 0)),
        out_shape=jax.ShapeDtypeStruct((b, s, ATT_WIDTH), BF16),
        scratch_shapes=[pltpu.VMEM((ATT_SUB, ATT_Q_HEADS // 2, BLOCK, 2 * 3 * BLOCK), F32)],
        compiler_params=_cparams(("arbitrary", "arbitrary")),
        name="attn",
    )(sink2, aq, ak, ak, ak, av, av, av, bias)


def _mixin_col_map():
    hw = HG_WIDTH
    att = ATT_WIDTH + 2 * KV_WIDTH
    return ((Z_FF, hw, 2 * hw), (Z_Q, 0, hw), (Z_G, 4 * hw, hw), (Z_AQ, 5 * hw, att),
            (Z_I, 3 * hw, hw))


def kernel(x, c, w_ada, b_ada, norm_g, w_ffn1_in, w_ffn1_out, w_ffn2_in, w_ffn2_out,
           w_mix_in, w_mix_out, hgrn_lb, hgrn_norm_g, qk_norm_g, attn_sink, rel_bias):
    b, s, d = x.shape
    depth = w_ada.shape[0]
    bias = _bias_tile(rel_bias)
    for l in range(depth):
        mods = _ada(c.astype(F32), w_ada[l], b_ada[l][None, :])
        sh1, sc1, g1, sh2, sc2, g2, sh3, sc3, g3 = [
            mods[:, i * d:(i + 1) * d][:, None, :] for i in range(N_MOD)]
        x, w2_in, w2_out, w_mi, w_mo = _ffn(
            x, sh1, sc1, g1, norm_g[l, 0][None, :],
            w_ffn1_in[l].astype(BF16), w_ffn1_out[l].astype(BF16),
            casts=((w_ffn2_in, l, None, 1), (w_ffn2_out, l, None, 2),
                   (w_mix_in, l, _mixin_col_map(), 1), (w_mix_out, l, None, 1)))
        qg = (jnp.tile(qk_norm_g[l, 0], 2) * (LOG2E / math.sqrt(ATT_HEAD_DIM)))[None, :]
        kg = jnp.tile(qk_norm_g[l, 1], 2)[None, :]
        hg, aq, ak, av = _mixin(x, sh2, sc2, norm_g[l, 1][None, :], w_mi, hgrn_lb, qg, kg, l)
        o_hg = _hgrn(hg, hgrn_norm_g[l][None, :])
        att = _attn(aq, ak, av, attn_sink[l] * LOG2E, bias)
        x = _ffn(x, sh3, sc3, g3, norm_g[l, 2][None, :],
                 w2_in, w2_out, mix=(o_hg, att, g2, w_mo))
    return x
```

```python
import functools
import math

import numpy as np
import jax
import jax.numpy as jnp
from jax import lax
from jax.experimental import pallas as pl
from jax.experimental.pallas import tpu as pltpu

F32 = jnp.float32
BF16 = jnp.bfloat16

HG_HEADS = 4
HG_DIM = 128
HG_WIDTH = HG_HEADS * HG_DIM
ATT_Q_HEADS = 8
ATT_KV_HEADS = 2
ATT_HEAD_DIM = 64
ATT_GROUP = ATT_Q_HEADS // ATT_KV_HEADS
ATT_WIDTH = ATT_Q_HEADS * ATT_HEAD_DIM
KV_WIDTH = ATT_KV_HEADS * ATT_HEAD_DIM
WINDOW = 128
BLOCK = 128
NUM_BUCKETS = 32
MAX_DISTANCE = 128
D_FF = 2816
N_MOD = 9
EPS = 1e-6

LANES = 128
PACKED_ROWS = 16
MXU_COLS = 256
VMEM_LIMIT = 56 * 1024 * 1024

ADA_TN = 2304
FFN_TM = 512
MIX_TM = 256
HG_CHUNK = 128
NEG_INF = float("-inf")
LOG2E = 1.0 / math.log(2.0)

Z_FF, Z_FB, Z_Q, Z_G, Z_AQ = (n * HG_WIDTH for n in range(5))
Z_AK = Z_AQ + ATT_WIDTH
Z_AV = Z_AK + KV_WIDTH
Z_I = Z_AV + KV_WIDTH
Z_END = Z_I + HG_WIDTH
(HK_Q, HK_I, HK_G, HK_KF, HK_KB, HK_GHF, HK_GMF, HK_GHB, HK_GMB) = range(9)
N_HK = 9


def _cparams(sem):
    return pltpu.CompilerParams(dimension_semantics=sem, vmem_limit_bytes=VMEM_LIMIT)


def _sigmoid(x):
    return 1.0 / (1.0 + jnp.exp2(x * (-LOG2E)))


def _ada_kernel(ct_ref, w_ref, b_ref, o_ref):
    ct = ct_ref[...]
    ca = ct * _sigmoid(ct)
    w = w_ref[...]
    for b in range(ct.shape[1]):
        o_ref[b:b + 1, :] = jnp.sum(w * ca[:, b:b + 1], axis=0, keepdims=True) + b_ref[...]


def _ada(c, w, b):
    batch, d = c.shape
    n = w.shape[1]
    tn = ADA_TN
    return pl.pallas_call(
        _ada_kernel,
        grid=(n // tn,),
        in_specs=[pl.BlockSpec((d, batch), lambda j: (0, 0)),
                  pl.BlockSpec((d, tn), lambda j: (0, j)),
                  pl.BlockSpec((1, tn), lambda j: (0, j))],
        out_specs=pl.BlockSpec((batch, tn), lambda j: (0, j)),
        out_shape=jax.ShapeDtypeStruct((batch, n), F32),
        compiler_params=_cparams(("arbitrary",)),
        name="ada",
    )(c.T, w, b)


def _t5_bucket_np(rel):
    nb = NUM_BUCKETS // 2
    max_exact = nb // 2
    ret = (rel > 0).astype(np.int32) * nb
    n = np.abs(rel)
    ratio = np.maximum(n, 1).astype(np.float32) / np.float32(max_exact)
    large = max_exact + (np.log(ratio) / np.float32(math.log(MAX_DISTANCE / max_exact))
                         * np.float32(nb - max_exact)).astype(np.int32)
    large = np.minimum(large, nb - 1)
    return ret + np.where(n < max_exact, n, large)


def _bias_kernel(rb_ref, bucket_ref, mask_ref, o_ref):
    pair = pl.program_id(0)
    bucket = bucket_ref[...]
    for hh in range(2):
        head = 2 * pair + hh
        tile = jnp.zeros(bucket.shape, F32)
        for b in range(NUM_BUCKETS):
            tile = jnp.where(bucket == b, rb_ref[b, head], tile)
        tile = tile * LOG2E
        for e in range(3):
            o_ref[e, 0, hh * BLOCK:(hh + 1) * BLOCK, :] = tile + mask_ref[e]


def _bias_tile(rel_bias):
    kcol = np.arange(3 * BLOCK)[None, :]
    rel = (kcol - BLOCK) - np.arange(BLOCK)[:, None]
    bucket = _t5_bucket_np(rel).astype(np.int32)
    window = np.abs(rel) <= WINDOW
    valid = np.stack([window & (kcol >= BLOCK), window, window & (kcol < 2 * BLOCK)])
    mask = np.where(valid, 0.0, NEG_INF).astype(np.float32)
    n_pairs = ATT_Q_HEADS // 2
    return pl.pallas_call(
        _bias_kernel,
        grid=(n_pairs,),
        in_specs=[pl.BlockSpec(memory_space=pltpu.SMEM),
                  pl.BlockSpec(bucket.shape, lambda p: (0, 0)),
                  pl.BlockSpec(mask.shape, lambda p: (0, 0, 0))],
        out_specs=pl.BlockSpec((3, 1, 2 * BLOCK, 3 * BLOCK), lambda p: (0, p, 0, 0)),
        out_shape=jax.ShapeDtypeStruct((3, n_pairs, 2 * BLOCK, 3 * BLOCK), F32),
        compiler_params=_cparams(("arbitrary",)),
        name="bias_tile",
    )(rel_bias.astype(F32), jnp.asarray(bucket), jnp.asarray(mask))


def _norm_mod(x, ng, sh, sc):
    ms = jnp.mean(x * x, axis=-1, keepdims=True)
    return (x * lax.rsqrt(ms + EPS)) * (ng * (1.0 + sc)) + sh


def _ffn_kernel(*refs, mix, casts):
    if mix:
        hg_ref, at_ref, gm_ref, wm_ref, *refs = refs
    x_ref, sh_ref, sc_ref, gt_ref, ng_ref, win_ref, wout_ref, *refs = refs
    src_refs, (o_ref, *dst_refs) = refs[:len(casts)], refs[len(casts):]
    x = x_ref[0]
    if mix:
        mixed = (jnp.dot(hg_ref[0], wm_ref[:HG_WIDTH, :], preferred_element_type=F32)
                 + jnp.dot(at_ref[0], wm_ref[HG_WIDTH:, :], preferred_element_type=F32))
        x = x + gm_ref[0] * mixed
    h = _norm_mod(x, ng_ref[...], sh_ref[0], sc_ref[0]).astype(BF16)
    gu = jnp.dot(h, win_ref[...], preferred_element_type=F32)
    g = gu[:, :D_FF]
    u = gu[:, D_FF:]
    act = (g * _sigmoid(g) * u).astype(BF16)
    y = jnp.dot(act, wout_ref[...], preferred_element_type=F32)
    o_ref[0] = x + 0.5 * gt_ref[0] * y
    for src_ref, dst_ref, col_map in zip(src_refs, dst_refs, casts):
        for dst0, src0, width in col_map:
            dst_ref[:, dst0:dst0 + width] = src_ref[0, :, src0:src0 + width].astype(BF16)


def _ffn(x, sh, sc, gt, ng, w_in, w_out, mix=None, casts=()):
    b, s, d = x.shape
    tm = FFN_TM
    n_steps = b * (s // tm)
    vec = pl.BlockSpec((1, 1, d), lambda i, j: (i, 0, 0))
    whole = lambda a: pl.BlockSpec(a.shape, lambda i, j: (0, 0), pipeline_mode=pl.Buffered(1))
    args = [x, sh, sc, gt, ng, w_in, w_out]
    specs = [pl.BlockSpec((1, tm, d), lambda i, j: (i, j, 0)), vec, vec, vec,
             whole(ng), whole(w_in), whole(w_out)]
    if mix is not None:
        o_hg, att, gm, wm = mix
        half = pl.BlockSpec((1, tm, HG_WIDTH), lambda i, j: (i, j, 0))
        args = [o_hg, att, gm, wm] + args
        specs = [half, half, vec, whole(wm)] + specs
    out_specs = [pl.BlockSpec((1, tm, d), lambda i, j: (i, j, 0))]
    out_shape = [jax.ShapeDtypeStruct(x.shape, F32)]
    col_maps = []
    for mats, layer, col_map, dup in casts:
        _, rows, cols = mats.shape
        assert n_steps % dup == 0
        n_slabs = n_steps // dup
        assert rows % (n_slabs * PACKED_ROWS) == 0 and cols % LANES == 0
        slab = rows // n_slabs
        args.append(mats)
        specs.append(pl.BlockSpec(
            (1, slab, cols), lambda i, j, layer=layer, dup=dup: (layer, (i * (s // tm) + j) // dup, 0)))
        out_specs.append(pl.BlockSpec(
            (slab, cols), lambda i, j, dup=dup: ((i * (s // tm) + j) // dup, 0)))
        out_shape.append(jax.ShapeDtypeStruct((rows, cols), BF16))
        col_maps.append(tuple(col_map) if col_map is not None else ((0, 0, cols),))
    out = pl.pallas_call(
        functools.partial(_ffn_kernel, mix=mix is not None, casts=tuple(col_maps)),
        grid=(b, s // tm),
        in_specs=specs,
        out_specs=out_specs,
        out_shape=out_shape,
        compiler_params=_cparams(("arbitrary", "arbitrary")),
        name="ffn_mix" if mix is not None else "ffn",
    )(*args)
    return out if casts else out[0]


def _half_norm(x, gain, lo):
    x2 = x * x
    s_lo = jnp.sum(jnp.where(lo, x2, 0.0), axis=-1, keepdims=True)
    s_hi = jnp.sum(jnp.where(lo, 0.0, x2), axis=-1, keepdims=True)
    ms = jnp.where(lo, s_lo, s_hi) * (1.0 / ATT_HEAD_DIM)
    return x * lax.rsqrt(ms + EPS) * gain


def _mixin_kernel(x_ref, sh_ref, sc_ref, ng_ref, w_ref, lb_ref, qg_ref, kg_ref,
                  hg_ref, aq_ref, ak_ref, av_ref, *, layer):
    h = _norm_mod(x_ref[0], ng_ref[...], sh_ref[0], sc_ref[0]).astype(BF16)

    def proj(c0, c1):
        return jnp.dot(h, w_ref[:, c0:c1], preferred_element_type=F32)

    part_w = HG_WIDTH // 2

    def put_heads(kind, val, part):
        for hh in range(part_w // HG_DIM):
            head = part * (part_w // HG_DIM) + hh
            hg_ref[0, kind * HG_HEADS + head] = val[:, hh * HG_DIM:(hh + 1) * HG_DIM].astype(BF16)

    raw = lb_ref[...]
    mx = jnp.max(raw, axis=1, keepdims=True)
    ex = jnp.exp(raw - mx)
    lb = jnp.sum(ex[:, :layer + 1, :], axis=1) / jnp.sum(ex, axis=1)

    def forget(d, z0, kk, kh, km, part):
        cols = slice(part * part_w, (part + 1) * part_w)
        fr = proj(z0 + cols.start, z0 + cols.stop)
        lbd = lb[d:d + 1, cols]
        f = lbd + (1.0 - lbd) * _sigmoid(fr)
        g2 = jnp.log(f) * LOG2E
        hi = g2.astype(BF16)
        put_heads(kk, 1.0 - f, part)
        put_heads(kh, hi, part)
        put_heads(km, g2 - hi.astype(F32), part)

    def plain(kind, z0, part, silu):
        val = proj(z0 + part * part_w, z0 + (part + 1) * part_w)
        put_heads(kind, val * _sigmoid(val) if silu else val, part)

    lo = lax.broadcasted_iota(jnp.int32, (h.shape[0], LANES), 1) < LANES // 2

    def queries(part):
        aq = proj(Z_AQ + part * part_w, Z_AQ + (part + 1) * part_w)
        for m in range(part_w // LANES):
            cols = slice(m * LANES, (m + 1) * LANES)
            aq_ref[0, :, part * part_w + m * LANES:part * part_w + (m + 1) * LANES] = (
                _half_norm(aq[:, cols], qg_ref[...], lo).astype(BF16))

    def keys_values():
        akv = proj(Z_AK, Z_I)
        ak_ref[0] = _half_norm(akv[:, :KV_WIDTH], kg_ref[...], lo).T.astype(BF16)
        av_ref[0] = akv[:, KV_WIDTH:].astype(BF16)

    forget(0, Z_FF, HK_KF, HK_GHF, HK_GMF, 0)
    plain(HK_I, Z_I, 0, False)
    forget(0, Z_FF, HK_KF, HK_GHF, HK_GMF, 1)
    plain(HK_I, Z_I, 1, False)
    forget(1, Z_FB, HK_KB, HK_GHB, HK_GMB, 0)
    keys_values()
    forget(1, Z_FB, HK_KB, HK_GHB, HK_GMB, 1)
    plain(HK_Q, Z_Q, 0, True)
    queries(0)
    plain(HK_Q, Z_Q, 1, True)
    queries(1)
    plain(HK_G, Z_G, 0, True)
    plain(HK_G, Z_G, 1, True)


def _mixin(x, sh, sc, ng, w, hgrn_lb, qg, kg, layer):
    b, s, d = x.shape
    tm = MIX_TM
    vec = pl.BlockSpec((1, 1, d), lambda i, j: (i, 0, 0))
    row = lambda n: pl.BlockSpec((1, n), lambda i, j: (0, 0))
    rows3 = lambda n: pl.BlockSpec((1, tm, n), lambda i, j: (i, j, 0))
    n_hg = N_HK * HG_HEADS
    return pl.pallas_call(
        functools.partial(_mixin_kernel, layer=layer),
        grid=(b, s // tm),
        in_specs=[pl.BlockSpec((1, tm, d), lambda i, j: (i, j, 0)),
                  vec, vec, row(d),
                  pl.BlockSpec(w.shape, lambda i, j: (0, 0)),
                  pl.BlockSpec(hgrn_lb.shape, lambda i, j: (0, 0, 0)),
                  row(LANES), row(LANES)],
        out_specs=[pl.BlockSpec((1, n_hg, tm, HG_DIM), lambda i, j: (i, 0, j, 0)),
                   rows3(ATT_WIDTH),
                   pl.BlockSpec((1, KV_WIDTH, tm), lambda i, j: (i, 0, j)),
                   rows3(KV_WIDTH)],
        out_shape=[jax.ShapeDtypeStruct((b, n_hg, s, HG_DIM), BF16),
                   jax.ShapeDtypeStruct((b, s, ATT_WIDTH), BF16),
                   jax.ShapeDtypeStruct((b, KV_WIDTH, s), BF16),
                   jax.ShapeDtypeStruct((b, s, KV_WIDTH), BF16)],
        compiler_params=_cparams(("arbitrary", "arbitrary")),
        name="mix_in",
    )(x, sh, sc, ng, w, hgrn_lb, qg, kg)


SMALL_W = (1, 2, 4, 8)
GROUP = PACKED_ROWS
FINISH_CHUNKS = 8
PAIRS_PER_BODY = 8


def _hgrn_consts(c):
    t = np.arange(c)[:, None]
    s = np.arange(c)[None, :]
    x = t ^ s
    lev = np.where(x > 0, np.floor(np.log2(np.maximum(x, 1))), -1).astype(np.int32)
    lev_f = np.where(t > s, lev, np.where(t == s, -1, -2)).astype(np.int32)
    lev_b = lev_f.T.copy()

    def exponent_rows(w, rev):
        m = np.zeros((GROUP, GROUP), np.float32)
        for r in range(GROUP):
            b0 = (r // (2 * w)) * 2 * w
            if not rev:
                ref = b0 + w - 1
                lo_u, hi_u = (ref + 1, r) if r > ref else (r + 1, ref)
            else:
                ref = b0 + w
                lo_u, hi_u = (r, ref - 1) if r < ref else (ref, r - 1)
            m[r, lo_u:hi_u + 1] = 1.0
        return m

    cums, smalls = [], []
    for rev in (False, True):
        cum = ((s >= t) if rev else (s <= t)).astype(np.float32)
        cums.append(np.concatenate([cum, cum], axis=1))
        rows = np.concatenate([exponent_rows(w, rev) for w in SMALL_W])
        smalls.append(np.concatenate([rows, rows], axis=1))
    return (jnp.asarray(np.stack([lev_f, lev_b])),
            jnp.asarray(np.stack(cums), dtype=BF16),
            jnp.asarray(np.stack(smalls), dtype=BF16))


def _neg_abs(x):
    bits = lax.bitcast_convert_type(x, jnp.uint32) | jnp.uint32(0x80000000)
    return lax.bitcast_convert_type(bits, F32)


def _ref_rows(g_cum, w, rev, c):
    idx = w if rev else w - 1
    g3 = g_cum.reshape(c // (2 * w), 2 * w, LANES)
    return jnp.broadcast_to(g3[:, idx:idx + 1, :], g3.shape).reshape(c, LANES)


def _hgrn_stage_a(tcum, tsmall, gh, gm, gc_ref, ge_ref, c):
    hm = jnp.concatenate([gh, gm], axis=0)
    half = c // 2
    for r in range(2):
        rows = slice(r * half, (r + 1) * half)
        gc_ref[rows, :] = jnp.dot(tcum[rows], hm, preferred_element_type=F32)
        yield
    per_tile = MXU_COLS // LANES
    for t in range(c // GROUP // per_tile):
        groups = [slice(g * GROUP, (g + 1) * GROUP) for g in range(t * per_tile, (t + 1) * per_tile)]
        wide = jnp.concatenate([jnp.concatenate([gh[g] for g in groups], axis=1),
                                jnp.concatenate([gm[g] for g in groups], axis=1)], axis=0)
        ge_ref[:, t * MXU_COLS:(t + 1) * MXU_COLS] = jnp.dot(tsmall, wide,
                                                              preferred_element_type=F32)
        yield


def _hgrn_stage_b(q, k, v, gc_ref, ge_ref, lev, st_ref, a_ref, ab_ref, oi_ref, rev, c):
    tn = (((0,), (0,)), ((), ()))
    g_cum = gc_ref[...]
    g_last = g_cum[0:1, :] if rev else g_cum[c - 1:c, :]
    qf, kf = q.astype(F32), k.astype(F32)

    st = st_ref[...]
    qi = (qf * jnp.exp2(g_cum)).astype(BF16)
    oi_ref[...] = jnp.dot(qi, st.T.astype(BF16), preferred_element_type=F32)
    kd = (kf * jnp.exp2(g_last - g_cum)).astype(BF16)
    u_t = lax.dot_general(v, kd, tn, preferred_element_type=F32)
    st_ref[...] = jnp.exp2(g_last) * st + u_t
    yield

    w = c // 2
    while w >= GROUP:
        x = jnp.exp2(_neg_abs(g_cum - _ref_rows(g_cum, w, rev, c)))
        q_rows, k_rows = [], []
        for b in range(c // (2 * w)):
            lo_half = slice(b * 2 * w, b * 2 * w + w)
            hi_half = slice(b * 2 * w + w, (b + 1) * 2 * w)
            qs = lo_half if rev else hi_half
            q_rows.append(qf[qs] * x[qs])
            k_rows.extend([kf[lo_half], kf[hi_half] * x[hi_half]] if rev
                          else [kf[lo_half] * x[lo_half], kf[hi_half]])
        p = jnp.dot(jnp.concatenate(q_rows, axis=0).astype(BF16),
                    jnp.concatenate(k_rows, axis=0).T.astype(BF16),
                    preferred_element_type=F32)
        for b in range(c // (2 * w)):
            lo_half = slice(b * 2 * w, b * 2 * w + w)
            hi_half = slice(b * 2 * w + w, (b + 1) * 2 * w)
            qs, ks = (lo_half, hi_half) if rev else (hi_half, lo_half)
            a_ref[qs, ks] = p[b * w:(b + 1) * w, ks]
        w //= 2
        yield

    a = jnp.where(lev == -1, jnp.dot(q, kf.T.astype(BF16), preferred_element_type=F32), 0.0)
    for li in range(len(SMALL_W)):
        e = jnp.concatenate([ge_ref[li * GROUP:(li + 1) * GROUP, g * LANES:(g + 1) * LANES]
                             for g in range(c // GROUP)], axis=0)
        x = jnp.exp2(e)
        p = jnp.dot((qf * x).astype(BF16), (kf * x).T.astype(BF16),
                    preferred_element_type=F32)
        a = jnp.where(lev == li, p, a)
        yield
    for b in range(c // GROUP):
        blk = slice(b * GROUP, (b + 1) * GROUP)
        a_ref[blk, blk] = a[blk, blk]
    ab_ref[...] = a_ref[...].astype(BF16)
    yield


def _interleave(main, fill):
    for _ in main:
        next(fill, None)
    for _ in fill:
        pass


def _hgrn_kernel(q_ref, i_ref, sg_ref, kf_ref, kb_ref, ghf_ref, gmf_ref, ghb_ref, gmb_ref,
                 ng_ref, lev_ref, tc_ref, ts_ref, o_ref,
                 acc_ref, st_ref, a_ref, ab_ref, oi_ref, gc_ref, ge_ref, *, c, nc):
    for ref in (acc_ref, st_ref, a_ref, ab_ref, oi_ref):
        ref[...] = jnp.zeros_like(ref)
    k_refs = (kf_ref, kb_ref)
    g_refs = ((ghf_ref, gmf_ref), (ghb_ref, gmb_ref))

    def rows_of(p, d):
        j = p if d == 0 else nc - 1 - p
        return pl.ds(pl.multiple_of(j * c, c), c)

    def stage_a(p, slot, d):
        rows = rows_of(p, d)
        return _hgrn_stage_a(tc_ref[d], ts_ref[d], g_refs[d][0][0, 0, rows, :],
                             g_refs[d][1][0, 0, rows, :], gc_ref.at[slot, d], ge_ref.at[slot, d], c)

    def stage_t(p, d):
        rows = rows_of(p, d)
        acc_ref[rows, :] += oi_ref[d] + jnp.dot(ab_ref[d], i_ref[0, 0, rows, :],
                                                preferred_element_type=F32)

    def stage_b(p, slot, d):
        rows = rows_of(p, d)
        return _hgrn_stage_b(q_ref[0, 0, rows, :], k_refs[d][0, 0, rows, :], i_ref[0, 0, rows, :],
                             gc_ref.at[slot, d], ge_ref.at[slot, d], lev_ref[d], st_ref.at[d],
                             a_ref.at[d], ab_ref.at[d], oi_ref.at[d], d == 1, c)

    def pair(p, slot):
        p_next, p_prev = jnp.minimum(p + 1, nc - 1), jnp.maximum(p - 1, 0)
        for d in range(2):
            stage_t(p_prev, d)
            _interleave(stage_b(p, slot, d), stage_a(p_next, 1 - slot, d))

    def pairs(i, carry):
        for u in range(PAIRS_PER_BODY):
            pair(PAIRS_PER_BODY * i + u, u % 2)
        return carry

    for d in range(2):
        for _ in stage_a(0, 0, d):
            pass
    lax.fori_loop(0, nc // PAIRS_PER_BODY, pairs, 0)
    for d in range(2):
        stage_t(nc - 1, d)

    ng = ng_ref[...]
    rows_fin = FINISH_CHUNKS * c

    def finish(j, carry):
        rows = pl.ds(pl.multiple_of(j * rows_fin, rows_fin), rows_fin)
        o = acc_ref[rows, :]
        ms = jnp.mean(o * o, axis=-1, keepdims=True)
        o = o * lax.rsqrt(ms + EPS) * ng
        o_ref[0, rows, :] = (o * sg_ref[0, 0, rows, :].astype(F32)).astype(o_ref.dtype)
        return carry

    lax.fori_loop(0, nc // FINISH_CHUNKS, finish, 0)


def _hgrn(hg, norm_g):
    b, _, s, _ = hg.shape
    c = HG_CHUNK
    nc = s // c
    assert PAIRS_PER_BODY % 2 == 0 and nc % PAIRS_PER_BODY == 0 and c % (2 * GROUP) == 0
    lev, tcum, tsmall = _hgrn_consts(c)
    n_small = len(SMALL_W) * GROUP

    def kind(k):
        return pl.BlockSpec((1, 1, s, HG_DIM), lambda i, h: (i, k * HG_HEADS + h, 0, 0))

    const3 = lambda a: pl.BlockSpec(a.shape, lambda i, h: (0, 0, 0))
    return pl.pallas_call(
        functools.partial(_hgrn_kernel, c=c, nc=nc),
        grid=(b, HG_HEADS),
        in_specs=[kind(HK_Q), kind(HK_I), kind(HK_G), kind(HK_KF), kind(HK_KB),
                  kind(HK_GHF), kind(HK_GMF), kind(HK_GHB), kind(HK_GMB),
                  pl.BlockSpec((1, LANES), lambda i, h: (0, h)),
                  const3(lev), const3(tcum), const3(tsmall)],
        out_specs=pl.BlockSpec((1, s, LANES), lambda i, h: (i, 0, h)),
        out_shape=jax.ShapeDtypeStruct((b, s, HG_WIDTH), BF16),
        scratch_shapes=[pltpu.VMEM((s, HG_DIM), F32),
                        pltpu.VMEM((2, HG_DIM, HG_DIM), F32),
                        pltpu.VMEM((2, c, c), F32),
                        pltpu.VMEM((2, c, c), BF16),
                        pltpu.VMEM((2, c, HG_DIM), F32),
                        pltpu.VMEM((2, 2, c, LANES), F32),
                        pltpu.VMEM((2, 2, n_small, (c // GROUP) * LANES), F32)],
        compiler_params=_cparams(("arbitrary", "arbitrary")),
        name="hgrn",
    )(hg, hg, hg, hg, hg, hg, hg, hg, hg, norm_g, lev, tcum, tsmall)


def _attn_kernel(sink_ref, q_ref, kp_ref, ko_ref, kn_ref, vp_ref, vo_ref, vn_ref,
                 bias_ref, o_ref, lg_ref, *, nb):
    n = pl.program_id(1)
    half = LANES // 2
    nk = 3 * BLOCK
    kt = jnp.concatenate([kp_ref[0], ko_ref[0], kn_ref[0]], axis=1)
    vb = jnp.concatenate([vp_ref[0], vo_ref[0], vn_ref[0]], axis=0)
    lo = lax.broadcasted_iota(jnp.int32, (BLOCK, LANES), 1) < half
    lo_v = lax.broadcasted_iota(jnp.int32, vb.shape, 1) < half
    zero_v = jnp.zeros_like(vb)
    zero_k = jnp.zeros((half, nk), kt.dtype)

    k_both = tuple(
        jnp.concatenate([jnp.concatenate([kj, zero_k], axis=0),
                         jnp.concatenate([zero_k, kj], axis=0)], axis=1)
        for kj in (kt[:half], kt[half:]))
    v0lo, v1hi = jnp.where(lo_v, vb, zero_v), jnp.where(lo_v, zero_v, vb)
    v_both = (jnp.concatenate([v0lo, pltpu.roll(v0lo, half, axis=1)], axis=0),
              jnp.concatenate([pltpu.roll(v1hi, half, axis=1), v1hi], axis=0))
    edge = jnp.where(n == 0, 0, jnp.where(n == nb - 1, 2, 1))

    for m in range(ATT_Q_HEADS // 2):
        lg_ref[m] = jnp.dot(q_ref[0, :, m * LANES:(m + 1) * LANES], k_both[(2 * m) // ATT_GROUP],
                            preferred_element_type=F32)

    for m in range(ATT_Q_HEADS // 2):
        ps, rdens = [], []
        for hh in range(2):
            lg = (lg_ref[m, :, hh * nk:(hh + 1) * nk]
                  + bias_ref[edge, m, hh * BLOCK:(hh + 1) * BLOCK, :])
            sink = sink_ref[2 * m + hh]
            mx = jnp.maximum(jnp.max(lg, axis=-1, keepdims=True), sink)
            p = jnp.exp2(lg - mx)
            rdens.append(1.0 / (jnp.sum(p, axis=-1, keepdims=True) + jnp.exp2(sink - mx)))
            ps.append(p.astype(BF16))
        o = jnp.dot(jnp.concatenate(ps, axis=1), v_both[(2 * m) // ATT_GROUP],
                    preferred_element_type=F32)
        o_ref[0, :, m * LANES:(m + 1) * LANES] = (
            o * jnp.where(lo, rdens[0], rdens[1])).astype(o_ref.dtype)


def _attn(aq, ak, av, sink2, bias):
    b, s, _ = aq.shape
    nb = s // BLOCK
    assert nb >= 2

    def neighbour(n, shift):
        return jnp.clip(n + shift, 0, nb - 1)

    kt = lambda shift: pl.BlockSpec((1, KV_WIDTH, BLOCK), lambda i, n: (i, 0, neighbour(n, shift)))
    v = lambda shift: pl.BlockSpec((1, BLOCK, KV_WIDTH), lambda i, n: (i, neighbour(n, shift), 0))
    return pl.pallas_call(
        functools.partial(_attn_kernel, nb=nb),
        grid=(b, nb),
        in_specs=[pl.BlockSpec(memory_space=pltpu.SMEM),
                  pl.BlockSpec((1, BLOCK, ATT_WIDTH), lambda i, n: (i, n, 0)),
                  kt(-1), kt(0), kt(1), v(-1), v(0), v(1),
                  pl.BlockSpec(bias.shape, lambda i, n: (0, 0, 0, 0))],
        out_specs=pl.BlockSpec((1, BLOCK, ATT_WIDTH), lambda i, n: (i, n, 0)),
        out_shape=jax.ShapeDtypeStruct((b, s, ATT_WIDTH), BF16),
        scratch_shapes=[pltpu.VMEM((ATT_Q_HEADS // 2, BLOCK, 2 * 3 * BLOCK), F32)],
        compiler_params=_cparams(("arbitrary", "arbitrary")),
        name="attn",
    )(sink2, aq, ak, ak, ak, av, av, av, bias)


def _mixin_col_map():
    hw = HG_WIDTH
    att = ATT_WIDTH + 2 * KV_WIDTH
    return ((Z_FF, hw, 2 * hw), (Z_Q, 0, hw), (Z_G, 4 * hw, hw), (Z_AQ, 5 * hw, att),
            (Z_I, 3 * hw, hw))


def kernel(x, c, w_ada, b_ada, norm_g, w_ffn1_in, w_ffn1_out, w_ffn2_in, w_ffn2_out,
           w_mix_in, w_mix_out, hgrn_lb, hgrn_norm_g, qk_norm_g, attn_sink, rel_bias):
    b, s, d = x.shape
    depth = w_ada.shape[0]
    bias = _bias_tile(rel_bias)
    for l in range(depth):
        mods = _ada(c.astype(F32), w_ada[l], b_ada[l][None, :])
        sh1, sc1, g1, sh2, sc2, g2, sh3, sc3, g3 = [
            mods[:, i * d:(i + 1) * d][:, None, :] for i in range(N_MOD)]
        x, w2_in, w2_out, w_mi, w_mo = _ffn(
            x, sh1, sc1, g1, norm_g[l, 0][None, :],
            w_ffn1_in[l].astype(BF16), w_ffn1_out[l].astype(BF16),
            casts=((w_ffn2_in, l, None, 1), (w_ffn2_out, l, None, 2),
                   (w_mix_in, l, _mixin_col_map(), 1), (w_mix_out, l, None, 1)))
        qg = (jnp.tile(qk_norm_g[l, 0], 2) * (LOG2E / math.sqrt(ATT_HEAD_DIM)))[None, :]
        kg = jnp.tile(qk_norm_g[l, 1], 2)[None, :]
        hg, aq, ak, av = _mixin(x, sh2, sc2, norm_g[l, 1][None, :], w_mi, hgrn_lb, qg, kg, l)
        o_hg = _hgrn(hg, hgrn_norm_g[l][None, :])
        att = _attn(aq, ak, av, attn_sink[l] * LOG2E, bias)
        x = _ffn(x, sh3, sc3, g3, norm_g[l, 2][None, :],
                 w2_in, w2_out, mix=(o_hg, att, g2, w_mo))
    return x
```

```python
import functools
import math

import numpy as np
import jax
import jax.numpy as jnp
from jax import lax
from jax.experimental import pallas as pl
from jax.experimental.pallas import tpu as pltpu

F32 = jnp.float32
BF16 = jnp.bfloat16

HG_HEADS = 4
HG_DIM = 128
HG_WIDTH = HG_HEADS * HG_DIM
ATT_Q_HEADS = 8
ATT_KV_HEADS = 2
ATT_HEAD_DIM = 64
ATT_GROUP = ATT_Q_HEADS // ATT_KV_HEADS
ATT_WIDTH = ATT_Q_HEADS * ATT_HEAD_DIM
KV_WIDTH = ATT_KV_HEADS * ATT_HEAD_DIM
WINDOW = 128
BLOCK = 128
NUM_BUCKETS = 32
MAX_DISTANCE = 128
D_FF = 2816
N_MOD = 9
EPS = 1e-6

LANES = 128
PACKED_ROWS = 16
MXU_COLS = 256
VMEM_LIMIT = 56 * 1024 * 1024

ADA_TN = 2304
FFN_TM = 512
FFN_LOAD_STEPS = 16
MIX_TM = 256
HG_CHUNK = 128
NEG_INF = float("-inf")
LOG2E = 1.0 / math.log(2.0)

Z_FF, Z_FB, Z_Q, Z_G, Z_AQ = (n * HG_WIDTH for n in range(5))
Z_AK = Z_AQ + ATT_WIDTH
Z_AV = Z_AK + KV_WIDTH
Z_I = Z_AV + KV_WIDTH
Z_END = Z_I + HG_WIDTH
(HK_Q, HK_I, HK_G, HK_KF, HK_KB, HK_GHF, HK_GMF, HK_GHB, HK_GMB) = range(9)
N_HK = 9


def _cparams(sem):
    return pltpu.CompilerParams(dimension_semantics=sem, vmem_limit_bytes=VMEM_LIMIT)


def _sigmoid(x):
    return 1.0 / (1.0 + jnp.exp2(x * (-LOG2E)))


def _ada_kernel(ct_ref, w_ref, b_ref, o_ref):
    ct = ct_ref[...]
    ca = ct * _sigmoid(ct)
    w = w_ref[...]
    for b in range(ct.shape[1]):
        o_ref[b:b + 1, :] = jnp.sum(w * ca[:, b:b + 1], axis=0, keepdims=True) + b_ref[...]


def _ada(c, w, b):
    batch, d = c.shape
    n = w.shape[1]
    tn = ADA_TN
    return pl.pallas_call(
        _ada_kernel,
        grid=(n // tn,),
        in_specs=[pl.BlockSpec((d, batch), lambda j: (0, 0)),
                  pl.BlockSpec((d, tn), lambda j: (0, j)),
                  pl.BlockSpec((1, tn), lambda j: (0, j))],
        out_specs=pl.BlockSpec((batch, tn), lambda j: (0, j)),
        out_shape=jax.ShapeDtypeStruct((batch, n), F32),
        compiler_params=_cparams(("arbitrary",)),
        name="ada",
    )(c.T, w, b)


def _t5_bucket_np(rel):
    nb = NUM_BUCKETS // 2
    max_exact = nb // 2
    ret = (rel > 0).astype(np.int32) * nb
    n = np.abs(rel)
    ratio = np.maximum(n, 1).astype(np.float32) / np.float32(max_exact)
    large = max_exact + (np.log(ratio) / np.float32(math.log(MAX_DISTANCE / max_exact))
                         * np.float32(nb - max_exact)).astype(np.int32)
    large = np.minimum(large, nb - 1)
    return ret + np.where(n < max_exact, n, large)


def _bias_kernel(rb_ref, bucket_ref, mask_ref, o_ref):
    pair = pl.program_id(0)
    bucket = bucket_ref[...]
    for hh in range(2):
        head = 2 * pair + hh
        tile = jnp.zeros(bucket.shape, F32)
        for b in range(NUM_BUCKETS):
            tile = jnp.where(bucket == b, rb_ref[b, head], tile)
        tile = tile * LOG2E
        for e in range(3):
            o_ref[e, 0, hh * BLOCK:(hh + 1) * BLOCK, :] = tile + mask_ref[e]


def _bias_tile(rel_bias):
    kcol = np.arange(3 * BLOCK)[None, :]
    rel = (kcol - BLOCK) - np.arange(BLOCK)[:, None]
    bucket = _t5_bucket_np(rel).astype(np.int32)
    window = np.abs(rel) <= WINDOW
    valid = np.stack([window & (kcol >= BLOCK), window, window & (kcol < 2 * BLOCK)])
    mask = np.where(valid, 0.0, NEG_INF).astype(np.float32)
    n_pairs = ATT_Q_HEADS // 2
    return pl.pallas_call(
        _bias_kernel,
        grid=(n_pairs,),
        in_specs=[pl.BlockSpec(memory_space=pltpu.SMEM),
                  pl.BlockSpec(bucket.shape, lambda p: (0, 0)),
                  pl.BlockSpec(mask.shape, lambda p: (0, 0, 0))],
        out_specs=pl.BlockSpec((3, 1, 2 * BLOCK, 3 * BLOCK), lambda p: (0, p, 0, 0)),
        out_shape=jax.ShapeDtypeStruct((3, n_pairs, 2 * BLOCK, 3 * BLOCK), F32),
        compiler_params=_cparams(("arbitrary",)),
        name="bias_tile",
    )(rel_bias.astype(F32), jnp.asarray(bucket), jnp.asarray(mask))


def _norm_mod(x, ng, sh, sc):
    ms = jnp.mean(x * x, axis=-1, keepdims=True)
    return (x * lax.rsqrt(ms + EPS)) * (ng * (1.0 + sc)) + sh


def _ffn_tile(x, sh_ref, sc_ref, gt_ref, ng_ref, win_ref, wout_ref, o_ref):
    h = _norm_mod(x, ng_ref[...], sh_ref[0], sc_ref[0]).astype(BF16)
    gu = jnp.dot(h, win_ref[...], preferred_element_type=F32)
    g = gu[:, :D_FF]
    u = gu[:, D_FF:]
    act = (g * _sigmoid(g) * u).astype(BF16)
    y = jnp.dot(act, wout_ref[...], preferred_element_type=F32)
    o_ref[0] = x + 0.5 * gt_ref[0] * y


def _ffn_first_kernel(x_ref, sh_ref, sc_ref, gt_ref, ng_ref, win_ref, wout_ref, *refs,
                      n_load, casts):
    src_refs = refs[:len(casts)]
    o_ref, *dst_refs = refs[len(casts):len(casts) + 1 + len(casts)]
    win_s, wout_s = refs[-2:]
    p = pl.program_id(0)

    @pl.when(p < n_load)
    def _():
        for slab_ref, full in ((win_ref, win_s), (wout_ref, wout_s)):
            rows = slab_ref.shape[1]
            full[pl.ds(pl.multiple_of(p * rows, rows), rows), :] = slab_ref[0].astype(BF16)

    @pl.when(p >= n_load)
    def _():
        _ffn_tile(x_ref[0], sh_ref, sc_ref, gt_ref, ng_ref, win_s, wout_s, o_ref)
        for src_ref, dst_ref, col_map in zip(src_refs, dst_refs, casts):
            for dst0, src0, width in col_map:
                dst_ref[:, dst0:dst0 + width] = src_ref[0, :, src0:src0 + width].astype(BF16)


def _ffn_first(x, sh, sc, gt, ng, w_in, w_out, layer, casts):
    b, s, d = x.shape
    tm = FFN_TM
    per_b = s // tm
    n_tiles = b * per_b
    n_load = FFN_LOAD_STEPS

    def tile(p):
        return jnp.maximum(p - n_load, 0)

    def own_slab(mats):
        rows = mats.shape[1]
        assert rows % (n_load * PACKED_ROWS) == 0
        return pl.BlockSpec((1, rows // n_load, mats.shape[2]),
                            lambda p: (layer, jnp.minimum(p, n_load - 1), 0))

    rows_spec = pl.BlockSpec((1, tm, d), lambda p: (tile(p) // per_b, tile(p) % per_b, 0))
    vec = pl.BlockSpec((1, 1, d), lambda p: (tile(p) // per_b, 0, 0))
    args = [x, sh, sc, gt, ng, w_in, w_out]
    specs = [rows_spec, vec, vec, vec, pl.BlockSpec(ng.shape, lambda p: (0, 0)),
             own_slab(w_in), own_slab(w_out)]
    out_specs = [rows_spec]
    out_shape = [jax.ShapeDtypeStruct(x.shape, F32)]
    col_maps = []
    for mats, col_map, dup in casts:
        _, rows, cols = mats.shape
        assert n_tiles % dup == 0
        n_slabs = n_tiles // dup
        assert rows % (n_slabs * PACKED_ROWS) == 0 and cols % LANES == 0
        slab = rows // n_slabs
        args.append(mats)
        specs.append(pl.BlockSpec((1, slab, cols), lambda p, dup=dup: (layer, tile(p) // dup, 0)))
        out_specs.append(pl.BlockSpec((slab, cols), lambda p, dup=dup: (tile(p) // dup, 0)))
        out_shape.append(jax.ShapeDtypeStruct((rows, cols), BF16))
        col_maps.append(tuple(col_map) if col_map is not None else ((0, 0, cols),))
    return pl.pallas_call(
        functools.partial(_ffn_first_kernel, n_load=n_load, casts=tuple(col_maps)),
        grid=(n_load + n_tiles,),
        in_specs=specs,
        out_specs=out_specs,
        out_shape=out_shape,
        scratch_shapes=[pltpu.VMEM(w_in.shape[1:], BF16), pltpu.VMEM(w_out.shape[1:], BF16)],
        compiler_params=_cparams(("arbitrary",)),
        name="ffn",
    )(*args)


def _ffn_mix_kernel(hg_ref, at_ref, gm_ref, wm_ref, x_ref, sh_ref, sc_ref, gt_ref, ng_ref,
                    win_ref, wout_ref, o_ref):
    mixed = (jnp.dot(hg_ref[0], wm_ref[:HG_WIDTH, :], preferred_element_type=F32)
             + jnp.dot(at_ref[0], wm_ref[HG_WIDTH:, :], preferred_element_type=F32))
    x = x_ref[0] + gm_ref[0] * mixed
    _ffn_tile(x, sh_ref, sc_ref, gt_ref, ng_ref, win_ref, wout_ref, o_ref)


def _ffn_mix(o_hg, att, gm, wm, x, sh, sc, gt, ng, w_in, w_out):
    b, s, d = x.shape
    tm = FFN_TM
    vec = pl.BlockSpec((1, 1, d), lambda i, j: (i, 0, 0))
    rows = lambda n: pl.BlockSpec((1, tm, n), lambda i, j: (i, j, 0))
    whole = lambda a: pl.BlockSpec(a.shape, lambda i, j: (0, 0), pipeline_mode=pl.Buffered(1))
    return pl.pallas_call(
        _ffn_mix_kernel,
        grid=(b, s // tm),
        in_specs=[rows(HG_WIDTH), rows(ATT_WIDTH), vec, whole(wm), rows(d), vec, vec, vec,
                  whole(ng), whole(w_in), whole(w_out)],
        out_specs=rows(d),
        out_shape=jax.ShapeDtypeStruct(x.shape, F32),
        compiler_params=_cparams(("arbitrary", "arbitrary")),
        name="ffn_mix",
    )(o_hg, att, gm, wm, x, sh, sc, gt, ng, w_in, w_out)


def _half_norm(x, gain, lo):
    x2 = x * x
    s_lo = jnp.sum(jnp.where(lo, x2, 0.0), axis=-1, keepdims=True)
    s_hi = jnp.sum(jnp.where(lo, 0.0, x2), axis=-1, keepdims=True)
    ms = jnp.where(lo, s_lo, s_hi) * (1.0 / ATT_HEAD_DIM)
    return x * lax.rsqrt(ms + EPS) * gain


def _mixin_kernel(x_ref, sh_ref, sc_ref, ng_ref, w_ref, lb_ref, qg_ref, kg_ref,
                  hg_ref, aq_ref, ak_ref, av_ref, *, layer):
    h = _norm_mod(x_ref[0], ng_ref[...], sh_ref[0], sc_ref[0]).astype(BF16)

    def proj(c0, c1):
        return jnp.dot(h, w_ref[:, c0:c1], preferred_element_type=F32)

    part_w = HG_WIDTH // 2

    def put_heads(kind, val, part):
        for hh in range(part_w // HG_DIM):
            head = part * (part_w // HG_DIM) + hh
            hg_ref[0, kind * HG_HEADS + head] = val[:, hh * HG_DIM:(hh + 1) * HG_DIM].astype(BF16)

    raw = lb_ref[...]
    mx = jnp.max(raw, axis=1, keepdims=True)
    ex = jnp.exp(raw - mx)
    lb = jnp.sum(ex[:, :layer + 1, :], axis=1) / jnp.sum(ex, axis=1)

    def forget(d, z0, kk, kh, km, part):
        cols = slice(part * part_w, (part + 1) * part_w)
        fr = proj(z0 + cols.start, z0 + cols.stop)
        lbd = lb[d:d + 1, cols]
        f = lbd + (1.0 - lbd) * _sigmoid(fr)
        g2 = jnp.log(f) * LOG2E
        hi = g2.astype(BF16)
        put_heads(kk, 1.0 - f, part)
        put_heads(kh, hi, part)
        put_heads(km, g2 - hi.astype(F32), part)

    def plain(kind, z0, part, silu):
        val = proj(z0 + part * part_w, z0 + (part + 1) * part_w)
        put_heads(kind, val * _sigmoid(val) if silu else val, part)

    lo = lax.broadcasted_iota(jnp.int32, (h.shape[0], LANES), 1) < LANES // 2

    def queries(part):
        aq = proj(Z_AQ + part * part_w, Z_AQ + (part + 1) * part_w)
        for m in range(part_w // LANES):
            cols = slice(m * LANES, (m + 1) * LANES)
            aq_ref[0, :, part * part_w + m * LANES:part * part_w + (m + 1) * LANES] = (
                _half_norm(aq[:, cols], qg_ref[...], lo).astype(BF16))

    def keys_values():
        akv = proj(Z_AK, Z_I)
        ak_ref[0] = _half_norm(akv[:, :KV_WIDTH], kg_ref[...], lo).T.astype(BF16)
        av_ref[0] = akv[:, KV_WIDTH:].astype(BF16)

    forget(0, Z_FF, HK_KF, HK_GHF, HK_GMF, 0)
    plain(HK_I, Z_I, 0, False)
    forget(0, Z_FF, HK_KF, HK_GHF, HK_GMF, 1)
    plain(HK_I, Z_I, 1, False)
    forget(1, Z_FB, HK_KB, HK_GHB, HK_GMB, 0)
    keys_values()
    forget(1, Z_FB, HK_KB, HK_GHB, HK_GMB, 1)
    plain(HK_Q, Z_Q, 0, True)
    queries(0)
    plain(HK_Q, Z_Q, 1, True)
    queries(1)
    plain(HK_G, Z_G, 0, True)
    plain(HK_G, Z_G, 1, True)


def _mixin(x, sh, sc, ng, w, hgrn_lb, qg, kg, layer):
    b, s, d = x.shape
    tm = MIX_TM
    vec = pl.BlockSpec((1, 1, d), lambda i, j: (i, 0, 0))
    row = lambda n: pl.BlockSpec((1, n), lambda i, j: (0, 0))
    rows3 = lambda n: pl.BlockSpec((1, tm, n), lambda i, j: (i, j, 0))
    n_hg = N_HK * HG_HEADS
    return pl.pallas_call(
        functools.partial(_mixin_kernel, layer=layer),
        grid=(b, s // tm),
        in_specs=[pl.BlockSpec((1, tm, d), lambda i, j: (i, j, 0)),
                  vec, vec, row(d),
                  pl.BlockSpec(w.shape, lambda i, j: (0, 0)),
                  pl.BlockSpec(hgrn_lb.shape, lambda i, j: (0, 0, 0)),
                  row(LANES), row(LANES)],
        out_specs=[pl.BlockSpec((1, n_hg, tm, HG_DIM), lambda i, j: (i, 0, j, 0)),
                   rows3(ATT_WIDTH),
                   pl.BlockSpec((1, KV_WIDTH, tm), lambda i, j: (i, 0, j)),
                   rows3(KV_WIDTH)],
        out_shape=[jax.ShapeDtypeStruct((b, n_hg, s, HG_DIM), BF16),
                   jax.ShapeDtypeStruct((b, s, ATT_WIDTH), BF16),
                   jax.ShapeDtypeStruct((b, KV_WIDTH, s), BF16),
                   jax.ShapeDtypeStruct((b, s, KV_WIDTH), BF16)],
        compiler_params=_cparams(("arbitrary", "arbitrary")),
        name="mix_in",
    )(x, sh, sc, ng, w, hgrn_lb, qg, kg)


SMALL_W = (1, 2, 4, 8)
GROUP = PACKED_ROWS
FINISH_CHUNKS = 8
PAIRS_PER_BODY = 16


def _hgrn_consts(c):
    t = np.arange(c)[:, None]
    s = np.arange(c)[None, :]
    x = t ^ s
    lev = np.where(x > 0, np.floor(np.log2(np.maximum(x, 1))), -1).astype(np.int32)
    lev_f = np.where(t > s, lev, np.where(t == s, -1, -2)).astype(np.int32)
    lev_b = lev_f.T.copy()

    def exponent_rows(w, rev):
        m = np.zeros((GROUP, GROUP), np.float32)
        for r in range(GROUP):
            b0 = (r // (2 * w)) * 2 * w
            if not rev:
                ref = b0 + w - 1
                lo_u, hi_u = (ref + 1, r) if r > ref else (r + 1, ref)
            else:
                ref = b0 + w
                lo_u, hi_u = (r, ref - 1) if r < ref else (ref, r - 1)
            m[r, lo_u:hi_u + 1] = 1.0
        return m

    cums, smalls = [], []
    for rev in (False, True):
        cum = ((s >= t) if rev else (s <= t)).astype(np.float32)
        cums.append(np.concatenate([cum, cum], axis=1))
        rows = np.concatenate([exponent_rows(w, rev) for w in SMALL_W])
        smalls.append(np.concatenate([rows, rows], axis=1))
    return (jnp.asarray(np.stack([lev_f, lev_b])),
            jnp.asarray(np.stack(cums), dtype=BF16),
            jnp.asarray(np.stack(smalls), dtype=BF16))


def _neg_abs(x):
    bits = lax.bitcast_convert_type(x, jnp.uint32) | jnp.uint32(0x80000000)
    return lax.bitcast_convert_type(bits, F32)


def _ref_rows(g_cum, w, rev, c):
    idx = w if rev else w - 1
    g3 = g_cum.reshape(c // (2 * w), 2 * w, LANES)
    return jnp.broadcast_to(g3[:, idx:idx + 1, :], g3.shape).reshape(c, LANES)


def _hgrn_stage_a(tcum, tsmall, gh, gm, gc_ref, ge_ref, c):
    hm = jnp.concatenate([gh, gm], axis=0)
    half = c // 2
    for r in range(2):
        rows = slice(r * half, (r + 1) * half)
        gc_ref[rows, :] = jnp.dot(tcum[rows], hm, preferred_element_type=F32)
        yield
    per_tile = MXU_COLS // LANES
    for t in range(c // GROUP // per_tile):
        groups = [slice(g * GROUP, (g + 1) * GROUP) for g in range(t * per_tile, (t + 1) * per_tile)]
        wide = jnp.concatenate([jnp.concatenate([gh[g] for g in groups], axis=1),
                                jnp.concatenate([gm[g] for g in groups], axis=1)], axis=0)
        ge_ref[:, t * MXU_COLS:(t + 1) * MXU_COLS] = jnp.dot(tsmall, wide,
                                                              preferred_element_type=F32)
        yield


def _hgrn_stage_b(q, k, v, gc_ref, ge_ref, lev, st_ref, a_ref, ab_ref, oi_ref, rev, c):
    tn = (((0,), (0,)), ((), ()))
    g_cum = gc_ref[...]
    g_last = g_cum[0:1, :] if rev else g_cum[c - 1:c, :]
    qf, kf = q.astype(F32), k.astype(F32)

    st = st_ref[...]
    qi = (qf * jnp.exp2(g_cum)).astype(BF16)
    oi_ref[...] = jnp.dot(qi, st.T.astype(BF16), preferred_element_type=F32)
    kd = (kf * jnp.exp2(g_last - g_cum)).astype(BF16)
    u_t = lax.dot_general(v, kd, tn, preferred_element_type=F32)
    st_ref[...] = jnp.exp2(g_last) * st + u_t
    yield

    w = c // 2
    while w >= GROUP:
        x = jnp.exp2(_neg_abs(g_cum - _ref_rows(g_cum, w, rev, c)))
        q_rows, k_rows = [], []
        for b in range(c // (2 * w)):
            lo_half = slice(b * 2 * w, b * 2 * w + w)
            hi_half = slice(b * 2 * w + w, (b + 1) * 2 * w)
            qs = lo_half if rev else hi_half
            q_rows.append(qf[qs] * x[qs])
            k_rows.extend([kf[lo_half], kf[hi_half] * x[hi_half]] if rev
                          else [kf[lo_half] * x[lo_half], kf[hi_half]])
        p = jnp.dot(jnp.concatenate(q_rows, axis=0).astype(BF16),
                    jnp.concatenate(k_rows, axis=0).T.astype(BF16),
                    preferred_element_type=F32)
        for b in range(c // (2 * w)):
            lo_half = slice(b * 2 * w, b * 2 * w + w)
            hi_half = slice(b * 2 * w + w, (b + 1) * 2 * w)
            qs, ks = (lo_half, hi_half) if rev else (hi_half, lo_half)
            a_ref[qs, ks] = p[b * w:(b + 1) * w, ks]
        w //= 2
        yield

    a = jnp.where(lev == -1, jnp.dot(q, kf.T.astype(BF16), preferred_element_type=F32), 0.0)
    for li in range(len(SMALL_W)):
        e = jnp.concatenate([ge_ref[li * GROUP:(li + 1) * GROUP, g * LANES:(g + 1) * LANES]
                             for g in range(c // GROUP)], axis=0)
        x = jnp.exp2(e)
        p = jnp.dot((qf * x).astype(BF16), (kf * x).T.astype(BF16),
                    preferred_element_type=F32)
        a = jnp.where(lev == li, p, a)
        yield
    for b in range(c // GROUP):
        blk = slice(b * GROUP, (b + 1) * GROUP)
        a_ref[blk, blk] = a[blk, blk]
    ab_ref[...] = a_ref[...].astype(BF16)
    yield


def _interleave(main, fill):
    for _ in main:
        next(fill, None)
    for _ in fill:
        pass


def _hgrn_kernel(q_ref, i_ref, sg_ref, kf_ref, kb_ref, ghf_ref, gmf_ref, ghb_ref, gmb_ref,
                 ng_ref, lev_ref, tc_ref, ts_ref, o_ref,
                 acc_ref, st_ref, a_ref, ab_ref, oi_ref, gc_ref, ge_ref, *, c, nc):
    for ref in (acc_ref, st_ref, a_ref, ab_ref, oi_ref):
        ref[...] = jnp.zeros_like(ref)
    k_refs = (kf_ref, kb_ref)
    g_refs = ((ghf_ref, gmf_ref), (ghb_ref, gmb_ref))

    def rows_of(p, d):
        j = p if d == 0 else nc - 1 - p
        return pl.ds(pl.multiple_of(j * c, c), c)

    def stage_a(p, slot, d):
        rows = rows_of(p, d)
        return _hgrn_stage_a(tc_ref[d], ts_ref[d], g_refs[d][0][0, 0, rows, :],
                             g_refs[d][1][0, 0, rows, :], gc_ref.at[slot, d], ge_ref.at[slot, d], c)

    def stage_t(p, d):
        rows = rows_of(p, d)
        acc_ref[rows, :] += oi_ref[d] + jnp.dot(ab_ref[d], i_ref[0, 0, rows, :],
                                                preferred_element_type=F32)

    def stage_b(p, slot, d):
        rows = rows_of(p, d)
        return _hgrn_stage_b(q_ref[0, 0, rows, :], k_refs[d][0, 0, rows, :], i_ref[0, 0, rows, :],
                             gc_ref.at[slot, d], ge_ref.at[slot, d], lev_ref[d], st_ref.at[d],
                             a_ref.at[d], ab_ref.at[d], oi_ref.at[d], d == 1, c)

    def pair(p, slot):
        p_next, p_prev = jnp.minimum(p + 1, nc - 1), jnp.maximum(p - 1, 0)
        for d in range(2):
            stage_t(p_prev, d)
            _interleave(stage_b(p, slot, d), stage_a(p_next, 1 - slot, d))

    def pairs(i, carry):
        for u in range(PAIRS_PER_BODY):
            pair(PAIRS_PER_BODY * i + u, u % 2)
        return carry

    for d in range(2):
        for _ in stage_a(0, 0, d):
            pass
    lax.fori_loop(0, nc // PAIRS_PER_BODY, pairs, 0)
    for d in range(2):
        stage_t(nc - 1, d)

    ng = ng_ref[...]
    rows_fin = FINISH_CHUNKS * c

    def finish(j, carry):
        rows = pl.ds(pl.multiple_of(j * rows_fin, rows_fin), rows_fin)
        o = acc_ref[rows, :]
        ms = jnp.mean(o * o, axis=-1, keepdims=True)
        o = o * lax.rsqrt(ms + EPS) * ng
        o_ref[0, rows, :] = (o * sg_ref[0, 0, rows, :].astype(F32)).astype(o_ref.dtype)
        return carry

    lax.fori_loop(0, nc // FINISH_CHUNKS, finish, 0)


def _hgrn(hg, norm_g):
    b, _, s, _ = hg.shape
    c = HG_CHUNK
    nc = s // c
    assert PAIRS_PER_BODY % 2 == 0 and nc % PAIRS_PER_BODY == 0 and c % (2 * GROUP) == 0
    lev, tcum, tsmall = _hgrn_consts(c)
    n_small = len(SMALL_W) * GROUP

    def kind(k):
        return pl.BlockSpec((1, 1, s, HG_DIM), lambda i, h: (i, k * HG_HEADS + h, 0, 0))

    const3 = lambda a: pl.BlockSpec(a.shape, lambda i, h: (0, 0, 0))
    return pl.pallas_call(
        functools.partial(_hgrn_kernel, c=c, nc=nc),
        grid=(b, HG_HEADS),
        in_specs=[kind(HK_Q), kind(HK_I), kind(HK_G), kind(HK_KF), kind(HK_KB),
                  kind(HK_GHF), kind(HK_GMF), kind(HK_GHB), kind(HK_GMB),
                  pl.BlockSpec((1, LANES), lambda i, h: (0, h)),
                  const3(lev), const3(tcum), const3(tsmall)],
        out_specs=pl.BlockSpec((1, s, LANES), lambda i, h: (i, 0, h)),
        out_shape=jax.ShapeDtypeStruct((b, s, HG_WIDTH), BF16),
        scratch_shapes=[pltpu.VMEM((s, HG_DIM), F32),
                        pltpu.VMEM((2, HG_DIM, HG_DIM), F32),
                        pltpu.VMEM((2, c, c), F32),
                        pltpu.VMEM((2, c, c), BF16),
                        pltpu.VMEM((2, c, HG_DIM), F32),
                        pltpu.VMEM((2, 2, c, LANES), F32),
                        pltpu.VMEM((2, 2, n_small, (c // GROUP) * LANES), F32)],
        compiler_params=_cparams(("arbitrary", "arbitrary")),
        name="hgrn",
    )(hg, hg, hg, hg, hg, hg, hg, hg, hg, norm_g, lev, tcum, tsmall)


def _attn_kernel(sink_ref, q_ref, kp_ref, ko_ref, kn_ref, vp_ref, vo_ref, vn_ref,
                 bias_ref, o_ref, lg_ref, *, nb):
    n = pl.program_id(1)
    half = LANES // 2
    nk = 3 * BLOCK
    kt = jnp.concatenate([kp_ref[0], ko_ref[0], kn_ref[0]], axis=1)
    vb = jnp.concatenate([vp_ref[0], vo_ref[0], vn_ref[0]], axis=0)
    lo = lax.broadcasted_iota(jnp.int32, (BLOCK, LANES), 1) < half
    lo_v = lax.broadcasted_iota(jnp.int32, vb.shape, 1) < half
    zero_v = jnp.zeros_like(vb)
    zero_k = jnp.zeros((half, nk), kt.dtype)

    k_both = tuple(
        jnp.concatenate([jnp.concatenate([kj, zero_k], axis=0),
                         jnp.concatenate([zero_k, kj], axis=0)], axis=1)
        for kj in (kt[:half], kt[half:]))
    v0lo, v1hi = jnp.where(lo_v, vb, zero_v), jnp.where(lo_v, zero_v, vb)
    v_both = (jnp.concatenate([v0lo, pltpu.roll(v0lo, half, axis=1)], axis=0),
              jnp.concatenate([pltpu.roll(v1hi, half, axis=1), v1hi], axis=0))
    edge = jnp.where(n == 0, 0, jnp.where(n == nb - 1, 2, 1))

    for m in range(ATT_Q_HEADS // 2):
        lg_ref[m] = jnp.dot(q_ref[0, :, m * LANES:(m + 1) * LANES], k_both[(2 * m) // ATT_GROUP],
                            preferred_element_type=F32)

    for m in range(ATT_Q_HEADS // 2):
        ps, rdens = [], []
        for hh in range(2):
            lg = (lg_ref[m, :, hh * nk:(hh + 1) * nk]
                  + bias_ref[edge, m, hh * BLOCK:(hh + 1) * BLOCK, :])
            sink = sink_ref[2 * m + hh]
            mx = jnp.maximum(jnp.max(lg, axis=-1, keepdims=True), sink)
            p = jnp.exp2(lg - mx)
            rdens.append(1.0 / (jnp.sum(p, axis=-1, keepdims=True) + jnp.exp2(sink - mx)))
            ps.append(p.astype(BF16))
        o = jnp.dot(jnp.concatenate(ps, axis=1), v_both[(2 * m) // ATT_GROUP],
                    preferred_element_type=F32)
        o_ref[0, :, m * LANES:(m + 1) * LANES] = (
            o * jnp.where(lo, rdens[0], rdens[1])).astype(o_ref.dtype)


def _attn(aq, ak, av, sink2, bias):
    b, s, _ = aq.shape
    nb = s // BLOCK
    assert nb >= 2

    def neighbour(n, shift):
        return jnp.clip(n + shift, 0, nb - 1)

    kt = lambda shift: pl.BlockSpec((1, KV_WIDTH, BLOCK), lambda i, n: (i, 0, neighbour(n, shift)))
    v = lambda shift: pl.BlockSpec((1, BLOCK, KV_WIDTH), lambda i, n: (i, neighbour(n, shift), 0))
    return pl.pallas_call(
        functools.partial(_attn_kernel, nb=nb),
        grid=(b, nb),
        in_specs=[pl.BlockSpec(memory_space=pltpu.SMEM),
                  pl.BlockSpec((1, BLOCK, ATT_WIDTH), lambda i, n: (i, n, 0)),
                  kt(-1), kt(0), kt(1), v(-1), v(0), v(1),
                  pl.BlockSpec(bias.shape, lambda i, n: (0, 0, 0, 0))],
        out_specs=pl.BlockSpec((1, BLOCK, ATT_WIDTH), lambda i, n: (i, n, 0)),
        out_shape=jax.ShapeDtypeStruct((b, s, ATT_WIDTH), BF16),
        scratch_shapes=[pltpu.VMEM((ATT_Q_HEADS // 2, BLOCK, 2 * 3 * BLOCK), F32)],
        compiler_params=_cparams(("arbitrary", "arbitrary")),
        name="attn",
    )(sink2, aq, ak, ak, ak, av, av, av, bias)


def _mixin_col_map():
    hw = HG_WIDTH
    att = ATT_WIDTH + 2 * KV_WIDTH
    return ((Z_FF, hw, 2 * hw), (Z_Q, 0, hw), (Z_G, 4 * hw, hw), (Z_AQ, 5 * hw, att),
            (Z_I, 3 * hw, hw))


def kernel(x, c, w_ada, b_ada, norm_g, w_ffn1_in, w_ffn1_out, w_ffn2_in, w_ffn2_out,
           w_mix_in, w_mix_out, hgrn_lb, hgrn_norm_g, qk_norm_g, attn_sink, rel_bias):
    b, s, d = x.shape
    depth = w_ada.shape[0]
    bias = _bias_tile(rel_bias)
    for l in range(depth):
        mods = _ada(c.astype(F32), w_ada[l], b_ada[l][None, :])
        sh1, sc1, g1, sh2, sc2, g2, sh3, sc3, g3 = [
            mods[:, i * d:(i + 1) * d][:, None, :] for i in range(N_MOD)]
        x, w2_in, w2_out, w_mi, w_mo = _ffn_first(
            x, sh1, sc1, g1, norm_g[l, 0][None, :], w_ffn1_in, w_ffn1_out, l,
            casts=((w_ffn2_in, None, 1), (w_ffn2_out, None, 2),
                   (w_mix_in, _mixin_col_map(), 1), (w_mix_out, None, 1)))
        qg = (jnp.tile(qk_norm_g[l, 0], 2) * (LOG2E / math.sqrt(ATT_HEAD_DIM)))[None, :]
        kg = jnp.tile(qk_norm_g[l, 1], 2)[None, :]
        hg, aq, ak, av = _mixin(x, sh2, sc2, norm_g[l, 1][None, :], w_mi, hgrn_lb, qg, kg, l)
        o_hg = _hgrn(hg, hgrn_norm_g[l][None, :])
        att = _attn(aq, ak, av, attn_sink[l] * LOG2E, bias)
        x = _ffn_mix(o_hg, att, g2, w_mo, x, sh3, sc3, g3, norm_g[l, 2][None, :], w2_in, w2_out)
    return x
```

```python
import functools
import math

import numpy as np
import jax
import jax.numpy as jnp
from jax import lax
from jax.experimental import pallas as pl
from jax.experimental.pallas import tpu as pltpu

F32 = jnp.float32
BF16 = jnp.bfloat16

HG_HEADS = 4
HG_DIM = 128
HG_WIDTH = HG_HEADS * HG_DIM
ATT_Q_HEADS = 8
ATT_KV_HEADS = 2
ATT_HEAD_DIM = 64
ATT_GROUP = ATT_Q_HEADS // ATT_KV_HEADS
ATT_WIDTH = ATT_Q_HEADS * ATT_HEAD_DIM
KV_WIDTH = ATT_KV_HEADS * ATT_HEAD_DIM
WINDOW = 128
BLOCK = 128
NUM_BUCKETS = 32
MAX_DISTANCE = 128
D_FF = 2816
N_MOD = 9
EPS = 1e-6

LANES = 128
PACKED_ROWS = 16
MXU_COLS = 256
VMEM_LIMIT = 56 * 1024 * 1024

FFN_TM = 512
FFN_LOAD_STEPS = 16
MIX_TM = 256
HG_CHUNK = 128
NEG_INF = float("-inf")
LOG2E = 1.0 / math.log(2.0)

Z_FF, Z_FB, Z_Q, Z_G, Z_AQ = (n * HG_WIDTH for n in range(5))
Z_AK = Z_AQ + ATT_WIDTH
Z_AV = Z_AK + KV_WIDTH
Z_I = Z_AV + KV_WIDTH
Z_END = Z_I + HG_WIDTH
(HK_Q, HK_I, HK_G, HK_KF, HK_KB, HK_GHF, HK_GMF, HK_GHB, HK_GMB) = range(9)
N_HK = 9


def _cparams(sem):
    return pltpu.CompilerParams(dimension_semantics=sem, vmem_limit_bytes=VMEM_LIMIT)


def _sigmoid(x):
    return 1.0 / (1.0 + jnp.exp2(x * (-LOG2E)))


def _t5_bucket_np(rel):
    nb = NUM_BUCKETS // 2
    max_exact = nb // 2
    ret = (rel > 0).astype(np.int32) * nb
    n = np.abs(rel)
    ratio = np.maximum(n, 1).astype(np.float32) / np.float32(max_exact)
    large = max_exact + (np.log(ratio) / np.float32(math.log(MAX_DISTANCE / max_exact))
                         * np.float32(nb - max_exact)).astype(np.int32)
    large = np.minimum(large, nb - 1)
    return ret + np.where(n < max_exact, n, large)


def _ada_bias_kernel(rb_ref, ct_ref, w_ref, b_ref, bucket_ref, mask_ref, o_ref, bias_ref):
    ct = ct_ref[...]
    ca = ct * _sigmoid(ct)
    w = w_ref[...]
    for b in range(ct.shape[1]):
        o_ref[b:b + 1, :] = jnp.sum(w * ca[:, b:b + 1], axis=0, keepdims=True) + b_ref[...]

    pair = pl.program_id(0)
    bucket = bucket_ref[...]
    for hh in range(2):
        head = 2 * pair + hh
        tile = jnp.zeros(bucket.shape, F32)
        for b in range(NUM_BUCKETS):
            tile = jnp.where(bucket == b, rb_ref[b, head], tile)
        tile = tile * LOG2E
        for e in range(3):
            bias_ref[e, 0, hh * BLOCK:(hh + 1) * BLOCK, :] = tile + mask_ref[e]


def _ada_bias(c, w, b, rel_bias):
    batch, d = c.shape
    n = w.shape[1]
    n_pairs = ATT_Q_HEADS // 2
    assert n % (n_pairs * LANES) == 0
    tn = n // n_pairs
    kcol = np.arange(3 * BLOCK)[None, :]
    rel = (kcol - BLOCK) - np.arange(BLOCK)[:, None]
    bucket = _t5_bucket_np(rel).astype(np.int32)
    window = np.abs(rel) <= WINDOW
    valid = np.stack([window & (kcol >= BLOCK), window, window & (kcol < 2 * BLOCK)])
    mask = np.where(valid, 0.0, NEG_INF).astype(np.float32)
    return pl.pallas_call(
        _ada_bias_kernel,
        grid=(n_pairs,),
        in_specs=[pl.BlockSpec(memory_space=pltpu.SMEM),
                  pl.BlockSpec((d, batch), lambda j: (0, 0)),
                  pl.BlockSpec((d, tn), lambda j: (0, j)),
                  pl.BlockSpec((1, tn), lambda j: (0, j)),
                  pl.BlockSpec(bucket.shape, lambda j: (0, 0)),
                  pl.BlockSpec(mask.shape, lambda j: (0, 0, 0))],
        out_specs=[pl.BlockSpec((batch, tn), lambda j: (0, j)),
                   pl.BlockSpec((3, 1, 2 * BLOCK, 3 * BLOCK), lambda j: (0, j, 0, 0))],
        out_shape=[jax.ShapeDtypeStruct((batch, n), F32),
                   jax.ShapeDtypeStruct((3, n_pairs, 2 * BLOCK, 3 * BLOCK), F32)],
        compiler_params=_cparams(("arbitrary",)),
        name="ada_bias",
    )(rel_bias.astype(F32), c.T, w, b, jnp.asarray(bucket), jnp.asarray(mask))


def _scaled_lhs(x, ng, sc):
    r = lax.rsqrt(jnp.mean(x * x, axis=-1, keepdims=True) + EPS)
    return (x * (ng * (1.0 + sc))).astype(BF16), r


def _shift_slabs(shift, n):
    batch, _, d = shift.shape
    return shift[:, 0, :].reshape(batch, n, d // n).transpose(1, 2, 0)


def _shift_product(slab, shift_cols):
    return jnp.concatenate(
        [jnp.sum(slab * shift_cols[:, b:b + 1], axis=0, keepdims=True)
         for b in range(shift_cols.shape[1])], axis=0)


def _ffn_tile(x, shw, sc_ref, gt_ref, ng_ref, win_ref, wout_ref, o_ref):
    lhs, r = _scaled_lhs(x, ng_ref[...], sc_ref[0])
    gu = jnp.dot(lhs, win_ref[...], preferred_element_type=F32) * r + shw
    g = gu[:, :D_FF]
    u = gu[:, D_FF:]
    act = (g * _sigmoid(g) * u).astype(BF16)
    y = jnp.dot(act, wout_ref[...], preferred_element_type=F32)
    o_ref[0] = x + 0.5 * gt_ref[0] * y


def _ffn_first_kernel(x_ref, sc_ref, gt_ref, ng_ref, win_ref, wout_ref, shift_ref, *refs,
                      n_load, per_b, casts):
    n_shift = sum(1 for _, has_shift in casts if has_shift)
    src_refs, refs = refs[:len(casts)], refs[len(casts):]
    cast_shift_refs, refs = refs[:n_shift], refs[n_shift:]
    o_ref, *refs = refs
    dst_refs, refs = refs[:len(casts)], refs[len(casts):]
    shw_refs, (win_s, wout_s, shw_s) = refs[:n_shift], refs[n_shift:]
    p = pl.program_id(0)

    @pl.when(p == 0)
    def _():
        shw_s[...] = jnp.zeros_like(shw_s)

    @pl.when(p < n_load)
    def _():
        for slab_ref, full in ((win_ref, win_s), (wout_ref, wout_s)):
            rows = slab_ref.shape[1]
            full[pl.ds(pl.multiple_of(p * rows, rows), rows), :] = slab_ref[0].astype(BF16)
        shw_s[...] += _shift_product(win_ref[0], shift_ref[0])

    @pl.when(p == n_load)
    def _():
        for shw_ref in shw_refs:
            shw_ref[...] = jnp.zeros_like(shw_ref)

    @pl.when(p >= n_load)
    def _():
        batch = (p - n_load) // per_b
        _ffn_tile(x_ref[0], shw_s[pl.ds(batch, 1), :], sc_ref, gt_ref, ng_ref, win_s, wout_s, o_ref)
        shifts = iter(zip(cast_shift_refs, shw_refs))
        for src_ref, dst_ref, (col_map, has_shift) in zip(src_refs, dst_refs, casts):
            slab_ref, shw_ref = next(shifts) if has_shift else (None, None)
            for dst0, src0, width in col_map:
                piece = src_ref[0, :, src0:src0 + width]
                dst_ref[:, dst0:dst0 + width] = piece.astype(BF16)
                if has_shift:
                    shw_ref[:, dst0:dst0 + width] += _shift_product(piece, slab_ref[0])


def _ffn_first(x, sh, sc, gt, ng, w_in, w_out, layer, casts):
    b, s, d = x.shape
    tm = FFN_TM
    per_b = s // tm
    n_tiles = b * per_b
    n_load = FFN_LOAD_STEPS

    def tile(p):
        return jnp.maximum(p - n_load, 0)

    def own_slab(mats):
        rows = mats.shape[1]
        assert rows % (n_load * PACKED_ROWS) == 0
        return pl.BlockSpec((1, rows // n_load, mats.shape[2]),
                            lambda p: (layer, jnp.minimum(p, n_load - 1), 0))

    rows_spec = pl.BlockSpec((1, tm, d), lambda p: (tile(p) // per_b, tile(p) % per_b, 0))
    vec = pl.BlockSpec((1, 1, d), lambda p: (tile(p) // per_b, 0, 0))
    args = [x, sc, gt, ng, w_in, w_out, _shift_slabs(sh, n_load)]
    specs = [rows_spec, vec, vec, pl.BlockSpec(ng.shape, lambda p: (0, 0)),
             own_slab(w_in), own_slab(w_out),
             pl.BlockSpec((1, d // n_load, b), lambda p: (jnp.minimum(p, n_load - 1), 0, 0))]
    cast_args, cast_specs, shift_args, shift_specs = [], [], [], []
    out_specs, out_shape = [rows_spec], [jax.ShapeDtypeStruct(x.shape, F32)]
    shw_specs, shw_shape, col_maps = [], [], []
    for mats, col_map, dup, shift in casts:
        _, rows, cols = mats.shape
        assert n_tiles % dup == 0
        n_slabs = n_tiles // dup
        assert rows % (n_slabs * PACKED_ROWS) == 0 and cols % LANES == 0
        slab = rows // n_slabs
        cast_args.append(mats)
        cast_specs.append(pl.BlockSpec((1, slab, cols), lambda p, dup=dup: (layer, tile(p) // dup, 0)))
        out_specs.append(pl.BlockSpec((slab, cols), lambda p, dup=dup: (tile(p) // dup, 0)))
        out_shape.append(jax.ShapeDtypeStruct((rows, cols), BF16))
        col_maps.append((tuple(col_map) if col_map is not None else ((0, 0, cols),),
                         shift is not None))
        if shift is not None:
            assert dup == 1
            shift_args.append(_shift_slabs(shift, n_slabs))
            shift_specs.append(pl.BlockSpec((1, slab, b), lambda p: (tile(p), 0, 0)))
            shw_specs.append(pl.BlockSpec((b, cols), lambda p: (0, 0)))
            shw_shape.append(jax.ShapeDtypeStruct((b, cols), F32))
    return pl.pallas_call(
        functools.partial(_ffn_first_kernel, n_load=n_load, per_b=per_b, casts=tuple(col_maps)),
        grid=(n_load + n_tiles,),
        in_specs=specs + cast_specs + shift_specs,
        out_specs=out_specs + shw_specs,
        out_shape=out_shape + shw_shape,
        scratch_shapes=[pltpu.VMEM(w_in.shape[1:], BF16), pltpu.VMEM(w_out.shape[1:], BF16),
                        pltpu.VMEM((b, w_in.shape[2]), F32)],
        compiler_params=_cparams(("arbitrary",)),
        name="ffn",
    )(*args, *cast_args, *shift_args)


def _ffn_mix_kernel(hg_ref, at_ref, gm_ref, wm_ref, x_ref, shw_ref, sc_ref, gt_ref, ng_ref,
                    win_ref, wout_ref, o_ref):
    mixed = (jnp.dot(hg_ref[0], wm_ref[:HG_WIDTH, :], preferred_element_type=F32)
             + jnp.dot(at_ref[0], wm_ref[HG_WIDTH:, :], preferred_element_type=F32))
    x = x_ref[0] + gm_ref[0] * mixed
    _ffn_tile(x, shw_ref[0], sc_ref, gt_ref, ng_ref, win_ref, wout_ref, o_ref)


def _ffn_mix(o_hg, att, gm, wm, x, shw, sc, gt, ng, w_in, w_out):
    b, s, d = x.shape
    tm = FFN_TM
    vec = lambda n: pl.BlockSpec((1, 1, n), lambda i, j: (i, 0, 0))
    rows = lambda n: pl.BlockSpec((1, tm, n), lambda i, j: (i, j, 0))
    whole = lambda a: pl.BlockSpec(a.shape, lambda i, j: (0, 0), pipeline_mode=pl.Buffered(1))
    return pl.pallas_call(
        _ffn_mix_kernel,
        grid=(b, s // tm),
        in_specs=[rows(HG_WIDTH), rows(ATT_WIDTH), vec(d), whole(wm), rows(d), vec(shw.shape[2]),
                  vec(d), vec(d), whole(ng), whole(w_in), whole(w_out)],
        out_specs=rows(d),
        out_shape=jax.ShapeDtypeStruct(x.shape, F32),
        compiler_params=_cparams(("arbitrary", "arbitrary")),
        name="ffn_mix",
    )(o_hg, att, gm, wm, x, shw, sc, gt, ng, w_in, w_out)


def _half_norm(x, gain, lo):
    x2 = x * x
    s_lo = jnp.sum(jnp.where(lo, x2, 0.0), axis=-1, keepdims=True)
    s_hi = jnp.sum(jnp.where(lo, 0.0, x2), axis=-1, keepdims=True)
    ms = jnp.where(lo, s_lo, s_hi) * (1.0 / ATT_HEAD_DIM)
    return x * lax.rsqrt(ms + EPS) * gain


def _mixin_kernel(x_ref, shw_ref, sc_ref, ng_ref, w_ref, lb_ref, qg_ref, kg_ref,
                  hg_ref, aq_ref, ak_ref, av_ref, *, layer):
    lhs, r = _scaled_lhs(x_ref[0], ng_ref[...], sc_ref[0])

    def proj(c0, c1):
        return (jnp.dot(lhs, w_ref[:, c0:c1], preferred_element_type=F32) * r
                + shw_ref[0, :, c0:c1])

    part_w = HG_WIDTH // 2

    def put_heads(kind, val, part):
        for hh in range(part_w // HG_DIM):
            head = part * (part_w // HG_DIM) + hh
            hg_ref[0, kind * HG_HEADS + head] = val[:, hh * HG_DIM:(hh + 1) * HG_DIM].astype(BF16)

    raw = lb_ref[...]
    mx = jnp.max(raw, axis=1, keepdims=True)
    ex = jnp.exp(raw - mx)
    lb = jnp.sum(ex[:, :layer + 1, :], axis=1) / jnp.sum(ex, axis=1)

    def forget(d, z0, kk, kh, km, part):
        cols = slice(part * part_w, (part + 1) * part_w)
        fr = proj(z0 + cols.start, z0 + cols.stop)
        lbd = lb[d:d + 1, cols]
        f = lbd + (1.0 - lbd) * _sigmoid(fr)
        g2 = jnp.log(f) * LOG2E
        hi = g2.astype(BF16)
        put_heads(kk, 1.0 - f, part)
        put_heads(kh, hi, part)
        put_heads(km, g2 - hi.astype(F32), part)

    def plain(kind, z0, part, silu):
        val = proj(z0 + part * part_w, z0 + (part + 1) * part_w)
        put_heads(kind, val * _sigmoid(val) if silu else val, part)

    lo = lax.broadcasted_iota(jnp.int32, (lhs.shape[0], LANES), 1) < LANES // 2

    def queries(part):
        aq = proj(Z_AQ + part * part_w, Z_AQ + (part + 1) * part_w)
        for m in range(part_w // LANES):
            cols = slice(m * LANES, (m + 1) * LANES)
            aq_ref[0, :, part * part_w + m * LANES:part * part_w + (m + 1) * LANES] = (
                _half_norm(aq[:, cols], qg_ref[...], lo).astype(BF16))

    def keys_values():
        akv = proj(Z_AK, Z_I)
        ak_ref[0] = _half_norm(akv[:, :KV_WIDTH], kg_ref[...], lo).T.astype(BF16)
        av_ref[0] = akv[:, KV_WIDTH:].astype(BF16)

    forget(0, Z_FF, HK_KF, HK_GHF, HK_GMF, 0)
    plain(HK_I, Z_I, 0, False)
    forget(0, Z_FF, HK_KF, HK_GHF, HK_GMF, 1)
    plain(HK_I, Z_I, 1, False)
    forget(1, Z_FB, HK_KB, HK_GHB, HK_GMB, 0)
    keys_values()
    forget(1, Z_FB, HK_KB, HK_GHB, HK_GMB, 1)
    plain(HK_Q, Z_Q, 0, True)
    queries(0)
    plain(HK_Q, Z_Q, 1, True)
    queries(1)
    plain(HK_G, Z_G, 0, True)
    plain(HK_G, Z_G, 1, True)


def _mixin(x, shw, sc, ng, w, hgrn_lb, qg, kg, layer):
    b, s, d = x.shape
    tm = MIX_TM
    vec = pl.BlockSpec((1, 1, d), lambda i, j: (i, 0, 0))
    row = lambda n: pl.BlockSpec((1, n), lambda i, j: (0, 0))
    rows3 = lambda n: pl.BlockSpec((1, tm, n), lambda i, j: (i, j, 0))
    n_hg = N_HK * HG_HEADS
    return pl.pallas_call(
        functools.partial(_mixin_kernel, layer=layer),
        grid=(b, s // tm),
        in_specs=[pl.BlockSpec((1, tm, d), lambda i, j: (i, j, 0)),
                  pl.BlockSpec((1, 1, w.shape[1]), lambda i, j: (i, 0, 0)), vec, row(d),
                  pl.BlockSpec(w.shape, lambda i, j: (0, 0)),
                  pl.BlockSpec(hgrn_lb.shape, lambda i, j: (0, 0, 0)),
                  row(LANES), row(LANES)],
        out_specs=[pl.BlockSpec((1, n_hg, tm, HG_DIM), lambda i, j: (i, 0, j, 0)),
                   rows3(ATT_WIDTH),
                   pl.BlockSpec((1, KV_WIDTH, tm), lambda i, j: (i, 0, j)),
                   rows3(KV_WIDTH)],
        out_shape=[jax.ShapeDtypeStruct((b, n_hg, s, HG_DIM), BF16),
                   jax.ShapeDtypeStruct((b, s, ATT_WIDTH), BF16),
                   jax.ShapeDtypeStruct((b, KV_WIDTH, s), BF16),
                   jax.ShapeDtypeStruct((b, s, KV_WIDTH), BF16)],
        compiler_params=_cparams(("arbitrary", "arbitrary")),
        name="mix_in",
    )(x, shw, sc, ng, w, hgrn_lb, qg, kg)


SMALL_W = (1, 2, 4, 8)
GROUP = PACKED_ROWS
FINISH_CHUNKS = 8
PAIRS_PER_BODY = 16


def _hgrn_consts(c):
    t = np.arange(c)[:, None]
    s = np.arange(c)[None, :]
    x = t ^ s
    lev = np.where(x > 0, np.floor(np.log2(np.maximum(x, 1))), -1).astype(np.int32)
    lev_f = np.where(t > s, lev, np.where(t == s, -1, -2)).astype(np.int32)
    lev_b = lev_f.T.copy()

    def exponent_rows(w, rev):
        m = np.zeros((GROUP, GROUP), np.float32)
        for r in range(GROUP):
            b0 = (r // (2 * w)) * 2 * w
            if not rev:
                ref = b0 + w - 1
                lo_u, hi_u = (ref + 1, r) if r > ref else (r + 1, ref)
            else:
                ref = b0 + w
                lo_u, hi_u = (r, ref - 1) if r < ref else (ref, r - 1)
            m[r, lo_u:hi_u + 1] = 1.0
        return m

    cums, smalls = [], []
    for rev in (False, True):
        cum = ((s >= t) if rev else (s <= t)).astype(np.float32)
        cums.append(np.concatenate([cum, cum], axis=1))
        rows = np.concatenate([exponent_rows(w, rev) for w in SMALL_W])
        smalls.append(np.concatenate([rows, rows], axis=1))
    return (jnp.asarray(np.stack([lev_f, lev_b])),
            jnp.asarray(np.stack(cums), dtype=BF16),
            jnp.asarray(np.stack(smalls), dtype=BF16))


def _neg_abs(x):
    bits = lax.bitcast_convert_type(x, jnp.uint32) | jnp.uint32(0x80000000)
    return lax.bitcast_convert_type(bits, F32)


def _ref_rows(g_cum, w, rev, c):
    idx = w if rev else w - 1
    g3 = g_cum.reshape(c // (2 * w), 2 * w, LANES)
    return jnp.broadcast_to(g3[:, idx:idx + 1, :], g3.shape).reshape(c, LANES)


def _hgrn_stage_a(tcum, tsmall, gh, gm, gc_ref, ge_ref, c):
    hm = jnp.concatenate([gh, gm], axis=0)
    half = c // 2
    for r in range(2):
        rows = slice(r * half, (r + 1) * half)
        gc_ref[rows, :] = jnp.dot(tcum[rows], hm, preferred_element_type=F32)
        yield
    per_tile = MXU_COLS // LANES
    for t in range(c // GROUP // per_tile):
        groups = [slice(g * GROUP, (g + 1) * GROUP) for g in range(t * per_tile, (t + 1) * per_tile)]
        wide = jnp.concatenate([jnp.concatenate([gh[g] for g in groups], axis=1),
                                jnp.concatenate([gm[g] for g in groups], axis=1)], axis=0)
        ge_ref[:, t * MXU_COLS:(t + 1) * MXU_COLS] = jnp.dot(tsmall, wide,
                                                              preferred_element_type=F32)
        yield


def _hgrn_stage_b(q, k, v, gc_ref, ge_ref, lev, st_ref, a_ref, ab_ref, oi_ref, rev, c):
    tn = (((0,), (0,)), ((), ()))
    g_cum = gc_ref[...]
    g_last = g_cum[0:1, :] if rev else g_cum[c - 1:c, :]
    qf, kf = q.astype(F32), k.astype(F32)

    st = st_ref[...]
    qi = (qf * jnp.exp2(g_cum)).astype(BF16)
    oi_ref[...] = jnp.dot(qi, st.T.astype(BF16), preferred_element_type=F32)
    kd = (kf * jnp.exp2(g_last - g_cum)).astype(BF16)
    u_t = lax.dot_general(v, kd, tn, preferred_element_type=F32)
    st_ref[...] = jnp.exp2(g_last) * st + u_t
    yield

    w = c // 2
    while w >= GROUP:
        x = jnp.exp2(_neg_abs(g_cum - _ref_rows(g_cum, w, rev, c)))
        q_rows, k_rows = [], []
        for b in range(c // (2 * w)):
            lo_half = slice(b * 2 * w, b * 2 * w + w)
            hi_half = slice(b * 2 * w + w, (b + 1) * 2 * w)
            qs = lo_half if rev else hi_half
            q_rows.append(qf[qs] * x[qs])
            k_rows.extend([kf[lo_half], kf[hi_half] * x[hi_half]] if rev
                          else [kf[lo_half] * x[lo_half], kf[hi_half]])
        p = jnp.dot(jnp.concatenate(q_rows, axis=0).astype(BF16),
                    jnp.concatenate(k_rows, axis=0).T.astype(BF16),
                    preferred_element_type=F32)
        for b in range(c // (2 * w)):
            lo_half = slice(b * 2 * w, b * 2 * w + w)
            hi_half = slice(b * 2 * w + w, (b + 1) * 2 * w)
            qs, ks = (lo_half, hi_half) if rev else (hi_half, lo_half)
            a_ref[qs, ks] = p[b * w:(b + 1) * w, ks]
        w //= 2
        yield

    a = jnp.where(lev == -1, jnp.dot(q, kf.T.astype(BF16), preferred_element_type=F32), 0.0)
    for li in range(len(SMALL_W)):
        e = jnp.concatenate([ge_ref[li * GROUP:(li + 1) * GROUP, g * LANES:(g + 1) * LANES]
                             for g in range(c // GROUP)], axis=0)
        x = jnp.exp2(e)
        p = jnp.dot((qf * x).astype(BF16), (kf * x).T.astype(BF16),
                    preferred_element_type=F32)
        a = jnp.where(lev == li, p, a)
        yield
    for b in range(c // GROUP):
        blk = slice(b * GROUP, (b + 1) * GROUP)
        a_ref[blk, blk] = a[blk, blk]
    ab_ref[...] = a_ref[...].astype(BF16)
    yield


def _interleave(main, fill):
    for _ in main:
        next(fill, None)
    for _ in fill:
        pass


def _hgrn_kernel(q_ref, i_ref, sg_ref, kf_ref, kb_ref, ghf_ref, gmf_ref, ghb_ref, gmb_ref,
                 ng_ref, lev_ref, tc_ref, ts_ref, o_ref,
                 acc_ref, st_ref, a_ref, ab_ref, oi_ref, gc_ref, ge_ref, *, c, nc):
    for ref in (acc_ref, st_ref, a_ref, ab_ref, oi_ref):
        ref[...] = jnp.zeros_like(ref)
    k_refs = (kf_ref, kb_ref)
    g_refs = ((ghf_ref, gmf_ref), (ghb_ref, gmb_ref))

    def rows_of(p, d):
        j = p if d == 0 else nc - 1 - p
        return pl.ds(pl.multiple_of(j * c, c), c)

    def stage_a(p, slot, d):
        rows = rows_of(p, d)
        return _hgrn_stage_a(tc_ref[d], ts_ref[d], g_refs[d][0][0, 0, rows, :],
                             g_refs[d][1][0, 0, rows, :], gc_ref.at[slot, d], ge_ref.at[slot, d], c)

    def stage_t(p, d):
        rows = rows_of(p, d)
        acc_ref[rows, :] += oi_ref[d] + jnp.dot(ab_ref[d], i_ref[0, 0, rows, :],
                                                preferred_element_type=F32)

    def stage_b(p, slot, d):
        rows = rows_of(p, d)
        return _hgrn_stage_b(q_ref[0, 0, rows, :], k_refs[d][0, 0, rows, :], i_ref[0, 0, rows, :],
                             gc_ref.at[slot, d], ge_ref.at[slot, d], lev_ref[d], st_ref.at[d],
                             a_ref.at[d], ab_ref.at[d], oi_ref.at[d], d == 1, c)

    def pair(p, slot):
        p_next, p_prev = jnp.minimum(p + 1, nc - 1), jnp.maximum(p - 1, 0)
        for d in range(2):
            stage_t(p_prev, d)
            _interleave(stage_b(p, slot, d), stage_a(p_next, 1 - slot, d))

    def pairs(i, carry):
        for u in range(PAIRS_PER_BODY):
            pair(PAIRS_PER_BODY * i + u, u % 2)
        return carry

    for d in range(2):
        for _ in stage_a(0, 0, d):
            pass
    lax.fori_loop(0, nc // PAIRS_PER_BODY, pairs, 0)
    for d in range(2):
        stage_t(nc - 1, d)

    ng = ng_ref[...]
    rows_fin = FINISH_CHUNKS * c

    def finish(j, carry):
        rows = pl.ds(pl.multiple_of(j * rows_fin, rows_fin), rows_fin)
        o = acc_ref[rows, :]
        ms = jnp.mean(o * o, axis=-1, keepdims=True)
        o = o * lax.rsqrt(ms + EPS) * ng
        o_ref[0, rows, :] = (o * sg_ref[0, 0, rows, :].astype(F32)).astype(o_ref.dtype)
        return carry

    lax.fori_loop(0, nc // FINISH_CHUNKS, finish, 0)


def _hgrn(hg, norm_g):
    b, _, s, _ = hg.shape
    c = HG_CHUNK
    nc = s // c
    assert PAIRS_PER_BODY % 2 == 0 and nc % PAIRS_PER_BODY == 0 and c % (2 * GROUP) == 0
    lev, tcum, tsmall = _hgrn_consts(c)
    n_small = len(SMALL_W) * GROUP

    def kind(k):
        return pl.BlockSpec((1, 1, s, HG_DIM), lambda i, h: (i, k * HG_HEADS + h, 0, 0))

    const3 = lambda a: pl.BlockSpec(a.shape, lambda i, h: (0, 0, 0))
    return pl.pallas_call(
        functools.partial(_hgrn_kernel, c=c, nc=nc),
        grid=(b, HG_HEADS),
        in_specs=[kind(HK_Q), kind(HK_I), kind(HK_G), kind(HK_KF), kind(HK_KB),
                  kind(HK_GHF), kind(HK_GMF), kind(HK_GHB), kind(HK_GMB),
                  pl.BlockSpec((1, LANES), lambda i, h: (0, h)),
                  const3(lev), const3(tcum), const3(tsmall)],
        out_specs=pl.BlockSpec((1, s, LANES), lambda i, h: (i, 0, h)),
        out_shape=jax.ShapeDtypeStruct((b, s, HG_WIDTH), BF16),
        scratch_shapes=[pltpu.VMEM((s, HG_DIM), F32),
                        pltpu.VMEM((2, HG_DIM, HG_DIM), F32),
                        pltpu.VMEM((2, c, c), F32),
                        pltpu.VMEM((2, c, c), BF16),
                        pltpu.VMEM((2, c, HG_DIM), F32),
                        pltpu.VMEM((2, 2, c, LANES), F32),
                        pltpu.VMEM((2, 2, n_small, (c // GROUP) * LANES), F32)],
        compiler_params=_cparams(("arbitrary", "arbitrary")),
        name="hgrn",
    )(hg, hg, hg, hg, hg, hg, hg, hg, hg, norm_g, lev, tcum, tsmall)


def _attn_kernel(sink_ref, q_ref, kp_ref, ko_ref, kn_ref, vp_ref, vo_ref, vn_ref,
                 bias_ref, o_ref, lg_ref, *, nb):
    n = pl.program_id(1)
    half = LANES // 2
    nk = 3 * BLOCK
    kt = jnp.concatenate([kp_ref[0], ko_ref[0], kn_ref[0]], axis=1)
    vb = jnp.concatenate([vp_ref[0], vo_ref[0], vn_ref[0]], axis=0)
    lo = lax.broadcasted_iota(jnp.int32, (BLOCK, LANES), 1) < half
    lo_v = lax.broadcasted_iota(jnp.int32, vb.shape, 1) < half
    zero_v = jnp.zeros_like(vb)
    zero_k = jnp.zeros((half, nk), kt.dtype)

    k_both = tuple(
        jnp.concatenate([jnp.concatenate([kj, zero_k], axis=0),
                         jnp.concatenate([zero_k, kj], axis=0)], axis=1)
        for kj in (kt[:half], kt[half:]))
    v0lo, v1hi = jnp.where(lo_v, vb, zero_v), jnp.where(lo_v, zero_v, vb)
    v_both = (jnp.concatenate([v0lo, pltpu.roll(v0lo, half, axis=1)], axis=0),
              jnp.concatenate([pltpu.roll(v1hi, half, axis=1), v1hi], axis=0))
    edge = jnp.where(n == 0, 0, jnp.where(n == nb - 1, 2, 1))

    for m in range(ATT_Q_HEADS // 2):
        lg_ref[m] = jnp.dot(q_ref[0, :, m * LANES:(m + 1) * LANES], k_both[(2 * m) // ATT_GROUP],
                            preferred_element_type=F32)

    for m in range(ATT_Q_HEADS // 2):
        ps, rdens = [], []
        for hh in range(2):
            lg = (lg_ref[m, :, hh * nk:(hh + 1) * nk]
                  + bias_ref[edge, m, hh * BLOCK:(hh + 1) * BLOCK, :])
            sink = sink_ref[2 * m + hh]
            mx = jnp.maximum(jnp.max(lg, axis=-1, keepdims=True), sink)
            p = jnp.exp2(lg - mx)
            rdens.append(1.0 / (jnp.sum(p, axis=-1, keepdims=True) + jnp.exp2(sink - mx)))
            ps.append(p.astype(BF16))
        o = jnp.dot(jnp.concatenate(ps, axis=1), v_both[(2 * m) // ATT_GROUP],
                    preferred_element_type=F32)
        o_ref[0, :, m * LANES:(m + 1) * LANES] = (
            o * jnp.where(lo, rdens[0], rdens[1])).astype(o_ref.dtype)


def _attn(aq, ak, av, sink2, bias):
    b, s, _ = aq.shape
    nb = s // BLOCK
    assert nb >= 2

    def neighbour(n, shift):
        return jnp.clip(n + shift, 0, nb - 1)

    kt = lambda shift: pl.BlockSpec((1, KV_WIDTH, BLOCK), lambda i, n: (i, 0, neighbour(n, shift)))
    v = lambda shift: pl.BlockSpec((1, BLOCK, KV_WIDTH), lambda i, n: (i, neighbour(n, shift), 0))
    return pl.pallas_call(
        functools.partial(_attn_kernel, nb=nb),
        grid=(b, nb),
        in_specs=[pl.BlockSpec(memory_space=pltpu.SMEM),
                  pl.BlockSpec((1, BLOCK, ATT_WIDTH), lambda i, n: (i, n, 0)),
                  kt(-1), kt(0), kt(1), v(-1), v(0), v(1),
                  pl.BlockSpec(bias.shape, lambda i, n: (0, 0, 0, 0))],
        out_specs=pl.BlockSpec((1, BLOCK, ATT_WIDTH), lambda i, n: (i, n, 0)),
        out_shape=jax.ShapeDtypeStruct((b, s, ATT_WIDTH), BF16),
        scratch_shapes=[pltpu.VMEM((ATT_Q_HEADS // 2, BLOCK, 2 * 3 * BLOCK), F32)],
        compiler_params=_cparams(("arbitrary", "arbitrary")),
        name="attn",
    )(sink2, aq, ak, ak, ak, av, av, av, bias)


def _mixin_col_map():
    hw = HG_WIDTH
    att = ATT_WIDTH + 2 * KV_WIDTH
    return ((Z_FF, hw, 2 * hw), (Z_Q, 0, hw), (Z_G, 4 * hw, hw), (Z_AQ, 5 * hw, att),
            (Z_I, 3 * hw, hw))


def kernel(x, c, w_ada, b_ada, norm_g, w_ffn1_in, w_ffn1_out, w_ffn2_in, w_ffn2_out,
           w_mix_in, w_mix_out, hgrn_lb, hgrn_norm_g, qk_norm_g, attn_sink, rel_bias):
    b, s, d = x.shape
    depth = w_ada.shape[0]
    for l in range(depth):
        mods, bias = _ada_bias(c.astype(F32), w_ada[l], b_ada[l][None, :], rel_bias)
        sh1, sc1, g1, sh2, sc2, g2, sh3, sc3, g3 = [
            mods[:, i * d:(i + 1) * d][:, None, :] for i in range(N_MOD)]
        x, w2_in, w2_out, w_mi, w_mo, shw3, shw2 = _ffn_first(
            x, sh1, sc1, g1, norm_g[l, 0][None, :], w_ffn1_in, w_ffn1_out, l,
            casts=((w_ffn2_in, None, 1, sh3), (w_ffn2_out, None, 2, None),
                   (w_mix_in, _mixin_col_map(), 1, sh2), (w_mix_out, None, 1, None)))
        qg = (jnp.tile(qk_norm_g[l, 0], 2) * (LOG2E / math.sqrt(ATT_HEAD_DIM)))[None, :]
        kg = jnp.tile(qk_norm_g[l, 1], 2)[None, :]
        hg, aq, ak, av = _mixin(x, shw2[:, None, :], sc2, norm_g[l, 1][None, :], w_mi,
                                hgrn_lb, qg, kg, l)
        o_hg = _hgrn(hg, hgrn_norm_g[l][None, :])
        att = _attn(aq, ak, av, attn_sink[l] * LOG2E, bias)
        x = _ffn_mix(o_hg, att, g2, w_mo, x, shw3[:, None, :], sc3, g3, norm_g[l, 2][None, :],
                     w2_in, w2_out)
    return x
```

```python
import functools
import math

import numpy as np
import jax
import jax.numpy as jnp
from jax import lax
from jax.experimental import pallas as pl
from jax.experimental.pallas import tpu as pltpu

F32 = jnp.float32
BF16 = jnp.bfloat16

HG_HEADS = 4
HG_DIM = 128
HG_WIDTH = HG_HEADS * HG_DIM
ATT_Q_HEADS = 8
ATT_KV_HEADS = 2
ATT_HEAD_DIM = 64
ATT_GROUP = ATT_Q_HEADS // ATT_KV_HEADS
ATT_WIDTH = ATT_Q_HEADS * ATT_HEAD_DIM
KV_WIDTH = ATT_KV_HEADS * ATT_HEAD_DIM
WINDOW = 128
BLOCK = 128
NUM_BUCKETS = 32
MAX_DISTANCE = 128
D_FF = 2816
N_MOD = 9
EPS = 1e-6

LANES = 128
PACKED_ROWS = 16
MXU_COLS = 256
VMEM_LIMIT = 56 * 1024 * 1024

FFN_TM = 512
FFN_LOAD_STEPS = 16
MIX_TM = 256
HG_CHUNK = 128
NEG_INF = float("-inf")
LOG2E = 1.0 / math.log(2.0)

Z_FF, Z_FB, Z_Q, Z_G, Z_AQ = (n * HG_WIDTH for n in range(5))
Z_AK = Z_AQ + ATT_WIDTH
Z_AV = Z_AK + KV_WIDTH
Z_I = Z_AV + KV_WIDTH
Z_END = Z_I + HG_WIDTH
(HK_Q, HK_I, HK_G, HK_KF, HK_KB, HK_GHF, HK_GMF, HK_GHB, HK_GMB) = range(9)
N_HK = 9


def _cparams(sem):
    return pltpu.CompilerParams(dimension_semantics=sem, vmem_limit_bytes=VMEM_LIMIT)


def _sigmoid(x):
    return 1.0 / (1.0 + jnp.exp2(x * (-LOG2E)))


def _t5_bucket_np(rel):
    nb = NUM_BUCKETS // 2
    max_exact = nb // 2
    ret = (rel > 0).astype(np.int32) * nb
    n = np.abs(rel)
    ratio = np.maximum(n, 1).astype(np.float32) / np.float32(max_exact)
    large = max_exact + (np.log(ratio) / np.float32(math.log(MAX_DISTANCE / max_exact))
                         * np.float32(nb - max_exact)).astype(np.int32)
    large = np.minimum(large, nb - 1)
    return ret + np.where(n < max_exact, n, large)


def _ada_bias_kernel(rb_ref, ct_ref, w_ref, b_ref, bucket_ref, mask_ref, o_ref, bias_ref):
    ct = ct_ref[...]
    ca = ct * _sigmoid(ct)
    w = w_ref[...]
    for b in range(ct.shape[1]):
        o_ref[b:b + 1, :] = jnp.sum(w * ca[:, b:b + 1], axis=0, keepdims=True) + b_ref[...]

    pair = pl.program_id(0)
    bucket = bucket_ref[...]
    for hh in range(2):
        head = 2 * pair + hh
        tile = jnp.zeros(bucket.shape, F32)
        for b in range(NUM_BUCKETS):
            tile = jnp.where(bucket == b, rb_ref[b, head], tile)
        tile = tile * LOG2E
        for e in range(3):
            bias_ref[e, 0, hh * BLOCK:(hh + 1) * BLOCK, :] = tile + mask_ref[e]


def _ada_bias(c, w, b, rel_bias):
    batch, d = c.shape
    n = w.shape[1]
    n_pairs = ATT_Q_HEADS // 2
    assert n % (n_pairs * LANES) == 0
    tn = n // n_pairs
    kcol = np.arange(3 * BLOCK)[None, :]
    rel = (kcol - BLOCK) - np.arange(BLOCK)[:, None]
    bucket = _t5_bucket_np(rel).astype(np.int32)
    window = np.abs(rel) <= WINDOW
    valid = np.stack([window & (kcol >= BLOCK), window, window & (kcol < 2 * BLOCK)])
    mask = np.where(valid, 0.0, NEG_INF).astype(np.float32)
    return pl.pallas_call(
        _ada_bias_kernel,
        grid=(n_pairs,),
        in_specs=[pl.BlockSpec(memory_space=pltpu.SMEM),
                  pl.BlockSpec((d, batch), lambda j: (0, 0)),
                  pl.BlockSpec((d, tn), lambda j: (0, j)),
                  pl.BlockSpec((1, tn), lambda j: (0, j)),
                  pl.BlockSpec(bucket.shape, lambda j: (0, 0)),
                  pl.BlockSpec(mask.shape, lambda j: (0, 0, 0))],
        out_specs=[pl.BlockSpec((batch, tn), lambda j: (0, j)),
                   pl.BlockSpec((3, 1, 2 * BLOCK, 3 * BLOCK), lambda j: (0, j, 0, 0))],
        out_shape=[jax.ShapeDtypeStruct((batch, n), F32),
                   jax.ShapeDtypeStruct((3, n_pairs, 2 * BLOCK, 3 * BLOCK), F32)],
        compiler_params=_cparams(("arbitrary",)),
        name="ada_bias",
    )(rel_bias.astype(F32), c.T, w, b, jnp.asarray(bucket), jnp.asarray(mask))


def _norm_mod(x, ng, sh, sc):
    ms = jnp.mean(x * x, axis=-1, keepdims=True)
    return (x * lax.rsqrt(ms + EPS)) * (ng * (1.0 + sc)) + sh


def _ffn_tile(x, sh_ref, sc_ref, gt_ref, ng_ref, win_ref, wout_ref, o_ref):
    h = _norm_mod(x, ng_ref[...], sh_ref[0], sc_ref[0]).astype(BF16)
    gu = jnp.dot(h, win_ref[...], preferred_element_type=F32)
    g = gu[:, :D_FF]
    u = gu[:, D_FF:]
    act = (g * _sigmoid(g) * u).astype(BF16)
    y = jnp.dot(act, wout_ref[...], preferred_element_type=F32)
    o_ref[0] = x + 0.5 * gt_ref[0] * y


def _ffn_first_kernel(x_ref, sh_ref, sc_ref, gt_ref, ng_ref, win_ref, wout_ref, *refs,
                      n_load, casts):
    src_refs = refs[:len(casts)]
    o_ref, *dst_refs = refs[len(casts):len(casts) + 1 + len(casts)]
    win_s, wout_s = refs[-2:]
    p = pl.program_id(0)

    @pl.when(p < n_load)
    def _():
        for slab_ref, full in ((win_ref, win_s), (wout_ref, wout_s)):
            rows = slab_ref.shape[1]
            full[pl.ds(pl.multiple_of(p * rows, rows), rows), :] = slab_ref[0].astype(BF16)

    @pl.when(p >= n_load)
    def _():
        _ffn_tile(x_ref[0], sh_ref, sc_ref, gt_ref, ng_ref, win_s, wout_s, o_ref)
        for src_ref, dst_ref, col_map in zip(src_refs, dst_refs, casts):
            for dst0, src0, width in col_map:
                dst_ref[:, dst0:dst0 + width] = src_ref[0, :, src0:src0 + width].astype(BF16)


def _ffn_first(x, sh, sc, gt, ng, w_in, w_out, layer, casts):
    b, s, d = x.shape
    tm = FFN_TM
    per_b = s // tm
    n_tiles = b * per_b
    n_load = FFN_LOAD_STEPS

    def tile(p):
        return jnp.maximum(p - n_load, 0)

    def own_slab(mats):
        rows = mats.shape[1]
        assert rows % (n_load * PACKED_ROWS) == 0
        return pl.BlockSpec((1, rows // n_load, mats.shape[2]),
                            lambda p: (layer, jnp.minimum(p, n_load - 1), 0))

    rows_spec = pl.BlockSpec((1, tm, d), lambda p: (tile(p) // per_b, tile(p) % per_b, 0))
    vec = pl.BlockSpec((1, 1, d), lambda p: (tile(p) // per_b, 0, 0))
    args = [x, sh, sc, gt, ng, w_in, w_out]
    specs = [rows_spec, vec, vec, vec, pl.BlockSpec(ng.shape, lambda p: (0, 0)),
             own_slab(w_in), own_slab(w_out)]
    out_specs = [rows_spec]
    out_shape = [jax.ShapeDtypeStruct(x.shape, F32)]
    col_maps = []
    for mats, col_map, dup in casts:
        _, rows, cols = mats.shape
        assert n_tiles % dup == 0
        n_slabs = n_tiles // dup
        assert rows % (n_slabs * PACKED_ROWS) == 0 and cols % LANES == 0
        slab = rows // n_slabs
        args.append(mats)
        specs.append(pl.BlockSpec((1, slab, cols), lambda p, dup=dup: (layer, tile(p) // dup, 0)))
        out_specs.append(pl.BlockSpec((slab, cols), lambda p, dup=dup: (tile(p) // dup, 0)))
        out_shape.append(jax.ShapeDtypeStruct((rows, cols), BF16))
        col_maps.append(tuple(col_map) if col_map is not None else ((0, 0, cols),))
    return pl.pallas_call(
        functools.partial(_ffn_first_kernel, n_load=n_load, casts=tuple(col_maps)),
        grid=(n_load + n_tiles,),
        in_specs=specs,
        out_specs=out_specs,
        out_shape=out_shape,
        scratch_shapes=[pltpu.VMEM(w_in.shape[1:], BF16), pltpu.VMEM(w_out.shape[1:], BF16)],
        compiler_params=_cparams(("arbitrary",)),
        name="ffn",
    )(*args)


def _ffn_mix_kernel(hg_ref, at_ref, gm_ref, wm_ref, x_ref, sh_ref, sc_ref, gt_ref, ng_ref,
                    win_ref, wout_ref, o_ref):
    mixed = (jnp.dot(hg_ref[0], wm_ref[:HG_WIDTH, :], preferred_element_type=F32)
             + jnp.dot(at_ref[0], wm_ref[HG_WIDTH:, :], preferred_element_type=F32))
    x = x_ref[0] + gm_ref[0] * mixed
    _ffn_tile(x, sh_ref, sc_ref, gt_ref, ng_ref, win_ref, wout_ref, o_ref)


def _ffn_mix(o_hg, att, gm, wm, x, sh, sc, gt, ng, w_in, w_out):
    b, s, d = x.shape
    tm = FFN_TM
    vec = pl.BlockSpec((1, 1, d), lambda i, j: (i, 0, 0))
    rows = lambda n: pl.BlockSpec((1, tm, n), lambda i, j: (i, j, 0))
    whole = lambda a: pl.BlockSpec(a.shape, lambda i, j: (0, 0), pipeline_mode=pl.Buffered(1))
    return pl.pallas_call(
        _ffn_mix_kernel,
        grid=(b, s // tm),
        in_specs=[rows(HG_WIDTH), rows(ATT_WIDTH), vec, whole(wm), rows(d), vec, vec, vec,
                  whole(ng), whole(w_in), whole(w_out)],
        out_specs=rows(d),
        out_shape=jax.ShapeDtypeStruct(x.shape, F32),
        compiler_params=_cparams(("arbitrary", "arbitrary")),
        name="ffn_mix",
    )(o_hg, att, gm, wm, x, sh, sc, gt, ng, w_in, w_out)


def _half_norm(x, gain, lo):
    x2 = x * x
    s_lo = jnp.sum(jnp.where(lo, x2, 0.0), axis=-1, keepdims=True)
    s_hi = jnp.sum(jnp.where(lo, 0.0, x2), axis=-1, keepdims=True)
    ms = jnp.where(lo, s_lo, s_hi) * (1.0 / ATT_HEAD_DIM)
    return x * lax.rsqrt(ms + EPS) * gain


def _mixin_kernel(x_ref, sh_ref, sc_ref, ng_ref, w_ref, lb_ref, qg_ref, kg_ref,
                  hg_ref, aq_ref, ak_ref, av_ref, *, layer):
    h = _norm_mod(x_ref[0], ng_ref[...], sh_ref[0], sc_ref[0]).astype(BF16)

    def proj(c0, c1):
        return jnp.dot(h, w_ref[:, c0:c1], preferred_element_type=F32)

    part_w = HG_WIDTH // 2

    def put_heads(kind, val, part):
        for hh in range(part_w // HG_DIM):
            head = part * (part_w // HG_DIM) + hh
            hg_ref[0, kind * HG_HEADS + head] = val[:, hh * HG_DIM:(hh + 1) * HG_DIM].astype(BF16)

    raw = lb_ref[...]
    mx = jnp.max(raw, axis=1, keepdims=True)
    ex = jnp.exp(raw - mx)
    lb = jnp.sum(ex[:, :layer + 1, :], axis=1) / jnp.sum(ex, axis=1)

    def forget(d, z0, kk, kh, km, part):
        cols = slice(part * part_w, (part + 1) * part_w)
        fr = proj(z0 + cols.start, z0 + cols.stop)
        lbd = lb[d:d + 1, cols]
        f = lbd + (1.0 - lbd) * _sigmoid(fr)
        g2 = jnp.log(f) * LOG2E
        hi = g2.astype(BF16)
        put_heads(kk, 1.0 - f, part)
        put_heads(kh, hi, part)
        put_heads(km, g2 - hi.astype(F32), part)

    def plain(kind, z0, part, silu):
        val = proj(z0 + part * part_w, z0 + (part + 1) * part_w)
        put_heads(kind, val * _sigmoid(val) if silu else val, part)

    lo = lax.broadcasted_iota(jnp.int32, (h.shape[0], LANES), 1) < LANES // 2

    def queries(part):
        aq = proj(Z_AQ + part * part_w, Z_AQ + (part + 1) * part_w)
        for m in range(part_w // LANES):
            cols = slice(m * LANES, (m + 1) * LANES)
            aq_ref[0, :, part * part_w + m * LANES:part * part_w + (m + 1) * LANES] = (
                _half_norm(aq[:, cols], qg_ref[...], lo).astype(BF16))

    def keys_values():
        akv = proj(Z_AK, Z_I)
        ak_ref[0] = _half_norm(akv[:, :KV_WIDTH], kg_ref[...], lo).T.astype(BF16)
        av_ref[0] = akv[:, KV_WIDTH:].astype(BF16)

    forget(0, Z_FF, HK_KF, HK_GHF, HK_GMF, 0)
    plain(HK_I, Z_I, 0, False)
    forget(0, Z_FF, HK_KF, HK_GHF, HK_GMF, 1)
    plain(HK_I, Z_I, 1, False)
    forget(1, Z_FB, HK_KB, HK_GHB, HK_GMB, 0)
    keys_values()
    forget(1, Z_FB, HK_KB, HK_GHB, HK_GMB, 1)
    plain(HK_Q, Z_Q, 0, True)
    queries(0)
    plain(HK_Q, Z_Q, 1, True)
    queries(1)
    plain(HK_G, Z_G, 0, True)
    plain(HK_G, Z_G, 1, True)


def _mixin(x, sh, sc, ng, w, hgrn_lb, qg, kg, layer):
    b, s, d = x.shape
    tm = MIX_TM
    vec = pl.BlockSpec((1, 1, d), lambda i, j: (i, 0, 0))
    row = lambda n: pl.BlockSpec((1, n), lambda i, j: (0, 0))
    rows3 = lambda n: pl.BlockSpec((1, tm, n), lambda i, j: (i, j, 0))
    n_hg = N_HK * HG_HEADS
    return pl.pallas_call(
        functools.partial(_mixin_kernel, layer=layer),
        grid=(b, s // tm),
        in_specs=[pl.BlockSpec((1, tm, d), lambda i, j: (i, j, 0)),
                  vec, vec, row(d),
                  pl.BlockSpec(w.shape, lambda i, j: (0, 0)),
                  pl.BlockSpec(hgrn_lb.shape, lambda i, j: (0, 0, 0)),
                  row(LANES), row(LANES)],
        out_specs=[pl.BlockSpec((1, n_hg, tm, HG_DIM), lambda i, j: (i, 0, j, 0)),
                   rows3(ATT_WIDTH),
                   pl.BlockSpec((1, KV_WIDTH, tm), lambda i, j: (i, 0, j)),
                   rows3(KV_WIDTH)],
        out_shape=[jax.ShapeDtypeStruct((b, n_hg, s, HG_DIM), BF16),
                   jax.ShapeDtypeStruct((b, s, ATT_WIDTH), BF16),
                   jax.ShapeDtypeStruct((b, KV_WIDTH, s), BF16),
                   jax.ShapeDtypeStruct((b, s, KV_WIDTH), BF16)],
        compiler_params=_cparams(("arbitrary", "arbitrary")),
        name="mix_in",
    )(x, sh, sc, ng, w, hgrn_lb, qg, kg)


SMALL_W = (1, 2, 4, 8)
GROUP = PACKED_ROWS
FINISH_CHUNKS = 8
PAIRS_PER_BODY = 16


def _hgrn_consts(c):
    t = np.arange(c)[:, None]
    s = np.arange(c)[None, :]
    x = t ^ s
    lev = np.where(x > 0, np.floor(np.log2(np.maximum(x, 1))), -1).astype(np.int32)
    lev_f = np.where(t > s, lev, np.where(t == s, -1, -2)).astype(np.int32)
    lev_b = lev_f.T.copy()

    def exponent_rows(w, rev):
        m = np.zeros((GROUP, GROUP), np.float32)
        for r in range(GROUP):
            b0 = (r // (2 * w)) * 2 * w
            if not rev:
                ref = b0 + w - 1
                lo_u, hi_u = (ref + 1, r) if r > ref else (r + 1, ref)
            else:
                ref = b0 + w
                lo_u, hi_u = (r, ref - 1) if r < ref else (ref, r - 1)
            m[r, lo_u:hi_u + 1] = 1.0
        return m

    cums, smalls = [], []
    for rev in (False, True):
        cum = ((s >= t) if rev else (s <= t)).astype(np.float32)
        cums.append(np.concatenate([cum, cum], axis=1))
        rows = np.concatenate([exponent_rows(w, rev) for w in SMALL_W])
        smalls.append(np.concatenate([rows, rows], axis=1))
    return (jnp.asarray(np.stack([lev_f, lev_b])),
            jnp.asarray(np.stack(cums), dtype=BF16),
            jnp.asarray(np.stack(smalls), dtype=BF16))


def _neg_abs(x):
    bits = lax.bitcast_convert_type(x, jnp.uint32) | jnp.uint32(0x80000000)
    return lax.bitcast_convert_type(bits, F32)


def _ref_rows(g_cum, w, rev, c):
    idx = w if rev else w - 1
    g3 = g_cum.reshape(c // (2 * w), 2 * w, LANES)
    return jnp.broadcast_to(g3[:, idx:idx + 1, :], g3.shape).reshape(c, LANES)


def _hgrn_stage_a(tcum, tsmall, gh, gm, gc_ref, ge_ref, c):
    hm = jnp.concatenate([gh, gm], axis=0)
    gc_ref[...] = jnp.dot(tcum, hm, preferred_element_type=F32)
    yield
    per_tile = MXU_COLS // LANES
    for t in range(c // GROUP // per_tile):
        groups = [slice(g * GROUP, (g + 1) * GROUP) for g in range(t * per_tile, (t + 1) * per_tile)]
        wide = jnp.concatenate([jnp.concatenate([gh[g] for g in groups], axis=1),
                                jnp.concatenate([gm[g] for g in groups], axis=1)], axis=0)
        ge_ref[:, t * MXU_COLS:(t + 1) * MXU_COLS] = jnp.dot(tsmall, wide,
                                                              preferred_element_type=F32)
        yield


def _hgrn_stage_b(q, k, v, gc_ref, ge_ref, lev, st_ref, a_ref, ab_ref, oi_ref, keep, rev, c):
    tn = (((0,), (0,)), ((), ()))

    def transposed(x):
        return jnp.where(keep, x.astype(BF16).T, jnp.zeros((), BF16))

    g_cum = gc_ref[...]
    g_last = g_cum[0:1, :] if rev else g_cum[c - 1:c, :]
    qf, kf = q.astype(F32), k.astype(F32)

    st = st_ref[...]
    qi = (qf * jnp.exp2(g_cum)).astype(BF16)
    oi_ref[...] = jnp.dot(qi, transposed(st), preferred_element_type=F32)
    kd = (kf * jnp.exp2(g_last - g_cum)).astype(BF16)
    u_t = lax.dot_general(v, kd, tn, preferred_element_type=F32)
    st_ref[...] = jnp.exp2(g_last) * st + u_t
    yield

    w = c // 2
    while w >= GROUP:
        x = jnp.exp2(_neg_abs(g_cum - _ref_rows(g_cum, w, rev, c)))
        q_rows, k_rows = [], []
        for b in range(c // (2 * w)):
            lo_half = slice(b * 2 * w, b * 2 * w + w)
            hi_half = slice(b * 2 * w + w, (b + 1) * 2 * w)
            qs = lo_half if rev else hi_half
            q_rows.append(qf[qs] * x[qs])
            k_rows.extend([kf[lo_half], kf[hi_half] * x[hi_half]] if rev
                          else [kf[lo_half] * x[lo_half], kf[hi_half]])
        p = jnp.dot(jnp.concatenate(q_rows, axis=0).astype(BF16),
                    transposed(jnp.concatenate(k_rows, axis=0)),
                    preferred_element_type=F32)
        for b in range(c // (2 * w)):
            lo_half = slice(b * 2 * w, b * 2 * w + w)
            hi_half = slice(b * 2 * w + w, (b + 1) * 2 * w)
            qs, ks = (lo_half, hi_half) if rev else (hi_half, lo_half)
            a_ref[qs, ks] = p[b * w:(b + 1) * w, ks]
        w //= 2
        yield

    a = jnp.where(lev == -1, jnp.dot(q, transposed(k), preferred_element_type=F32), 0.0)
    for li in range(len(SMALL_W)):
        e = jnp.concatenate([ge_ref[li * GROUP:(li + 1) * GROUP, g * LANES:(g + 1) * LANES]
                             for g in range(c // GROUP)], axis=0)
        x = jnp.exp2(e)
        p = jnp.dot((qf * x).astype(BF16), transposed(kf * x), preferred_element_type=F32)
        a = jnp.where(lev == li, p, a)
        yield
    for b in range(c // GROUP):
        blk = slice(b * GROUP, (b + 1) * GROUP)
        a_ref[blk, blk] = a[blk, blk]
    ab_ref[...] = a_ref[...].astype(BF16)
    yield


def _interleave(main, fill):
    for _ in main:
        next(fill, None)
    for _ in fill:
        pass


def _hgrn_kernel(q_ref, i_ref, sg_ref, kf_ref, kb_ref, ghf_ref, gmf_ref, ghb_ref, gmb_ref,
                 ng_ref, lev_ref, tc_ref, ts_ref, o_ref,
                 acc_ref, st_ref, a_ref, ab_ref, oi_ref, gc_ref, ge_ref, *, c, nc):
    for ref in (acc_ref, st_ref, a_ref, ab_ref, oi_ref):
        ref[...] = jnp.zeros_like(ref)
    k_refs = (kf_ref, kb_ref)
    g_refs = ((ghf_ref, gmf_ref), (ghb_ref, gmb_ref))
    keep = pl.program_id(0) >= 0

    def rows_of(p, d):
        j = p if d == 0 else nc - 1 - p
        return pl.ds(pl.multiple_of(j * c, c), c)

    def stage_a(p, slot, d):
        rows = rows_of(p, d)
        return _hgrn_stage_a(tc_ref[d], ts_ref[d], g_refs[d][0][0, 0, rows, :],
                             g_refs[d][1][0, 0, rows, :], gc_ref.at[slot, d], ge_ref.at[slot, d], c)

    def stage_t(p, d):
        rows = rows_of(p, d)
        acc_ref[rows, :] += oi_ref[d] + jnp.dot(ab_ref[d], i_ref[0, 0, rows, :],
                                                preferred_element_type=F32)

    def stage_b(p, slot, d):
        rows = rows_of(p, d)
        return _hgrn_stage_b(q_ref[0, 0, rows, :], k_refs[d][0, 0, rows, :], i_ref[0, 0, rows, :],
                             gc_ref.at[slot, d], ge_ref.at[slot, d], lev_ref[d], st_ref.at[d],
                             a_ref.at[d], ab_ref.at[d], oi_ref.at[d], keep, d == 1, c)

    def pair(p, slot):
        p_next, p_prev = jnp.minimum(p + 1, nc - 1), jnp.maximum(p - 1, 0)
        for d in range(2):
            stage_t(p_prev, d)
            _interleave(stage_b(p, slot, d), stage_a(p_next, 1 - slot, d))

    def pairs(i, carry):
        for u in range(PAIRS_PER_BODY):
            pair(PAIRS_PER_BODY * i + u, u % 2)
        return carry

    for d in range(2):
        for _ in stage_a(0, 0, d):
            pass
    lax.fori_loop(0, nc // PAIRS_PER_BODY, pairs, 0)
    for d in range(2):
        stage_t(nc - 1, d)

    ng = ng_ref[...]
    rows_fin = FINISH_CHUNKS * c

    def finish(j, carry):
        rows = pl.ds(pl.multiple_of(j * rows_fin, rows_fin), rows_fin)
        o = acc_ref[rows, :]
        ms = jnp.mean(o * o, axis=-1, keepdims=True)
        o = o * lax.rsqrt(ms + EPS) * ng
        o_ref[0, rows, :] = (o * sg_ref[0, 0, rows, :].astype(F32)).astype(o_ref.dtype)
        return carry

    lax.fori_loop(0, nc // FINISH_CHUNKS, finish, 0)


def _hgrn(hg, norm_g):
    b, _, s, _ = hg.shape
    c = HG_CHUNK
    nc = s // c
    assert PAIRS_PER_BODY % 2 == 0 and nc % PAIRS_PER_BODY == 0 and c % (2 * GROUP) == 0
    lev, tcum, tsmall = _hgrn_consts(c)
    n_small = len(SMALL_W) * GROUP

    def kind(k):
        return pl.BlockSpec((1, 1, s, HG_DIM), lambda i, h: (i, k * HG_HEADS + h, 0, 0))

    const3 = lambda a: pl.BlockSpec(a.shape, lambda i, h: (0, 0, 0))
    return pl.pallas_call(
        functools.partial(_hgrn_kernel, c=c, nc=nc),
        grid=(b, HG_HEADS),
        in_specs=[kind(HK_Q), kind(HK_I), kind(HK_G), kind(HK_KF), kind(HK_KB),
                  kind(HK_GHF), kind(HK_GMF), kind(HK_GHB), kind(HK_GMB),
                  pl.BlockSpec((1, LANES), lambda i, h: (0, h)),
                  const3(lev), const3(tcum), const3(tsmall)],
        out_specs=pl.BlockSpec((1, s, LANES), lambda i, h: (i, 0, h)),
        out_shape=jax.ShapeDtypeStruct((b, s, HG_WIDTH), BF16),
        scratch_shapes=[pltpu.VMEM((s, HG_DIM), F32),
                        pltpu.VMEM((2, HG_DIM, HG_DIM), F32),
                        pltpu.VMEM((2, c, c), F32),
                        pltpu.VMEM((2, c, c), BF16),
                        pltpu.VMEM((2, c, HG_DIM), F32),
                        pltpu.VMEM((2, 2, c, LANES), F32),
                        pltpu.VMEM((2, 2, n_small, (c // GROUP) * LANES), F32)],
        compiler_params=_cparams(("arbitrary", "arbitrary")),
        name="hgrn",
    )(hg, hg, hg, hg, hg, hg, hg, hg, hg, norm_g, lev, tcum, tsmall)


def _attn_kernel(sink_ref, q_ref, kp_ref, ko_ref, kn_ref, vp_ref, vo_ref, vn_ref,
                 bias_ref, o_ref, lg_ref, *, nb):
    n = pl.program_id(1)
    half = LANES // 2
    nk = 3 * BLOCK
    kt = jnp.concatenate([kp_ref[0], ko_ref[0], kn_ref[0]], axis=1)
    vb = jnp.concatenate([vp_ref[0], vo_ref[0], vn_ref[0]], axis=0)
    lo = lax.broadcasted_iota(jnp.int32, (BLOCK, LANES), 1) < half
    lo_v = lax.broadcasted_iota(jnp.int32, vb.shape, 1) < half
    zero_v = jnp.zeros_like(vb)
    zero_k = jnp.zeros((half, nk), kt.dtype)

    k_both = tuple(
        jnp.concatenate([jnp.concatenate([kj, zero_k], axis=0),
                         jnp.concatenate([zero_k, kj], axis=0)], axis=1)
        for kj in (kt[:half], kt[half:]))
    v0lo, v1hi = jnp.where(lo_v, vb, zero_v), jnp.where(lo_v, zero_v, vb)
    v_both = (jnp.concatenate([v0lo, pltpu.roll(v0lo, half, axis=1)], axis=0),
              jnp.concatenate([pltpu.roll(v1hi, half, axis=1), v1hi], axis=0))
    edge = jnp.where(n == 0, 0, jnp.where(n == nb - 1, 2, 1))

    for m in range(ATT_Q_HEADS // 2):
        lg_ref[m] = jnp.dot(q_ref[0, :, m * LANES:(m + 1) * LANES], k_both[(2 * m) // ATT_GROUP],
                            preferred_element_type=F32)

    for m in range(ATT_Q_HEADS // 2):
        ps, rdens = [], []
        for hh in range(2):
            lg = (lg_ref[m, :, hh * nk:(hh + 1) * nk]
                  + bias_ref[edge, m, hh * BLOCK:(hh + 1) * BLOCK, :])
            sink = sink_ref[2 * m + hh]
            mx = jnp.maximum(jnp.max(lg, axis=-1, keepdims=True), sink)
            p = jnp.exp2(lg - mx)
            rdens.append(1.0 / (jnp.sum(p, axis=-1, keepdims=True) + jnp.exp2(sink - mx)))
            ps.append(p.astype(BF16))
        o = jnp.dot(jnp.concatenate(ps, axis=1), v_both[(2 * m) // ATT_GROUP],
                    preferred_element_type=F32)
        o_ref[0, :, m * LANES:(m + 1) * LANES] = (
            o * jnp.where(lo, rdens[0], rdens[1])).astype(o_ref.dtype)


def _attn(aq, ak, av, sink2, bias):
    b, s, _ = aq.shape
    nb = s // BLOCK
    assert nb >= 2

    def neighbour(n, shift):
        return jnp.clip(n + shift, 0, nb - 1)

    kt = lambda shift: pl.BlockSpec((1, KV_WIDTH, BLOCK), lambda i, n: (i, 0, neighbour(n, shift)))
    v = lambda shift: pl.BlockSpec((1, BLOCK, KV_WIDTH), lambda i, n: (i, neighbour(n, shift), 0))
    return pl.pallas_call(
        functools.partial(_attn_kernel, nb=nb),
        grid=(b, nb),
        in_specs=[pl.BlockSpec(memory_space=pltpu.SMEM),
                  pl.BlockSpec((1, BLOCK, ATT_WIDTH), lambda i, n: (i, n, 0)),
                  kt(-1), kt(0), kt(1), v(-1), v(0), v(1),
                  pl.BlockSpec(bias.shape, lambda i, n: (0, 0, 0, 0))],
        out_specs=pl.BlockSpec((1, BLOCK, ATT_WIDTH), lambda i, n: (i, n, 0)),
        out_shape=jax.ShapeDtypeStruct((b, s, ATT_WIDTH), BF16),
        scratch_shapes=[pltpu.VMEM((ATT_Q_HEADS // 2, BLOCK, 2 * 3 * BLOCK), F32)],
        compiler_params=_cparams(("arbitrary", "arbitrary")),
        name="attn",
    )(sink2, aq, ak, ak, ak, av, av, av, bias)


def _mixin_col_map():
    hw = HG_WIDTH
    att = ATT_WIDTH + 2 * KV_WIDTH
    return ((Z_FF, hw, 2 * hw), (Z_Q, 0, hw), (Z_G, 4 * hw, hw), (Z_AQ, 5 * hw, att),
            (Z_I, 3 * hw, hw))


def kernel(x, c, w_ada, b_ada, norm_g, w_ffn1_in, w_ffn1_out, w_ffn2_in, w_ffn2_out,
           w_mix_in, w_mix_out, hgrn_lb, hgrn_norm_g, qk_norm_g, attn_sink, rel_bias):
    b, s, d = x.shape
    depth = w_ada.shape[0]
    for l in range(depth):
        mods, bias = _ada_bias(c.astype(F32), w_ada[l], b_ada[l][None, :], rel_bias)
        sh1, sc1, g1, sh2, sc2, g2, sh3, sc3, g3 = [
            mods[:, i * d:(i + 1) * d][:, None, :] for i in range(N_MOD)]
        x, w2_in, w2_out, w_mi, w_mo = _ffn_first(
            x, sh1, sc1, g1, norm_g[l, 0][None, :], w_ffn1_in, w_ffn1_out, l,
            casts=((w_ffn2_in, None, 1), (w_ffn2_out, None, 2),
                   (w_mix_in, _mixin_col_map(), 1), (w_mix_out, None, 1)))
        qg = (jnp.tile(qk_norm_g[l, 0], 2) * (LOG2E / math.sqrt(ATT_HEAD_DIM)))[None, :]
        kg = jnp.tile(qk_norm_g[l, 1], 2)[None, :]
        hg, aq, ak, av = _mixin(x, sh2, sc2, norm_g[l, 1][None, :], w_mi, hgrn_lb, qg, kg, l)
        o_hg = _hgrn(hg, hgrn_norm_g[l][None, :])
        att = _attn(aq, ak, av, attn_sink[l] * LOG2E, bias)
        x = _ffn_mix(o_hg, att, g2, w_mo, x, sh3, sc3, g3, norm_g[l, 2][None, :], w2_in, w2_out)
    return x
```

```python
import functools
import math

import numpy as np
import jax
import jax.numpy as jnp
from jax import lax
from jax.experimental import pallas as pl
from jax.experimental.pallas import tpu as pltpu

F32 = jnp.float32
BF16 = jnp.bfloat16

HG_HEADS = 4
HG_DIM = 128
HG_WIDTH = HG_HEADS * HG_DIM
ATT_Q_HEADS = 8
ATT_KV_HEADS = 2
ATT_HEAD_DIM = 64
ATT_GROUP = ATT_Q_HEADS // ATT_KV_HEADS
ATT_WIDTH = ATT_Q_HEADS * ATT_HEAD_DIM
KV_WIDTH = ATT_KV_HEADS * ATT_HEAD_DIM
WINDOW = 128
BLOCK = 128
NUM_BUCKETS = 32
MAX_DISTANCE = 128
D_FF = 2816
N_MOD = 9
EPS = 1e-6

LANES = 128
PACKED_ROWS = 16
MXU_COLS = 256
VMEM_LIMIT = 56 * 1024 * 1024

FFN_TM = 512
FFN_LOAD_STEPS = 16
MIX_TM = 256
HG_CHUNK = 128
NEG_INF = float("-inf")
LOG2E = 1.0 / math.log(2.0)

Z_FF, Z_FB, Z_Q, Z_G, Z_AQ = (n * HG_WIDTH for n in range(5))
Z_AK = Z_AQ + ATT_WIDTH
Z_AV = Z_AK + KV_WIDTH
Z_I = Z_AV + KV_WIDTH
Z_END = Z_I + HG_WIDTH
(HK_Q, HK_I, HK_G, HK_KF, HK_KB, HK_GHF, HK_GMF, HK_GHB, HK_GMB) = range(9)
N_HK = 9


def _cparams(sem):
    return pltpu.CompilerParams(dimension_semantics=sem, vmem_limit_bytes=VMEM_LIMIT)


def _sigmoid(x):
    return 1.0 / (1.0 + jnp.exp2(x * (-LOG2E)))


def _t5_bucket_np(rel):
    nb = NUM_BUCKETS // 2
    max_exact = nb // 2
    ret = (rel > 0).astype(np.int32) * nb
    n = np.abs(rel)
    ratio = np.maximum(n, 1).astype(np.float32) / np.float32(max_exact)
    large = max_exact + (np.log(ratio) / np.float32(math.log(MAX_DISTANCE / max_exact))
                         * np.float32(nb - max_exact)).astype(np.int32)
    large = np.minimum(large, nb - 1)
    return ret + np.where(n < max_exact, n, large)


def _ada_bias_kernel(rb_ref, ct_ref, w_ref, b_ref, bucket_ref, mask_ref, o_ref, bias_ref):
    ct = ct_ref[...]
    ca = ct * _sigmoid(ct)
    w = w_ref[...]
    for b in range(ct.shape[1]):
        o_ref[b:b + 1, :] = jnp.sum(w * ca[:, b:b + 1], axis=0, keepdims=True) + b_ref[...]

    pair = pl.program_id(0)
    bucket = bucket_ref[...]
    for hh in range(2):
        head = 2 * pair + hh
        tile = jnp.zeros(bucket.shape, F32)
        for b in range(NUM_BUCKETS):
            tile = jnp.where(bucket == b, rb_ref[b, head], tile)
        tile = tile * LOG2E
        for e in range(3):
            bias_ref[e, 0, hh * BLOCK:(hh + 1) * BLOCK, :] = tile + mask_ref[e]


def _ada_bias(c, w, b, rel_bias):
    batch, d = c.shape
    n = w.shape[1]
    n_pairs = ATT_Q_HEADS // 2
    assert n % (n_pairs * LANES) == 0
    tn = n // n_pairs
    kcol = np.arange(3 * BLOCK)[None, :]
    rel = (kcol - BLOCK) - np.arange(BLOCK)[:, None]
    bucket = _t5_bucket_np(rel).astype(np.int32)
    window = np.abs(rel) <= WINDOW
    valid = np.stack([window & (kcol >= BLOCK), window, window & (kcol < 2 * BLOCK)])
    mask = np.where(valid, 0.0, NEG_INF).astype(np.float32)
    return pl.pallas_call(
        _ada_bias_kernel,
        grid=(n_pairs,),
        in_specs=[pl.BlockSpec(memory_space=pltpu.SMEM),
                  pl.BlockSpec((d, batch), lambda j: (0, 0)),
                  pl.BlockSpec((d, tn), lambda j: (0, j)),
                  pl.BlockSpec((1, tn), lambda j: (0, j)),
                  pl.BlockSpec(bucket.shape, lambda j: (0, 0)),
                  pl.BlockSpec(mask.shape, lambda j: (0, 0, 0))],
        out_specs=[pl.BlockSpec((batch, tn), lambda j: (0, j)),
                   pl.BlockSpec((3, 1, 2 * BLOCK, 3 * BLOCK), lambda j: (0, j, 0, 0))],
        out_shape=[jax.ShapeDtypeStruct((batch, n), F32),
                   jax.ShapeDtypeStruct((3, n_pairs, 2 * BLOCK, 3 * BLOCK), F32)],
        compiler_params=_cparams(("arbitrary",)),
        name="ada_bias",
    )(rel_bias.astype(F32), c.T, w, b, jnp.asarray(bucket), jnp.asarray(mask))


def _norm_mod(x, ng, sh, sc):
    ms = jnp.mean(x * x, axis=-1, keepdims=True)
    return (x * lax.rsqrt(ms + EPS)) * (ng * (1.0 + sc)) + sh


def _ffn_tile(x, sh_ref, sc_ref, gt_ref, ng_ref, win_ref, wout_ref, o_ref):
    h = _norm_mod(x, ng_ref[...], sh_ref[0], sc_ref[0]).astype(BF16)
    gu = jnp.dot(h, win_ref[...], preferred_element_type=F32)
    g = gu[:, :D_FF]
    u = gu[:, D_FF:]
    act = (g * _sigmoid(g) * u).astype(BF16)
    y = jnp.dot(act, wout_ref[...], preferred_element_type=F32)
    o_ref[0] = x + 0.5 * gt_ref[0] * y


def _ffn_first_kernel(x_ref, sh_ref, sc_ref, gt_ref, ng_ref, win_ref, wout_ref, *refs,
                      n_load, casts):
    src_refs = refs[:len(casts)]
    o_ref, *dst_refs = refs[len(casts):len(casts) + 1 + len(casts)]
    win_s, wout_s = refs[-2:]
    p = pl.program_id(0)

    @pl.when(p < n_load)
    def _():
        for slab_ref, full in ((win_ref, win_s), (wout_ref, wout_s)):
            rows = slab_ref.shape[1]
            full[pl.ds(pl.multiple_of(p * rows, rows), rows), :] = slab_ref[0].astype(BF16)

    @pl.when(p >= n_load)
    def _():
        _ffn_tile(x_ref[0], sh_ref, sc_ref, gt_ref, ng_ref, win_s, wout_s, o_ref)
        for src_ref, dst_ref, col_map in zip(src_refs, dst_refs, casts):
            for dst0, src0, width in col_map:
                dst_ref[:, dst0:dst0 + width] = src_ref[0, :, src0:src0 + width].astype(BF16)


def _ffn_first(x, sh, sc, gt, ng, w_in, w_out, layer, casts):
    b, s, d = x.shape
    tm = FFN_TM
    per_b = s // tm
    n_tiles = b * per_b
    n_load = FFN_LOAD_STEPS

    def tile(p):
        return jnp.maximum(p - n_load, 0)

    def own_slab(mats):
        rows = mats.shape[1]
        assert rows % (n_load * PACKED_ROWS) == 0
        return pl.BlockSpec((1, rows // n_load, mats.shape[2]),
                            lambda p: (layer, jnp.minimum(p, n_load - 1), 0))

    rows_spec = pl.BlockSpec((1, tm, d), lambda p: (tile(p) // per_b, tile(p) % per_b, 0))
    vec = pl.BlockSpec((1, 1, d), lambda p: (tile(p) // per_b, 0, 0))
    args = [x, sh, sc, gt, ng, w_in, w_out]
    specs = [rows_spec, vec, vec, vec, pl.BlockSpec(ng.shape, lambda p: (0, 0)),
             own_slab(w_in), own_slab(w_out)]
    out_specs = [rows_spec]
    out_shape = [jax.ShapeDtypeStruct(x.shape, F32)]
    col_maps = []
    for mats, col_map, dup in casts:
        _, rows, cols = mats.shape
        assert n_tiles % dup == 0
        n_slabs = n_tiles // dup
        assert rows % (n_slabs * PACKED_ROWS) == 0 and cols % LANES == 0
        slab = rows // n_slabs
        args.append(mats)
        specs.append(pl.BlockSpec((1, slab, cols), lambda p, dup=dup: (layer, tile(p) // dup, 0)))
        out_specs.append(pl.BlockSpec((slab, cols), lambda p, dup=dup: (tile(p) // dup, 0)))
        out_shape.append(jax.ShapeDtypeStruct((rows, cols), BF16))
        col_maps.append(tuple(col_map) if col_map is not None else ((0, 0, cols),))
    return pl.pallas_call(
        functools.partial(_ffn_first_kernel, n_load=n_load, casts=tuple(col_maps)),
        grid=(n_load + n_tiles,),
        in_specs=specs,
        out_specs=out_specs,
        out_shape=out_shape,
        scratch_shapes=[pltpu.VMEM(w_in.shape[1:], BF16), pltpu.VMEM(w_out.shape[1:], BF16)],
        compiler_params=_cparams(("arbitrary",)),
        name="ffn",
    )(*args)


def _ffn_mix_kernel(hg_ref, at_ref, gm_ref, wm_ref, x_ref, sh_ref, sc_ref, gt_ref, ng_ref,
                    win_ref, wout_ref, o_ref):
    mixed = (jnp.dot(hg_ref[0], wm_ref[:HG_WIDTH, :], preferred_element_type=F32)
             + jnp.dot(at_ref[0], wm_ref[HG_WIDTH:, :], preferred_element_type=F32))
    x = x_ref[0] + gm_ref[0] * mixed
    _ffn_tile(x, sh_ref, sc_ref, gt_ref, ng_ref, win_ref, wout_ref, o_ref)


def _ffn_mix(o_hg, att, gm, wm, x, sh, sc, gt, ng, w_in, w_out):
    b, s, d = x.shape
    tm = FFN_TM
    vec = pl.BlockSpec((1, 1, d), lambda i, j: (i, 0, 0))
    rows = lambda n: pl.BlockSpec((1, tm, n), lambda i, j: (i, j, 0))
    whole = lambda a: pl.BlockSpec(a.shape, lambda i, j: (0, 0), pipeline_mode=pl.Buffered(1))
    return pl.pallas_call(
        _ffn_mix_kernel,
        grid=(b, s // tm),
        in_specs=[rows(HG_WIDTH), rows(ATT_WIDTH), vec, whole(wm), rows(d), vec, vec, vec,
                  whole(ng), whole(w_in), whole(w_out)],
        out_specs=rows(d),
        out_shape=jax.ShapeDtypeStruct(x.shape, F32),
        compiler_params=_cparams(("arbitrary", "arbitrary")),
        name="ffn_mix",
    )(o_hg, att, gm, wm, x, sh, sc, gt, ng, w_in, w_out)


def _half_norm(x, gain, lo):
    x2 = x * x
    s_lo = jnp.sum(jnp.where(lo, x2, 0.0), axis=-1, keepdims=True)
    s_hi = jnp.sum(jnp.where(lo, 0.0, x2), axis=-1, keepdims=True)
    ms = jnp.where(lo, s_lo, s_hi) * (1.0 / ATT_HEAD_DIM)
    return x * lax.rsqrt(ms + EPS) * gain


def _mixin_kernel(x_ref, sh_ref, sc_ref, ng_ref, w_ref, lb_ref, qg_ref, kg_ref,
                  hg_ref, aq_ref, ak_ref, av_ref, *, layer):
    h = _norm_mod(x_ref[0], ng_ref[...], sh_ref[0], sc_ref[0]).astype(BF16)

    def proj(c0, c1):
        return jnp.dot(h, w_ref[:, c0:c1], preferred_element_type=F32)

    part_w = HG_WIDTH // 2

    def put_heads(kind, val, part):
        for hh in range(part_w // HG_DIM):
            head = part * (part_w // HG_DIM) + hh
            hg_ref[0, kind * HG_HEADS + head] = val[:, hh * HG_DIM:(hh + 1) * HG_DIM].astype(BF16)

    raw = lb_ref[...]
    mx = jnp.max(raw, axis=1, keepdims=True)
    ex = jnp.exp(raw - mx)
    lb = jnp.sum(ex[:, :layer + 1, :], axis=1) / jnp.sum(ex, axis=1)

    def forget(d, z0, kk, kh, km, part):
        cols = slice(part * part_w, (part + 1) * part_w)
        fr = proj(z0 + cols.start, z0 + cols.stop)
        lbd = lb[d:d + 1, cols]
        f = lbd + (1.0 - lbd) * _sigmoid(fr)
        g2 = jnp.log(f) * LOG2E
        hi = g2.astype(BF16)
        put_heads(kk, 1.0 - f, part)
        put_heads(kh, hi, part)
        put_heads(km, g2 - hi.astype(F32), part)

    def plain(kind, z0, part, silu):
        val = proj(z0 + part * part_w, z0 + (part + 1) * part_w)
        put_heads(kind, val * _sigmoid(val) if silu else val, part)

    lo = lax.broadcasted_iota(jnp.int32, (h.shape[0], LANES), 1) < LANES // 2

    def queries(part):
        aq = proj(Z_AQ + part * part_w, Z_AQ + (part + 1) * part_w)
        for m in range(part_w // LANES):
            cols = slice(m * LANES, (m + 1) * LANES)
            aq_ref[0, :, part * part_w + m * LANES:part * part_w + (m + 1) * LANES] = (
                _half_norm(aq[:, cols], qg_ref[...], lo).astype(BF16))

    def keys_values():
        akv = proj(Z_AK, Z_I)
        ak_ref[0] = _half_norm(akv[:, :KV_WIDTH], kg_ref[...], lo).astype(BF16).T
        av_ref[0] = akv[:, KV_WIDTH:].astype(BF16)

    forget(0, Z_FF, HK_KF, HK_GHF, HK_GMF, 0)
    plain(HK_I, Z_I, 0, False)
    forget(0, Z_FF, HK_KF, HK_GHF, HK_GMF, 1)
    plain(HK_I, Z_I, 1, False)
    forget(1, Z_FB, HK_KB, HK_GHB, HK_GMB, 0)
    keys_values()
    forget(1, Z_FB, HK_KB, HK_GHB, HK_GMB, 1)
    plain(HK_Q, Z_Q, 0, True)
    queries(0)
    plain(HK_Q, Z_Q, 1, True)
    queries(1)
    plain(HK_G, Z_G, 0, True)
    plain(HK_G, Z_G, 1, True)


def _mixin(x, sh, sc, ng, w, hgrn_lb, qg, kg, layer):
    b, s, d = x.shape
    tm = MIX_TM
    vec = pl.BlockSpec((1, 1, d), lambda i, j: (i, 0, 0))
    row = lambda n: pl.BlockSpec((1, n), lambda i, j: (0, 0))
    rows3 = lambda n: pl.BlockSpec((1, tm, n), lambda i, j: (i, j, 0))
    n_hg = N_HK * HG_HEADS
    return pl.pallas_call(
        functools.partial(_mixin_kernel, layer=layer),
        grid=(b, s // tm),
        in_specs=[pl.BlockSpec((1, tm, d), lambda i, j: (i, j, 0)),
                  vec, vec, row(d),
                  pl.BlockSpec(w.shape, lambda i, j: (0, 0)),
                  pl.BlockSpec(hgrn_lb.shape, lambda i, j: (0, 0, 0)),
                  row(LANES), row(LANES)],
        out_specs=[pl.BlockSpec((1, n_hg, tm, HG_DIM), lambda i, j: (i, 0, j, 0)),
                   rows3(ATT_WIDTH),
                   pl.BlockSpec((1, KV_WIDTH, tm), lambda i, j: (i, 0, j)),
                   rows3(KV_WIDTH)],
        out_shape=[jax.ShapeDtypeStruct((b, n_hg, s, HG_DIM), BF16),
                   jax.ShapeDtypeStruct((b, s, ATT_WIDTH), BF16),
                   jax.ShapeDtypeStruct((b, KV_WIDTH, s), BF16),
                   jax.ShapeDtypeStruct((b, s, KV_WIDTH), BF16)],
        compiler_params=_cparams(("arbitrary", "arbitrary")),
        name="mix_in",
    )(x, sh, sc, ng, w, hgrn_lb, qg, kg)


SMALL_W = (1, 2, 4, 8)
GROUP = PACKED_ROWS
FINISH_CHUNKS = 8
PAIRS_PER_BODY = 32


def _hgrn_consts(c):
    t = np.arange(c)[:, None]
    s = np.arange(c)[None, :]
    x = t ^ s
    lev = np.where(x > 0, np.floor(np.log2(np.maximum(x, 1))), -1).astype(np.int32)
    lev_f = np.where(t > s, lev, np.where(t == s, -1, -2)).astype(np.int32)
    lev_b = lev_f.T.copy()

    def exponent_rows(w, rev):
        m = np.zeros((GROUP, GROUP), np.float32)
        for r in range(GROUP):
            b0 = (r // (2 * w)) * 2 * w
            if not rev:
                ref = b0 + w - 1
                lo_u, hi_u = (ref + 1, r) if r > ref else (r + 1, ref)
            else:
                ref = b0 + w
                lo_u, hi_u = (r, ref - 1) if r < ref else (ref, r - 1)
            m[r, lo_u:hi_u + 1] = 1.0
        return m

    cums, smalls = [], []
    for rev in (False, True):
        cum = ((s >= t) if rev else (s <= t)).astype(np.float32)
        cums.append(np.concatenate([cum, cum], axis=1))
        rows = np.concatenate([exponent_rows(w, rev) for w in SMALL_W])
        smalls.append(np.concatenate([rows, rows], axis=1))
    return (jnp.asarray(np.stack([lev_f, lev_b])),
            jnp.asarray(np.stack(cums), dtype=BF16),
            jnp.asarray(np.stack(smalls), dtype=BF16))


def _neg_abs(x):
    bits = lax.bitcast_convert_type(x, jnp.uint32) | jnp.uint32(0x80000000)
    return lax.bitcast_convert_type(bits, F32)


def _ref_rows(g_cum, w, rev, c):
    idx = w if rev else w - 1
    g3 = g_cum.reshape(c // (2 * w), 2 * w, LANES)
    return jnp.broadcast_to(g3[:, idx:idx + 1, :], g3.shape).reshape(c, LANES)


def _hgrn_stage_a(tcum, tsmall, gh, gm, gc_ref, ge_ref, c):
    hm = jnp.concatenate([gh, gm], axis=0)
    gc_ref[...] = jnp.dot(tcum, hm, preferred_element_type=F32)
    yield
    per_tile = MXU_COLS // LANES
    for t in range(c // GROUP // per_tile):
        groups = [slice(g * GROUP, (g + 1) * GROUP) for g in range(t * per_tile, (t + 1) * per_tile)]
        wide = jnp.concatenate([jnp.concatenate([gh[g] for g in groups], axis=1),
                                jnp.concatenate([gm[g] for g in groups], axis=1)], axis=0)
        ge_ref[:, t * MXU_COLS:(t + 1) * MXU_COLS] = jnp.dot(tsmall, wide,
                                                              preferred_element_type=F32)
        yield


def _hgrn_stage_b(q, k, v, gc_ref, ge_ref, lev, st_ref, ab_ref, oi_ref, keep, rev, c):
    tn = (((0,), (0,)), ((), ()))

    def transposed(x):
        return jnp.where(keep, x.astype(BF16).T, jnp.zeros((), BF16))

    g_cum = gc_ref[...]
    g_last = g_cum[0:1, :] if rev else g_cum[c - 1:c, :]
    qf, kf = q.astype(F32), k.astype(F32)

    st = st_ref[...]
    qi = (qf * jnp.exp2(g_cum)).astype(BF16)
    oi_ref[...] = jnp.dot(qi, transposed(st), preferred_element_type=F32)
    kd = (kf * jnp.exp2(g_last - g_cum)).astype(BF16)
    u_t = lax.dot_general(v, kd, tn, preferred_element_type=F32)
    st_ref[...] = jnp.exp2(g_last) * st + u_t
    yield

    w = c // 2
    while w >= GROUP:
        x = jnp.exp2(_neg_abs(g_cum - _ref_rows(g_cum, w, rev, c)))
        q_rows, k_rows = [], []
        for b in range(c // (2 * w)):
            lo_half = slice(b * 2 * w, b * 2 * w + w)
            hi_half = slice(b * 2 * w + w, (b + 1) * 2 * w)
            qs = lo_half if rev else hi_half
            q_rows.append(qf[qs] * x[qs])
            k_rows.extend([kf[lo_half], kf[hi_half] * x[hi_half]] if rev
                          else [kf[lo_half] * x[lo_half], kf[hi_half]])
        p = jnp.dot(jnp.concatenate(q_rows, axis=0).astype(BF16),
                    transposed(jnp.concatenate(k_rows, axis=0)),
                    preferred_element_type=F32).astype(BF16)
        for b in range(c // (2 * w)):
            lo_half = slice(b * 2 * w, b * 2 * w + w)
            hi_half = slice(b * 2 * w + w, (b + 1) * 2 * w)
            qs, ks = (lo_half, hi_half) if rev else (hi_half, lo_half)
            ab_ref[qs, ks] = p[b * w:(b + 1) * w, ks]
        w //= 2
        yield

    a = jnp.where(lev == -1, jnp.dot(q, transposed(k), preferred_element_type=F32), 0.0)
    for li in range(len(SMALL_W)):
        e = jnp.concatenate([ge_ref[li * GROUP:(li + 1) * GROUP, g * LANES:(g + 1) * LANES]
                             for g in range(c // GROUP)], axis=0)
        x = jnp.exp2(e)
        p = jnp.dot((qf * x).astype(BF16), transposed(kf * x), preferred_element_type=F32)
        a = jnp.where(lev == li, p, a)
        yield
    a = a.astype(BF16)
    for b in range(c // GROUP):
        blk = slice(b * GROUP, (b + 1) * GROUP)
        ab_ref[blk, blk] = a[blk, blk]
    yield


def _interleave(main, fill):
    for _ in main:
        next(fill, None)
    for _ in fill:
        pass


def _hgrn_kernel(q_ref, i_ref, sg_ref, kf_ref, kb_ref, ghf_ref, gmf_ref, ghb_ref, gmb_ref,
                 ng_ref, lev_ref, tc_ref, ts_ref, o_ref,
                 acc_ref, st_ref, ab_ref, oi_ref, gc_ref, ge_ref, *, c, nc):
    for ref in (acc_ref, st_ref, ab_ref, oi_ref):
        ref[...] = jnp.zeros_like(ref)
    k_refs = (kf_ref, kb_ref)
    g_refs = ((ghf_ref, gmf_ref), (ghb_ref, gmb_ref))
    keep = pl.program_id(0) >= 0

    def rows_of(p, d):
        j = p if d == 0 else nc - 1 - p
        return pl.ds(pl.multiple_of(j * c, c), c)

    def stage_a(p, slot, d):
        rows = rows_of(p, d)
        return _hgrn_stage_a(tc_ref[d], ts_ref[d], g_refs[d][0][0, 0, rows, :],
                             g_refs[d][1][0, 0, rows, :], gc_ref.at[slot, d], ge_ref.at[slot, d], c)

    def stage_t(p, d):
        rows = rows_of(p, d)
        acc_ref[rows, :] += oi_ref[d] + jnp.dot(ab_ref[d], i_ref[0, 0, rows, :],
                                                preferred_element_type=F32)

    def stage_b(p, slot, d):
        rows = rows_of(p, d)
        return _hgrn_stage_b(q_ref[0, 0, rows, :], k_refs[d][0, 0, rows, :], i_ref[0, 0, rows, :],
                             gc_ref.at[slot, d], ge_ref.at[slot, d], lev_ref[d], st_ref.at[d],
                             ab_ref.at[d], oi_ref.at[d], keep, d == 1, c)

    def pair(p, slot):
        p_next, p_prev = jnp.minimum(p + 1, nc - 1), jnp.maximum(p - 1, 0)
        for d in range(2):
            stage_t(p_prev, d)
            _interleave(stage_b(p, slot, d), stage_a(p_next, 1 - slot, d))

    def pairs(i, carry):
        for u in range(PAIRS_PER_BODY):
            pair(PAIRS_PER_BODY * i + u, u % 2)
        return carry

    for d in range(2):
        for _ in stage_a(0, 0, d):
            pass
    lax.fori_loop(0, nc // PAIRS_PER_BODY, pairs, 0)
    for d in range(2):
        stage_t(nc - 1, d)

    ng = ng_ref[...]
    rows_fin = FINISH_CHUNKS * c

    def finish(j, carry):
        rows = pl.ds(pl.multiple_of(j * rows_fin, rows_fin), rows_fin)
        o = acc_ref[rows, :]
        ms = jnp.mean(o * o, axis=-1, keepdims=True)
        o = o * lax.rsqrt(ms + EPS) * ng
        o_ref[0, rows, :] = (o * sg_ref[0, 0, rows, :].astype(F32)).astype(o_ref.dtype)
        return carry

    lax.fori_loop(0, nc // FINISH_CHUNKS, finish, 0)


def _hgrn(hg, norm_g):
    b, _, s, _ = hg.shape
    c = HG_CHUNK
    nc = s // c
    assert PAIRS_PER_BODY % 2 == 0 and nc % PAIRS_PER_BODY == 0 and c % (2 * GROUP) == 0
    lev, tcum, tsmall = _hgrn_consts(c)
    n_small = len(SMALL_W) * GROUP

    def kind(k):
        return pl.BlockSpec((1, 1, s, HG_DIM), lambda i, h: (i, k * HG_HEADS + h, 0, 0))

    const3 = lambda a: pl.BlockSpec(a.shape, lambda i, h: (0, 0, 0))
    return pl.pallas_call(
        functools.partial(_hgrn_kernel, c=c, nc=nc),
        grid=(b, HG_HEADS),
        in_specs=[kind(HK_Q), kind(HK_I), kind(HK_G), kind(HK_KF), kind(HK_KB),
                  kind(HK_GHF), kind(HK_GMF), kind(HK_GHB), kind(HK_GMB),
                  pl.BlockSpec((1, LANES), lambda i, h: (0, h)),
                  const3(lev), const3(tcum), const3(tsmall)],
        out_specs=pl.BlockSpec((1, s, LANES), lambda i, h: (i, 0, h)),
        out_shape=jax.ShapeDtypeStruct((b, s, HG_WIDTH), BF16),
        scratch_shapes=[pltpu.VMEM((s, HG_DIM), F32),
                        pltpu.VMEM((2, HG_DIM, HG_DIM), F32),
                        pltpu.VMEM((2, c, c), BF16),
                        pltpu.VMEM((2, c, HG_DIM), F32),
                        pltpu.VMEM((2, 2, c, LANES), F32),
                        pltpu.VMEM((2, 2, n_small, (c // GROUP) * LANES), F32)],
        compiler_params=_cparams(("arbitrary", "arbitrary")),
        name="hgrn",
    )(hg, hg, hg, hg, hg, hg, hg, hg, hg, norm_g, lev, tcum, tsmall)


def _attn_kernel(sink_ref, q_ref, kp_ref, ko_ref, kn_ref, vp_ref, vo_ref, vn_ref,
                 bias_ref, o_ref, lg_ref, *, nb):
    n = pl.program_id(1)
    half = LANES // 2
    nk = 3 * BLOCK
    kt = jnp.concatenate([kp_ref[0], ko_ref[0], kn_ref[0]], axis=1)
    vb = jnp.concatenate([vp_ref[0], vo_ref[0], vn_ref[0]], axis=0)
    lo = lax.broadcasted_iota(jnp.int32, (BLOCK, LANES), 1) < half
    lo_v = lax.broadcasted_iota(jnp.int32, vb.shape, 1) < half
    zero_v = jnp.zeros_like(vb)
    zero_k = jnp.zeros((half, nk), kt.dtype)

    k_both = tuple(
        jnp.concatenate([jnp.concatenate([kj, zero_k], axis=0),
                         jnp.concatenate([zero_k, kj], axis=0)], axis=1)
        for kj in (kt[:half], kt[half:]))
    v0lo, v1hi = jnp.where(lo_v, vb, zero_v), jnp.where(lo_v, zero_v, vb)
    v_both = (jnp.concatenate([v0lo, pltpu.roll(v0lo, half, axis=1)], axis=0),
              jnp.concatenate([pltpu.roll(v1hi, half, axis=1), v1hi], axis=0))
    edge = jnp.where(n == 0, 0, jnp.where(n == nb - 1, 2, 1))

    for m in range(ATT_Q_HEADS // 2):
        lg_ref[m] = jnp.dot(q_ref[0, :, m * LANES:(m + 1) * LANES], k_both[(2 * m) // ATT_GROUP],
                            preferred_element_type=F32)

    for m in range(ATT_Q_HEADS // 2):
        ps, rdens = [], []
        for hh in range(2):
            lg = (lg_ref[m, :, hh * nk:(hh + 1) * nk]
                  + bias_ref[edge, m, hh * BLOCK:(hh + 1) * BLOCK, :])
            sink = sink_ref[2 * m + hh]
            mx = jnp.maximum(jnp.max(lg, axis=-1, keepdims=True), sink)
            p = jnp.exp2(lg - mx)
            rdens.append(1.0 / (jnp.sum(p, axis=-1, keepdims=True) + jnp.exp2(sink - mx)))
            ps.append(p.astype(BF16))
        o = jnp.dot(jnp.concatenate(ps, axis=1), v_both[(2 * m) // ATT_GROUP],
                    preferred_element_type=F32)
        o_ref[0, :, m * LANES:(m + 1) * LANES] = (
            o * jnp.where(lo, rdens[0], rdens[1])).astype(o_ref.dtype)


def _attn(aq, ak, av, sink2, bias):
    b, s, _ = aq.shape
    nb = s // BLOCK
    assert nb >= 2

    def neighbour(n, shift):
        return jnp.clip(n + shift, 0, nb - 1)

    kt = lambda shift: pl.BlockSpec((1, KV_WIDTH, BLOCK), lambda i, n: (i, 0, neighbour(n, shift)))
    v = lambda shift: pl.BlockSpec((1, BLOCK, KV_WIDTH), lambda i, n: (i, neighbour(n, shift), 0))
    return pl.pallas_call(
        functools.partial(_attn_kernel, nb=nb),
        grid=(b, nb),
        in_specs=[pl.BlockSpec(memory_space=pltpu.SMEM),
                  pl.BlockSpec((1, BLOCK, ATT_WIDTH), lambda i, n: (i, n, 0)),
                  kt(-1), kt(0), kt(1), v(-1), v(0), v(1),
                  pl.BlockSpec(bias.shape, lambda i, n: (0, 0, 0, 0))],
        out_specs=pl.BlockSpec((1, BLOCK, ATT_WIDTH), lambda i, n: (i, n, 0)),
        out_shape=jax.ShapeDtypeStruct((b, s, ATT_WIDTH), BF16),
        scratch_shapes=[pltpu.VMEM((ATT_Q_HEADS // 2, BLOCK, 2 * 3 * BLOCK), F32)],
        compiler_params=_cparams(("arbitrary", "arbitrary")),
        name="attn",
    )(sink2, aq, ak, ak, ak, av, av, av, bias)


def _mixin_col_map():
    hw = HG_WIDTH
    att = ATT_WIDTH + 2 * KV_WIDTH
    return ((Z_FF, hw, 2 * hw), (Z_Q, 0, hw), (Z_G, 4 * hw, hw), (Z_AQ, 5 * hw, att),
            (Z_I, 3 * hw, hw))


def kernel(x, c, w_ada, b_ada, norm_g, w_ffn1_in, w_ffn1_out, w_ffn2_in, w_ffn2_out,
           w_mix_in, w_mix_out, hgrn_lb, hgrn_norm_g, qk_norm_g, attn_sink, rel_bias):
    b, s, d = x.shape
    depth = w_ada.shape[0]
    for l in range(depth):
        mods, bias = _ada_bias(c.astype(F32), w_ada[l], b_ada[l][None, :], rel_bias)
        sh1, sc1, g1, sh2, sc2, g2, sh3, sc3, g3 = [
            mods[:, i * d:(i + 1) * d][:, None, :] for i in range(N_MOD)]
        x, w2_in, w2_out, w_mi, w_mo = _ffn_first(
            x, sh1, sc1, g1, norm_g[l, 0][None, :], w_ffn1_in, w_ffn1_out, l,
            casts=((w_ffn2_in, None, 1), (w_ffn2_out, None, 2),
                   (w_mix_in, _mixin_col_map(), 1), (w_mix_out, None, 1)))
        qg = (jnp.tile(qk_norm_g[l, 0], 2) * (LOG2E / math.sqrt(ATT_HEAD_DIM)))[None, :]
        kg = jnp.tile(qk_norm_g[l, 1], 2)[None, :]
        hg, aq, ak, av = _mixin(x, sh2, sc2, norm_g[l, 1][None, :], w_mi, hgrn_lb, qg, kg, l)
        o_hg = _hgrn(hg, hgrn_norm_g[l][None, :])
        att = _attn(aq, ak, av, attn_sink[l] * LOG2E, bias)
        x = _ffn_mix(o_hg, att, g2, w_mo, x, sh3, sc3, g3, norm_g[l, 2][None, :], w2_in, w2_out)
    return x
```
